```python
import math
import jax, jax.numpy as jnp
from jax import lax
import numpy as np

D_MODEL = 1024
BATCH = 8
SEQ = 2048
DEPTH = 2

N_A = DEPTH // 2
N_B = DEPTH - N_A

RET_HEADS = 4
RET_QK_DIM = 128
RET_V_DIM = 192
RET_CHUNK = 128
RET_QK_W = RET_HEADS * RET_QK_DIM
RET_V_W = RET_HEADS * RET_V_DIM

MOBA_HEADS = 12
MOBA_HEAD_DIM = 64
MOBA_W = MOBA_HEADS * MOBA_HEAD_DIM
MOBA_BLOCK = 256
MOBA_TOPK = 3
MOBA_QCHUNK = 8

MEM_LEN = 256
MEM_HEADS = 4
MEM_HEAD_DIM = 64
MEM_W = MEM_HEADS * MEM_HEAD_DIM

A_IN_W = 2 * RET_QK_W + 2 * RET_V_W + MEM_W
A_MIX_W = RET_V_W + MEM_W
B_IN_W = MOBA_W + MEM_W
B_MIX_W = MOBA_W + MEM_W
D_FF = 4 * D_MODEL
EPS = 1e-6
NEG_INF = -1e30

kernel_name = "yoco_retention_moba_hybrid"


def alibi_slopes(n):
    def pow2(m):
        return [2.0 ** (-8.0 * (i + 1) / m) for i in range(m)]
    p = 2 ** int(math.floor(math.log2(n)))
    s = pow2(p)
    if p < n:
        s = s + pow2(2 * p)[0::2][: n - p]
    return np.asarray(s, dtype=np.float32)


def rmsnorm(x, g):
    xf = x.astype(jnp.float32)
    y = xf * lax.rsqrt(jnp.mean(xf * xf, axis=-1, keepdims=True) + EPS)
    return (y * g.astype(jnp.float32)).astype(x.dtype)


def split_heads(x, n_heads):
    b, s, w = x.shape
    return x.reshape(b, s, n_heads, w // n_heads).transpose(0, 2, 1, 3)


def merge_heads(x):
    b, h, s, d = x.shape
    return x.transpose(0, 2, 1, 3).reshape(b, s, h * d)


def memory_attention(q_m, mem, w_mkv):
    mk, mv = jnp.split(mem @ w_mkv, 2, axis=-1)
    q = split_heads(q_m, MEM_HEADS)
    k = split_heads(mk, MEM_HEADS)
    v = split_heads(mv, MEM_HEADS)
    s = jnp.einsum("bhsd,bhmd->bhsm", q, k).astype(jnp.float32) * (MEM_HEAD_DIM ** -0.5)
    p = jax.nn.softmax(s, axis=-1).astype(v.dtype)
    return merge_heads(jnp.einsum("bhsm,bhmd->bhsd", p, v))


def retention(q, k, v):
    b, h, s, dk = q.shape
    dv = v.shape[-1]
    c = RET_CHUNK
    nc = s // c
    q = q.astype(jnp.float32)
    k = k.astype(jnp.float32) * (dk ** -0.5)
    v = v.astype(jnp.float32)
    log_g = jnp.log1p(-jnp.exp2(-5.0 - jnp.arange(h, dtype=jnp.float32)))
    qc = q.reshape(b, h, nc, c, dk)
    kc = k.reshape(b, h, nc, c, dk)
    vc = v.reshape(b, h, nc, c, dv)
    pos = jnp.arange(c, dtype=jnp.float32)
    rel = pos[:, None] - pos[None, :]
    d_intra = jnp.where(rel >= 0, jnp.exp(log_g[:, None, None] * jnp.maximum(rel, 0.0)), 0.0)
    inner = jnp.einsum("bhncd,bhnkd->bhnck", qc, kc) * d_intra[None, :, None]
    inner_out = jnp.einsum("bhnck,bhnkv->bhncv", inner, vc)
    zeta = jnp.exp(log_g[:, None] * (c - 1.0 - pos))
    kv_chunk = jnp.einsum("bhnkd,bhnkv->bhndv", kc * zeta[None, :, None, :, None], vc)
    chunk_decay = jnp.exp(log_g * c)[None, :, None, None]

    def step(state, kv_n):
        return state * chunk_decay + kv_n, state

    _, prev = lax.scan(step, jnp.zeros((b, h, dk, dv), jnp.float32), jnp.moveaxis(kv_chunk, 2, 0))
    prev = jnp.moveaxis(prev, 0, 2)
    xi = jnp.exp(log_g[:, None] * (pos + 1.0))
    cross = jnp.einsum("bhncd,bhndv->bhncv", qc * xi[None, :, None, :, None], prev)
    return (inner_out + cross).reshape(b, h, s, dv)


def moba_attention(q, k, v, slopes):
    b, h, s, dh = q.shape
    bs = MOBA_BLOCK
    nb = -(-s // bs)
    sp = nb * bs
    pad = sp - s
    if pad:
        widths = ((0, 0), (0, 0), (0, pad), (0, 0))
        q, k, v = jnp.pad(q, widths), jnp.pad(k, widths), jnp.pad(v, widths)
    kb = k.reshape(b, h, nb, bs, dh)
    vb = v.reshape(b, h, nb, bs, dh)
    t = jnp.arange(sp, dtype=jnp.int32)
    qblk = t // bs
    own = jnp.broadcast_to(qblk, (b, h, sp))[..., None]
    kk = min(MOBA_TOPK, nb - 1)
    if kk > 0:
        kmean = jnp.mean(kb.astype(jnp.float32), axis=3)
        gate = jnp.einsum("bhsd,bhnd->bhsn", q.astype(jnp.float32), kmean)
        past = jnp.arange(nb, dtype=jnp.int32)[None, :] < qblk[:, None]
        gate = jnp.where(past, gate, NEG_INF)
        _, top = lax.top_k(gate, kk)
        idx = jnp.concatenate([top.astype(jnp.int32), own], axis=-1)
        valid = jnp.concatenate([jnp.arange(kk, dtype=jnp.int32)[None, :] < qblk[:, None],
                                 jnp.ones((sp, 1), bool)], axis=-1)
    else:
        idx = own
        valid = jnp.ones((sp, 1), bool)
    r = idx.shape[-1]
    qc_n = MOBA_QCHUNK
    nq = sp // qc_n

    def to_chunks(a):
        a = a.reshape(b, h, nq, qc_n, *a.shape[3:])
        return jnp.moveaxis(a, 2, 0)

    bi = jnp.arange(b)[:, None, None, None]
    hi = jnp.arange(h)[None, :, None, None]
    scale = dh ** -0.5
    blk_pos = jnp.arange(bs, dtype=jnp.int32)

    def one_chunk(args):
        qc, ic, tc, vc = args
        kg = kb[bi, hi, ic]
        vg = vb[bi, hi, ic]
        sc = jnp.einsum("bhqd,bhqrkd->bhqrk", qc, kg).astype(jnp.float32) * scale
        kpos = ic[..., None] * bs + blk_pos
        dist = tc[None, None, :, None, None] - kpos
        sc = sc - slopes[None, :, None, None, None] * dist.astype(jnp.float32)
        mask = (dist >= 0) & vc[None, None, :, :, None]
        sc = jnp.where(mask, sc, NEG_INF)
        p = jax.nn.softmax(sc.reshape(b, h, qc_n, r * bs), axis=-1).reshape(sc.shape)
        return jnp.einsum("bhqrk,bhqrkd->bhqd", p.astype(vg.dtype), vg)

    out = lax.map(one_chunk, (to_chunks(q), to_chunks(idx), t.reshape(nq, qc_n), valid.reshape(nq, qc_n, r)))
    out = jnp.moveaxis(out, 0, 2).reshape(b, h, sp, dh)
    return out[:, :, :s]


def retention_mixer(hn, mem, w_in, gn_gain, w_out, w_mkv):
    proj = hn @ w_in
    q_r, k_r, v_r, g_r, q_m = jnp.split(
        proj, [RET_QK_W, 2 * RET_QK_W, 2 * RET_QK_W + RET_V_W, 2 * RET_QK_W + 2 * RET_V_W], axis=-1)
    y = retention(split_heads(q_r, RET_HEADS), split_heads(k_r, RET_HEADS), split_heads(v_r, RET_HEADS))
    y = y * lax.rsqrt(jnp.mean(y * y, axis=-1, keepdims=True) + EPS)
    y = (merge_heads(y) * gn_gain.astype(jnp.float32)).astype(hn.dtype)
    y = jax.nn.silu(g_r) * y
    m = memory_attention(q_m, mem, w_mkv)
    return jnp.concatenate([y, m], axis=-1) @ w_out


def moba_mixer(hn, mem, k_sh, v_sh, slopes, w_in, w_out, w_mkv):
    q_b, q_m = jnp.split(hn @ w_in, [MOBA_W], axis=-1)
    a = merge_heads(moba_attention(split_heads(q_b, MOBA_HEADS), k_sh, v_sh, slopes))
    m = memory_attention(q_m, mem, w_mkv)
    return jnp.concatenate([a, m], axis=-1) @ w_out


def squared_relu_mlp(hn, w_up, w_down):
    return jnp.square(jax.nn.relu(hn @ w_up)) @ w_down


def setup_inputs(seed: int = 0) -> dict:
    key = jax.random.key(seed)
    ks = jax.random.split(key, 16)

    def w(k, shape):
        return jax.random.normal(k, shape, jnp.float32) * (shape[-2] ** -0.5)

    def gain(k, shape):
        return 1.0 + 0.1 * jax.random.normal(k, shape, jnp.float32)

    return {
        "x": jax.random.normal(ks[0], (BATCH, SEQ, D_MODEL), jnp.float32),
        "mem": jax.random.normal(ks[1], (BATCH, MEM_LEN, D_MODEL), jnp.float32),
        "w_in_a": w(ks[2], (N_A, D_MODEL, A_IN_W)),
        "ret_norm_gain": gain(ks[3], (N_A, RET_V_W)),
        "w_out_a": w(ks[4], (N_A, A_MIX_W, D_MODEL)),
        "kv_norm_gain": gain(ks[5], (D_MODEL,)),
        "w_kv_shared": w(ks[6], (D_MODEL, 2 * MOBA_W)),
        "w_in_b": w(ks[7], (N_B, D_MODEL, B_IN_W)),
        "w_out_b": w(ks[8], (N_B, B_MIX_W, D_MODEL)),
        "w_mem_kv": w(ks[9], (DEPTH, D_MODEL, 2 * MEM_W)),
        "norm_pre_mix": gain(ks[10], (DEPTH, D_MODEL)),
        "norm_post_mix": gain(ks[11], (DEPTH, D_MODEL)),
        "norm_pre_mlp": gain(ks[12], (DEPTH, D_MODEL)),
        "norm_post_mlp": gain(ks[13], (DEPTH, D_MODEL)),
        "w_up": w(ks[14], (DEPTH, D_MODEL, D_FF)),
        "w_down": w(ks[15], (DEPTH, D_FF, D_MODEL)),
    }


def reference(x, mem, w_in_a, ret_norm_gain, w_out_a, kv_norm_gain, w_kv_shared, w_in_b, w_out_b,
              w_mem_kv, norm_pre_mix, norm_post_mix, norm_pre_mlp, norm_post_mlp, w_up, w_down):
    slopes = jnp.asarray(alibi_slopes(MOBA_HEADS))
    h = x
    k_sh = None
    v_sh = None
    for l in range(DEPTH):
        hn = rmsnorm(h, norm_pre_mix[l])
        if l < N_A:
            mix = retention_mixer(hn, mem, w_in_a[l], ret_norm_gain[l], w_out_a[l], w_mem_kv[l])
        else:
            if l == N_A:
                k_all, v_all = jnp.split(rmsnorm(h, kv_norm_gain) @ w_kv_shared, 2, axis=-1)
                k_sh = split_heads(k_all, MOBA_HEADS)
                v_sh = split_heads(v_all, MOBA_HEADS)
            j = l - N_A
            mix = moba_mixer(hn, mem, k_sh, v_sh, slopes, w_in_b[j], w_out_b[j], w_mem_kv[l])
        h = h + rmsnorm(mix, norm_post_mix[l])
        y = squared_relu_mlp(rmsnorm(h, norm_pre_mlp[l]), w_up[l], w_down[l])
        h = h + rmsnorm(y, norm_post_mlp[l])
    return h
```

```python
import functools
import math

import numpy as np
import jax
import jax.numpy as jnp
from jax import lax
from jax.experimental import pallas as pl
from jax.experimental.pallas import tpu as pltpu

F32 = jnp.float32
BF16 = jnp.bfloat16

EPS = 1e-6
NEG_INF = -1e30

LANES = 128
V7X_VMEM_LIMIT_BYTES = 56 * 1024 * 1024

RET_HEADS = 4
RET_QK_DIM = 128
RET_V_DIM = 192
RET_V_PAD = 256
RET_CHUNK = 128
MOBA_HEADS = 12
MOBA_HEAD_DIM = 64
MOBA_BLOCK = 256
MOBA_TOPK = 3
MEM_HEADS = 4
MEM_HEAD_DIM = 64

ROW_TILE = 512
RET_ROW_TILE = 256
FF_CHUNK = 1024


def _rms_scale(x):
    return lax.rsqrt(jnp.mean(x * x, axis=-1, keepdims=True) + EPS)


def _dot(a, b):
    return jnp.dot(a, b, preferred_element_type=F32)


def _dot_nt(a, b):
    return lax.dot_general(a, b, (((1,), (1,)), ((), ())), preferred_element_type=F32)


def _dot_tn(a, b):
    return lax.dot_general(a, b, (((0,), (0,)), ((), ())), preferred_element_type=F32)


def _resident(shape):
    zeros = (0,) * len(shape)
    return pl.BlockSpec(shape, lambda *_: zeros, pipeline_mode=pl.Buffered(1))


def _norm_proj_kernel(x_ref, g_ref, *refs, n_out):
    w_refs, o_refs = refs[:n_out], refs[n_out:]
    x = x_ref[...]
    xhat = x * _rms_scale(x)
    for i in range(n_out):
        hn = (xhat * g_ref[i:i + 1, :]).astype(BF16)
        o_refs[i][...] = _dot(hn, w_refs[i][...]).astype(o_refs[i].dtype)


def _norm_proj(x2d, gains, weights):
    m, d = x2d.shape
    n_out = len(weights)
    return pl.pallas_call(
        functools.partial(_norm_proj_kernel, n_out=n_out),
        grid=(m // ROW_TILE,),
        in_specs=[pl.BlockSpec((ROW_TILE, d), lambda i: (i, 0)),
                  _resident(gains.shape)]
                 + [_resident(w.shape) for w in weights],
        out_specs=[pl.BlockSpec((ROW_TILE, w.shape[1]), lambda i: (i, 0)) for w in weights],
        out_shape=[jax.ShapeDtypeStruct((m, w.shape[1]), BF16) for w in weights],
        compiler_params=pltpu.CompilerParams(
            dimension_semantics=("parallel",), vmem_limit_bytes=V7X_VMEM_LIMIT_BYTES),
        name="norm_proj",
    )(x2d, gains, *weights)


def _mlp_kernel(h_ref, g_pre_ref, g_post_ref, wu_ref, wd_ref, o_ref):
    h = h_ref[...]
    hn = (h * _rms_scale(h) * g_pre_ref[...]).astype(BF16)
    d_ff = wu_ref.shape[1]
    y = jnp.zeros(h.shape, F32)
    for c in range(d_ff // FF_CHUNK):
        cols = slice(c * FF_CHUNK, (c + 1) * FF_CHUNK)
        u = _dot(hn, wu_ref[:, cols])
        a = jnp.square(jnp.maximum(u, 0.0)).astype(BF16)
        y = y + _dot(a, wd_ref[cols, :])
    o_ref[...] = h + y * _rms_scale(y) * g_post_ref[...]


def _mlp(h2d, g_pre, g_post, w_up, w_down):
    m, d = h2d.shape
    row_spec = pl.BlockSpec((ROW_TILE, d), lambda i: (i, 0))
    return pl.pallas_call(
        _mlp_kernel,
        grid=(m // ROW_TILE,),
        in_specs=[row_spec, _resident(g_pre.shape), _resident(g_post.shape),
                  _resident(w_up.shape), _resident(w_down.shape)],
        out_specs=row_spec,
        out_shape=jax.ShapeDtypeStruct((m, d), F32),
        compiler_params=pltpu.CompilerParams(
            dimension_semantics=("parallel",), vmem_limit_bytes=V7X_VMEM_LIMIT_BYTES),
        name="mlp",
    )(h2d, g_pre, g_post, w_up, w_down)


def _retention_kernel(q_ref, k_ref, v_ref, g_ref, gain_ref, o_ref, state_ref):
    c = RET_CHUNK

    @pl.when(pl.program_id(1) == 0)
    def _():
        state_ref[...] = jnp.zeros(state_ref.shape, F32)

    row = lax.broadcasted_iota(jnp.int32, (c, c), 0)
    col = lax.broadcasted_iota(jnp.int32, (c, c), 1)
    rel = (row - col).astype(F32)
    pos = lax.broadcasted_iota(jnp.int32, (c, 1), 0).astype(F32)
    qk_scale = RET_QK_DIM ** -0.5

    for h in range(RET_HEADS):
        log_g = float(np.log1p(-np.exp2(np.float32(-5.0 - h))))
        d_intra = jnp.where(rel >= 0, jnp.exp(log_g * jnp.maximum(rel, 0.0)), 0.0) * qk_scale
        zeta = jnp.exp(log_g * (c - 1.0 - pos)) * qk_scale
        xi = jnp.exp(log_g * (pos + 1.0))
        chunk_decay = float(np.exp(np.float32(log_g * c)))
        gain = gain_ref[:, h * RET_V_PAD:(h + 1) * RET_V_PAD]
        qk_cols = slice(h * RET_QK_DIM, (h + 1) * RET_QK_DIM)
        v_cols = slice(h * RET_V_PAD, (h + 1) * RET_V_PAD)
        for ci in range(RET_ROW_TILE // c):
            rows = slice(ci * c, (ci + 1) * c)
            q = q_ref[rows, qk_cols]
            k = k_ref[rows, qk_cols]
            v = v_ref[rows, v_cols]
            state = state_ref[h]
            s = _dot_nt(q, k) * d_intra
            y = _dot(s.astype(BF16), v) + _dot(q, state.astype(BF16)) * xi
            kz = (k.astype(F32) * zeta).astype(BF16)
            state_ref[h] = state * chunk_decay + _dot_tn(kz, v)
            ms = jnp.sum(y * y, axis=-1, keepdims=True) * (1.0 / RET_V_DIM)
            yn = y * lax.rsqrt(ms + EPS) * gain
            g = g_ref[rows, v_cols].astype(F32)
            o_ref[rows, v_cols] = (g / (1.0 + jnp.exp(-g)) * yn).astype(o_ref.dtype)


def _retention(proj, gain_pad, batch, seq):
    m = proj.shape[0]
    steps = seq // RET_ROW_TILE
    qk_w = RET_HEADS * RET_QK_DIM
    v_w = RET_HEADS * RET_V_PAD
    row = lambda b, s: b * steps + s
    return pl.pallas_call(
        _retention_kernel,
        grid=(batch, steps),
        in_specs=[pl.BlockSpec((RET_ROW_TILE, qk_w), lambda b, s: (row(b, s), 0)),
                  pl.BlockSpec((RET_ROW_TILE, qk_w), lambda b, s: (row(b, s), 1)),
                  pl.BlockSpec((RET_ROW_TILE, v_w), lambda b, s: (row(b, s), 1)),
                  pl.BlockSpec((RET_ROW_TILE, v_w), lambda b, s: (row(b, s), 2)),
                  _resident(gain_pad.shape)],
        out_specs=pl.BlockSpec((RET_ROW_TILE, v_w), lambda b, s: (row(b, s), 0)),
        out_shape=jax.ShapeDtypeStruct((m, v_w), BF16),
        scratch_shapes=[pltpu.VMEM((RET_HEADS, RET_QK_DIM, RET_V_PAD), F32)],
        compiler_params=pltpu.CompilerParams(
            dimension_semantics=("parallel", "arbitrary"),
            vmem_limit_bytes=V7X_VMEM_LIMIT_BYTES),
        name="retention",
    )(proj, proj, proj, proj, gain_pad)


def _mix_out_kernel(a_ref, qm_ref, mem_ref, wmkv_ref, wout_ref, h_ref, gain_ref, o_ref,
                    mk_ref, mv_ref):
    mem_w = MEM_HEADS * MEM_HEAD_DIM

    @pl.when(pl.program_id(1) == 0)
    def _():
        mkv = _dot(mem_ref[...].astype(BF16), wmkv_ref[...])
        mk_ref[...] = mkv[:, :mem_w].astype(BF16)
        mv_ref[...] = mkv[:, mem_w:].astype(BF16)

    lane = lax.broadcasted_iota(jnp.int32, (1, LANES), 1)
    scale = MEM_HEAD_DIM ** -0.5
    pairs = []
    for p in range(mem_w // LANES):
        cols = slice(p * LANES, (p + 1) * LANES)
        qp = qm_ref[:, cols] * jnp.asarray(scale, BF16)
        mkp = mk_ref[:, cols]
        mvp = mv_ref[:, cols]
        heads = []
        for hh in range(LANES // MEM_HEAD_DIM):
            in_head = (lane >= hh * MEM_HEAD_DIM) & (lane < (hh + 1) * MEM_HEAD_DIM)
            qh = jnp.where(in_head, qp, jnp.zeros_like(qp))
            s = _dot_nt(qh, mkp)
            e = jnp.exp(s - jnp.max(s, axis=-1, keepdims=True))
            o = _dot(e.astype(BF16), mvp) / jnp.sum(e, axis=-1, keepdims=True)
            heads.append(jnp.where(in_head, o, 0.0))
        pairs.append((heads[0] + heads[1]).astype(BF16))
    mem_out = jnp.concatenate(pairs, axis=-1)
    a_w = a_ref.shape[1]
    mix = _dot(a_ref[...], wout_ref[:a_w, :]) + _dot(mem_out, wout_ref[a_w:, :])
    o_ref[...] = h_ref[...] + mix * _rms_scale(mix) * gain_ref[...]


def _mix_out(a2d, qm_src, qm_col_block, mem, w_mkv, w_out, h2d, gain, batch, seq):
    m, d = h2d.shape
    a_w = a2d.shape[1]
    mem_w = MEM_HEADS * MEM_HEAD_DIM
    mem_len = mem.shape[1]
    steps = seq // ROW_TILE
    row = lambda b, s: b * steps + s
    return pl.pallas_call(
        _mix_out_kernel,
        grid=(batch, steps),
        in_specs=[pl.BlockSpec((ROW_TILE, a_w), lambda b, s: (row(b, s), 0)),
                  pl.BlockSpec((ROW_TILE, mem_w), lambda b, s: (row(b, s), qm_col_block)),
                  pl.BlockSpec((None, mem_len, d), lambda b, s: (b, 0, 0)),
                  _resident(w_mkv.shape), _resident(w_out.shape),
                  pl.BlockSpec((ROW_TILE, d), lambda b, s: (row(b, s), 0)),
                  _resident(gain.shape)],
        out_specs=pl.BlockSpec((ROW_TILE, d), lambda b, s: (row(b, s), 0)),
        out_shape=jax.ShapeDtypeStruct((m, d), F32),
        scratch_shapes=[pltpu.VMEM((mem_len, mem_w), BF16), pltpu.VMEM((mem_len, mem_w), BF16)],
        compiler_params=pltpu.CompilerParams(
            dimension_semantics=("parallel", "arbitrary"),
            vmem_limit_bytes=V7X_VMEM_LIMIT_BYTES),
        name="mix_out",
    )(a2d, qm_src, mem, w_mkv, w_out, h2d, gain)


def _moba_kernel(slopes_ref, q_ref, k_ref, v_ref, o_ref, km_ref, cb_ref):
    bs = MOBA_BLOCK
    nb = k_ref.shape[0] // bs
    p = pl.program_id(1)
    j = pl.program_id(2)

    @pl.when(j == 0)
    def _():
        km_ref[...] = jnp.zeros(km_ref.shape, F32)
        for n in range(nb):
            kb = k_ref[n * bs:(n + 1) * bs, :].astype(F32)
            km_ref[n:n + 1, :] = jnp.sum(kb, axis=0, keepdims=True) * (1.0 / bs)

    km = km_ref[...]
    km_hi = km.astype(BF16)
    km_lo = (km - km_hi.astype(F32)).astype(BF16)
    lane = lax.broadcasted_iota(jnp.int32, (1, LANES), 1)
    rowi = lax.broadcasted_iota(jnp.int32, (bs, bs), 0)
    coli = lax.broadcasted_iota(jnp.int32, (bs, bs), 1)
    rel = (coli - rowi).astype(F32)
    causal = coli <= rowi
    q_pair = q_ref[...] * jnp.asarray(MOBA_HEAD_DIM ** -0.5, BF16)
    q_off = pl.multiple_of(j * bs, bs)
    heads = []
    for hh in range(LANES // MOBA_HEAD_DIM):
        slope = slopes_ref[2 * p + hh]
        in_head = (lane >= hh * MOBA_HEAD_DIM) & (lane < (hh + 1) * MOBA_HEAD_DIM)
        qh = jnp.where(in_head, q_pair, jnp.zeros_like(q_pair))
        gate = _dot_nt(qh, km_hi) + _dot_nt(qh, km_lo)
        for n in range(nb):
            gn = gate[:, n:n + 1]
            beats = (gate > gn) | ((gate == gn) & (lane < n))
            cnt = jnp.sum(jnp.where(beats & (lane < j), 1.0, 0.0), axis=-1, keepdims=True)
            dist0 = ((j - n) * bs).astype(F32)
            cb = jnp.where(cnt < float(MOBA_TOPK), -slope * dist0, NEG_INF)
            cb_ref[hh, n] = jnp.broadcast_to(cb, (bs, LANES))
        bias = slope * rel
        s = jnp.where(causal, _dot_nt(qh, k_ref[pl.ds(q_off, bs), :]) + bias, NEG_INF)
        m0 = jnp.max(s, axis=-1, keepdims=True)
        e = jnp.exp(s - m0)
        l0 = jnp.sum(e, axis=-1, keepdims=True)
        acc0 = _dot(e.astype(BF16), v_ref[pl.ds(q_off, bs), :])

        def body(n, carry, qh=qh, bias=bias, hh=hh):
            m_run, l_run, acc = carry
            off = pl.multiple_of(n * bs, bs)
            cbn = cb_ref[hh, n]
            s = _dot_nt(qh, k_ref[pl.ds(off, bs), :]) + bias + jnp.concatenate([cbn, cbn], axis=-1)
            m_new = jnp.maximum(m_run, jnp.max(s, axis=-1, keepdims=True))
            alpha = jnp.exp(m_run - m_new)
            e = jnp.exp(s - m_new)
            l_new = alpha * l_run + jnp.sum(e, axis=-1, keepdims=True)
            acc_new = alpha * acc + _dot(e.astype(BF16), v_ref[pl.ds(off, bs), :])
            return m_new, l_new, acc_new

        _, l_fin, acc_fin = lax.fori_loop(0, j, body, (m0, l0, acc0))
        heads.append(jnp.where(in_head, acc_fin / l_fin, 0.0))
    o_ref[...] = (heads[0] + heads[1]).astype(o_ref.dtype)


def _alibi_slopes(n):
    def pow2(m):
        return [2.0 ** (-8.0 * (i + 1) / m) for i in range(m)]
    p = 2 ** int(math.floor(math.log2(n)))
    s = pow2(p)
    if p < n:
        s = s + pow2(2 * p)[0::2][: n - p]
    return np.asarray(s, dtype=np.float32)


def _moba(qproj3, kv3):
    batch, seq, _ = qproj3.shape
    n_pairs = MOBA_HEADS * MOBA_HEAD_DIM // LANES
    nb = seq // MOBA_BLOCK
    slopes = jnp.asarray(_alibi_slopes(MOBA_HEADS))
    grid_spec = pltpu.PrefetchScalarGridSpec(
        num_scalar_prefetch=1,
        grid=(batch, n_pairs, nb),
        in_specs=[pl.BlockSpec((None, MOBA_BLOCK, LANES), lambda b, p, j, *_: (b, j, p)),
                  pl.BlockSpec((None, seq, LANES), lambda b, p, j, *_: (b, 0, p)),
                  pl.BlockSpec((None, seq, LANES), lambda b, p, j, *_: (b, 0, n_pairs + p))],
        out_specs=pl.BlockSpec((None, MOBA_BLOCK, LANES), lambda b, p, j, *_: (b, j, p)),
        scratch_shapes=[pltpu.VMEM((LANES, LANES), F32),
                        pltpu.VMEM((LANES // MOBA_HEAD_DIM, nb, MOBA_BLOCK, LANES), F32)],
    )
    return pl.pallas_call(
        _moba_kernel,
        grid_spec=grid_spec,
        out_shape=jax.ShapeDtypeStruct((batch, seq, n_pairs * LANES), BF16),
        compiler_params=pltpu.CompilerParams(
            dimension_semantics=("parallel", "parallel", "arbitrary"),
            vmem_limit_bytes=V7X_VMEM_LIMIT_BYTES),
        name="moba",
    )(slopes, qproj3, kv3, kv3)


def _pad_heads(w, axis):
    shape = list(w.shape)
    shape[axis:axis + 1] = [RET_HEADS, RET_V_DIM]
    w = w.reshape(shape)
    pad = [(0, 0)] * w.ndim
    pad[axis + 1] = (0, RET_V_PAD - RET_V_DIM)
    w = jnp.pad(w, pad)
    shape[axis:axis + 2] = [RET_HEADS * RET_V_PAD]
    return w.reshape(shape)


def kernel(x, mem, w_in_a, ret_norm_gain, w_out_a, kv_norm_gain, w_kv_shared, w_in_b, w_out_b,
           w_mem_kv, norm_pre_mix, norm_post_mix, norm_pre_mlp, norm_post_mlp, w_up, w_down):
    batch, seq, d = x.shape
    m = batch * seq
    depth = norm_pre_mix.shape[0]
    n_a = depth // 2
    qk_w = RET_HEADS * RET_QK_DIM
    v_w = RET_HEADS * RET_V_DIM
    mem_w = MEM_HEADS * MEM_HEAD_DIM
    moba_w = MOBA_HEADS * MOBA_HEAD_DIM
    row_vec = lambda g: g.reshape(1, -1)

    h = x.reshape(m, d)
    kv3 = None
    for l in range(depth):
        if l < n_a:
            w_in = w_in_a[l]
            w_in_pad = jnp.concatenate(
                [w_in[:, :2 * qk_w],
                 _pad_heads(w_in[:, 2 * qk_w:2 * qk_w + v_w], 1),
                 _pad_heads(w_in[:, 2 * qk_w + v_w:2 * qk_w + 2 * v_w], 1),
                 w_in[:, 2 * qk_w + 2 * v_w:]], axis=1).astype(BF16)
            w_out = w_out_a[l]
            w_out_pad = jnp.concatenate(
                [_pad_heads(w_out[:v_w], 0), w_out[v_w:]], axis=0).astype(BF16)
            (proj,) = _norm_proj(h, norm_pre_mix[l:l + 1], [w_in_pad])
            y = _retention(proj, row_vec(_pad_heads(ret_norm_gain[l], 0)), batch, seq)
            qm_block = (2 * qk_w + 2 * RET_HEADS * RET_V_PAD) // mem_w
            h = _mix_out(y, proj, qm_block, mem, w_mem_kv[l].astype(BF16), w_out_pad, h,
                         row_vec(norm_post_mix[l]), batch, seq)
        else:
            jb = l - n_a
            if l == n_a:
                qproj, kv = _norm_proj(
                    h, jnp.stack([norm_pre_mix[l], kv_norm_gain]),
                    [w_in_b[jb].astype(BF16), w_kv_shared.astype(BF16)])
                kv3 = kv.reshape(batch, seq, 2 * moba_w)
            else:
                (qproj,) = _norm_proj(h, norm_pre_mix[l:l + 1], [w_in_b[jb].astype(BF16)])
            a = _moba(qproj.reshape(batch, seq, moba_w + mem_w), kv3)
            h = _mix_out(a.reshape(m, moba_w), qproj, moba_w // mem_w, mem,
                         w_mem_kv[l].astype(BF16), w_out_b[jb].astype(BF16), h,
                         row_vec(norm_post_mix[l]), batch, seq)
        h = _mlp(h, row_vec(norm_pre_mlp[l]), row_vec(norm_post_mlp[l]),
                 w_up[l].astype(BF16), w_down[l].astype(BF16))
    return h.reshape(batch, seq, d)
```

```python
import functools
import math

import numpy as np
import jax
import jax.numpy as jnp
from jax import lax
from jax.experimental import pallas as pl
from jax.experimental.pallas import tpu as pltpu

F32 = jnp.float32
BF16 = jnp.bfloat16

EPS = 1e-6
NEG_INF = -1e30

LANES = 128
V7X_VMEM_LIMIT_BYTES = 56 * 1024 * 1024

RET_HEADS = 4
RET_QK_DIM = 128
RET_V_DIM = 192
RET_V_PAD = 256
RET_CHUNK = 128
MOBA_HEADS = 12
MOBA_HEAD_DIM = 64
MOBA_BLOCK = 256
MOBA_TOPK = 3
MEM_HEADS = 4
MEM_HEAD_DIM = 64

ROW_TILE = 512
RET_ROW_TILE = 256
FF_CHUNK = 1024


def _rms_scale(x):
    return lax.rsqrt(jnp.mean(x * x, axis=-1, keepdims=True) + EPS)


def _dot(a, b):
    return jnp.dot(a, b, preferred_element_type=F32)


def _dot_nt(a, b):
    return lax.dot_general(a, b, (((1,), (1,)), ((), ())), preferred_element_type=F32)


def _dot_tn(a, b):
    return lax.dot_general(a, b, (((0,), (0,)), ((), ())), preferred_element_type=F32)


def _resident(shape):
    zeros = (0,) * len(shape)
    return pl.BlockSpec(shape, lambda *_: zeros, pipeline_mode=pl.Buffered(1))


def _norm_proj_kernel(x_ref, g_ref, *refs, n_out):
    w_refs, o_refs = refs[:n_out], refs[n_out:]
    x = x_ref[...]
    xhat = x * _rms_scale(x)
    for i in range(n_out):
        hn = (xhat * g_ref[i:i + 1, :]).astype(BF16)
        o_refs[i][...] = _dot(hn, w_refs[i][...]).astype(o_refs[i].dtype)


def _norm_proj(x2d, gains, weights):
    m, d = x2d.shape
    n_out = len(weights)
    return pl.pallas_call(
        functools.partial(_norm_proj_kernel, n_out=n_out),
        grid=(m // ROW_TILE,),
        in_specs=[pl.BlockSpec((ROW_TILE, d), lambda i: (i, 0)),
                  _resident(gains.shape)]
                 + [_resident(w.shape) for w in weights],
        out_specs=[pl.BlockSpec((ROW_TILE, w.shape[1]), lambda i: (i, 0)) for w in weights],
        out_shape=[jax.ShapeDtypeStruct((m, w.shape[1]), BF16) for w in weights],
        compiler_params=pltpu.CompilerParams(
            dimension_semantics=("parallel",), vmem_limit_bytes=V7X_VMEM_LIMIT_BYTES),
        name="norm_proj",
    )(x2d, gains, *weights)


def _mlp_kernel(h_ref, g_pre_ref, g_post_ref, wu_ref, wd_ref, o_ref):
    h = h_ref[...]
    hn = (h * _rms_scale(h) * g_pre_ref[...]).astype(BF16)
    d_ff = wu_ref.shape[1]
    y = jnp.zeros(h.shape, F32)
    for c in range(d_ff // FF_CHUNK):
        cols = slice(c * FF_CHUNK, (c + 1) * FF_CHUNK)
        u = _dot(hn, wu_ref[:, cols])
        a = jnp.square(jnp.maximum(u, 0.0)).astype(BF16)
        y = y + _dot(a, wd_ref[cols, :])
    o_ref[...] = h + y * _rms_scale(y) * g_post_ref[...]


def _mlp(h2d, g_pre, g_post, w_up, w_down):
    m, d = h2d.shape
    row_spec = pl.BlockSpec((ROW_TILE, d), lambda i: (i, 0))
    return pl.pallas_call(
        _mlp_kernel,
        grid=(m // ROW_TILE,),
        in_specs=[row_spec, _resident(g_pre.shape), _resident(g_post.shape),
                  _resident(w_up.shape), _resident(w_down.shape)],
        out_specs=row_spec,
        out_shape=jax.ShapeDtypeStruct((m, d), F32),
        compiler_params=pltpu.CompilerParams(
            dimension_semantics=("parallel",), vmem_limit_bytes=V7X_VMEM_LIMIT_BYTES),
        name="mlp",
    )(h2d, g_pre, g_post, w_up, w_down)


def _retention_kernel(q_ref, k_ref, v_ref, g_ref, gain_ref, o_ref, state_ref):
    c = RET_CHUNK

    @pl.when(pl.program_id(1) == 0)
    def _():
        state_ref[...] = jnp.zeros(state_ref.shape, F32)

    row = lax.broadcasted_iota(jnp.int32, (c, c), 0)
    col = lax.broadcasted_iota(jnp.int32, (c, c), 1)
    rel = (row - col).astype(F32)
    pos = lax.broadcasted_iota(jnp.int32, (c, 1), 0).astype(F32)
    qk_scale = RET_QK_DIM ** -0.5

    for h in range(RET_HEADS):
        log_g = float(np.log1p(-np.exp2(np.float32(-5.0 - h))))
        d_intra = jnp.where(rel >= 0, jnp.exp(log_g * jnp.maximum(rel, 0.0)), 0.0) * qk_scale
        zeta = jnp.exp(log_g * (c - 1.0 - pos)) * qk_scale
        xi = jnp.exp(log_g * (pos + 1.0))
        chunk_decay = float(np.exp(np.float32(log_g * c)))
        gain = gain_ref[:, h * RET_V_PAD:(h + 1) * RET_V_PAD]
        qk_cols = slice(h * RET_QK_DIM, (h + 1) * RET_QK_DIM)
        v_cols = slice(h * RET_V_PAD, (h + 1) * RET_V_PAD)
        for ci in range(RET_ROW_TILE // c):
            rows = slice(ci * c, (ci + 1) * c)
            q = q_ref[rows, qk_cols]
            k = k_ref[rows, qk_cols]
            v = v_ref[rows, v_cols]
            state = state_ref[h]
            s = _dot_nt(q, k) * d_intra
            y = _dot(s.astype(BF16), v) + _dot(q, state.astype(BF16)) * xi
            kz = (k.astype(F32) * zeta).astype(BF16)
            state_ref[h] = state * chunk_decay + _dot_tn(kz, v)
            ms = jnp.sum(y * y, axis=-1, keepdims=True) * (1.0 / RET_V_DIM)
            yn = y * lax.rsqrt(ms + EPS) * gain
            g = g_ref[rows, v_cols].astype(F32)
            o_ref[rows, v_cols] = (g / (1.0 + jnp.exp(-g)) * yn).astype(o_ref.dtype)


def _retention(proj, gain_pad, batch, seq):
    m = proj.shape[0]
    steps = seq // RET_ROW_TILE
    qk_w = RET_HEADS * RET_QK_DIM
    v_w = RET_HEADS * RET_V_PAD
    row = lambda b, s: b * steps + s
    return pl.pallas_call(
        _retention_kernel,
        grid=(batch, steps),
        in_specs=[pl.BlockSpec((RET_ROW_TILE, qk_w), lambda b, s: (row(b, s), 0)),
                  pl.BlockSpec((RET_ROW_TILE, qk_w), lambda b, s: (row(b, s), 1)),
                  pl.BlockSpec((RET_ROW_TILE, v_w), lambda b, s: (row(b, s), 1)),
                  pl.BlockSpec((RET_ROW_TILE, v_w), lambda b, s: (row(b, s), 2)),
                  _resident(gain_pad.shape)],
        out_specs=pl.BlockSpec((RET_ROW_TILE, v_w), lambda b, s: (row(b, s), 0)),
        out_shape=jax.ShapeDtypeStruct((m, v_w), BF16),
        scratch_shapes=[pltpu.VMEM((RET_HEADS, RET_QK_DIM, RET_V_PAD), F32)],
        compiler_params=pltpu.CompilerParams(
            dimension_semantics=("parallel", "arbitrary"),
            vmem_limit_bytes=V7X_VMEM_LIMIT_BYTES),
        name="retention",
    )(proj, proj, proj, proj, gain_pad)


def _mix_out_kernel(a_ref, qm_ref, mem_ref, wmkv_ref, wout_ref, h_ref, gain_ref, o_ref,
                    mk_ref, mv_ref):
    mem_w = MEM_HEADS * MEM_HEAD_DIM

    @pl.when(pl.program_id(1) == 0)
    def _():
        mkv = _dot(mem_ref[...].astype(BF16), wmkv_ref[...])
        mk_ref[...] = mkv[:, :mem_w].astype(BF16)
        mv_ref[...] = mkv[:, mem_w:].astype(BF16)

    lane = lax.broadcasted_iota(jnp.int32, (1, LANES), 1)
    scale = MEM_HEAD_DIM ** -0.5
    pairs = []
    for p in range(mem_w // LANES):
        cols = slice(p * LANES, (p + 1) * LANES)
        qp = qm_ref[:, cols] * jnp.asarray(scale, BF16)
        mkp = mk_ref[:, cols]
        mvp = mv_ref[:, cols]
        heads = []
        for hh in range(LANES // MEM_HEAD_DIM):
            in_head = (lane >= hh * MEM_HEAD_DIM) & (lane < (hh + 1) * MEM_HEAD_DIM)
            qh = jnp.where(in_head, qp, jnp.zeros_like(qp))
            s = _dot_nt(qh, mkp)
            e = jnp.exp(s - jnp.max(s, axis=-1, keepdims=True))
            o = _dot(e.astype(BF16), mvp) / jnp.sum(e, axis=-1, keepdims=True)
            heads.append(jnp.where(in_head, o, 0.0))
        pairs.append((heads[0] + heads[1]).astype(BF16))
    mem_out = jnp.concatenate(pairs, axis=-1)
    a_w = a_ref.shape[1]
    mix = _dot(a_ref[...], wout_ref[:a_w, :]) + _dot(mem_out, wout_ref[a_w:, :])
    o_ref[...] = h_ref[...] + mix * _rms_scale(mix) * gain_ref[...]


def _mix_out(a2d, qm_src, qm_col_block, mem, w_mkv, w_out, h2d, gain, batch, seq):
    m, d = h2d.shape
    a_w = a2d.shape[1]
    mem_w = MEM_HEADS * MEM_HEAD_DIM
    mem_len = mem.shape[1]
    steps = seq // ROW_TILE
    row = lambda b, s: b * steps + s
    return pl.pallas_call(
        _mix_out_kernel,
        grid=(batch, steps),
        in_specs=[pl.BlockSpec((ROW_TILE, a_w), lambda b, s: (row(b, s), 0)),
                  pl.BlockSpec((ROW_TILE, mem_w), lambda b, s: (row(b, s), qm_col_block)),
                  pl.BlockSpec((None, mem_len, d), lambda b, s: (b, 0, 0)),
                  _resident(w_mkv.shape), _resident(w_out.shape),
                  pl.BlockSpec((ROW_TILE, d), lambda b, s: (row(b, s), 0)),
                  _resident(gain.shape)],
        out_specs=pl.BlockSpec((ROW_TILE, d), lambda b, s: (row(b, s), 0)),
        out_shape=jax.ShapeDtypeStruct((m, d), F32),
        scratch_shapes=[pltpu.VMEM((mem_len, mem_w), BF16), pltpu.VMEM((mem_len, mem_w), BF16)],
        compiler_params=pltpu.CompilerParams(
            dimension_semantics=("parallel", "arbitrary"),
            vmem_limit_bytes=V7X_VMEM_LIMIT_BYTES),
        name="mix_out",
    )(a2d, qm_src, mem, w_mkv, w_out, h2d, gain)


def _moba_kernel(slopes_ref, q_ref, k_ref, v_ref, o_ref, tmpl_ref, km_ref):
    seq = k_ref.shape[0]
    bs = MOBA_BLOCK
    nb = seq // bs
    heads_per_group = LANES // MOBA_HEAD_DIM
    first_gated = MOBA_TOPK + 1
    p = pl.program_id(0)
    lane = lax.broadcasted_iota(jnp.int32, (1, LANES), 1)
    spare = [((hh + 1) % heads_per_group) * MOBA_HEAD_DIM for hh in range(heads_per_group)]

    @pl.when(pl.program_id(1) == 0)
    def _():
        posi = lax.broadcasted_iota(jnp.int32, (seq, 1), 0)
        blk = posi // bs
        pos = posi.astype(F32)
        for hh in range(heads_per_group):
            x = slopes_ref[heads_per_group * p + hh] * pos
            hi = x.astype(BF16).astype(F32)
            mid = (x - hi).astype(BF16).astype(F32)
            lo = (x - hi - mid).astype(BF16).astype(F32)
            rel_lane = lane - spare[hh]
            t = jnp.where(rel_lane == blk, 1.0, 0.0)
            t = jnp.where(rel_lane == nb, hi, t)
            t = jnp.where(rel_lane == nb + 1, mid, t)
            t = jnp.where(rel_lane == nb + 2, lo, t)
            tmpl_ref[hh] = t.astype(BF16)

    k_all = k_ref[...]
    for n in range(nb):
        kb = k_all[n * bs:(n + 1) * bs, :].astype(F32)
        km_ref[n:n + 1, :] = jnp.sum(kb, axis=0, keepdims=True) * (1.0 / bs)
    km = km_ref[...]
    km_hi = km.astype(BF16)
    km_lo = (km - km_hi.astype(F32)).astype(BF16)

    rowi = lax.broadcasted_iota(jnp.int32, (bs, bs), 0)
    coli = lax.broadcasted_iota(jnp.int32, (bs, bs), 1)
    causal = coli <= rowi
    blk_id = lax.broadcasted_iota(jnp.int32, (nb, bs), 0)
    q_all = q_ref[...] * jnp.asarray(MOBA_HEAD_DIM ** -0.5, BF16)
    v_all = v_ref[...]
    gated_rows = slice(first_gated * bs, seq)
    in_head, k_aug, q_aug = [], [], []
    for hh in range(heads_per_group):
        head_lanes = (lane >= hh * MOBA_HEAD_DIM) & (lane < (hh + 1) * MOBA_HEAD_DIM)
        rel_lane = lane - spare[hh]
        qh = jnp.where(head_lanes, q_all, jnp.zeros_like(q_all))
        gate_t = _dot_nt(km_hi, qh[gated_rows]) + _dot_nt(km_lo, qh[gated_rows])
        bias_t = []
        for j in range(first_gated, nb):
            g = gate_t[:, (j - first_gated) * bs:(j - first_gated + 1) * bs]
            beaten_by = jnp.zeros((nb, bs), F32)
            for mth in range(j):
                gm = g[mth:mth + 1, :]
                wins = (gm > g) | ((gm == g) & (blk_id > mth))
                beaten_by = beaten_by + jnp.where(wins, 1.0, 0.0)
            keep = (beaten_by < float(MOBA_TOPK)) | (blk_id >= j)
            bias_t.append(jnp.where(keep, 0.0, NEG_INF))
        bias_t = jnp.concatenate(bias_t, axis=1)
        parts = [bias_t, jnp.zeros((LANES - spare[hh] - nb, bias_t.shape[1]), F32)]
        if spare[hh]:
            parts.insert(0, jnp.zeros((spare[hh], bias_t.shape[1]), F32))
        block_bias = jnp.concatenate(parts, axis=0).T
        ones_lanes = jnp.where((rel_lane >= nb) & (rel_lane < nb + 3), 1.0, 0.0)
        aug = jnp.concatenate(
            [jnp.broadcast_to(ones_lanes, (first_gated * bs, LANES)), block_bias + ones_lanes],
            axis=0).astype(BF16)
        in_head.append(head_lanes)
        q_aug.append(jnp.where(head_lanes, qh, aug))
        k_aug.append(jnp.where(head_lanes, k_all, tmpl_ref[hh]))

    for j in range(nb):
        rows = slice(j * bs, (j + 1) * bs)
        head_out = []
        for hh in range(heads_per_group):
            qa = q_aug[hh][rows]
            s_own = jnp.where(causal, _dot_nt(qa, k_aug[hh][rows]), NEG_INF)
            m_row = jnp.max(s_own, axis=-1, keepdims=True)
            if j:
                s_past = _dot_nt(qa, k_aug[hh][:j * bs])
                m_row = jnp.maximum(m_row, jnp.max(s_past, axis=-1, keepdims=True))
            e_own = jnp.exp(s_own - m_row)
            l_row = jnp.sum(e_own, axis=-1, keepdims=True)
            acc = _dot(e_own.astype(BF16), v_all[rows])
            if j:
                e_past = jnp.exp(s_past - m_row)
                l_row = l_row + jnp.sum(e_past, axis=-1, keepdims=True)
                acc = acc + _dot(e_past.astype(BF16), v_all[:j * bs])
            head_out.append(acc / l_row)
        o_ref[rows, :] = jnp.where(in_head[0], head_out[0], head_out[1]).astype(o_ref.dtype)


def _alibi_slopes(n):
    def pow2(m):
        return [2.0 ** (-8.0 * (i + 1) / m) for i in range(m)]
    p = 2 ** int(math.floor(math.log2(n)))
    s = pow2(p)
    if p < n:
        s = s + pow2(2 * p)[0::2][: n - p]
    return np.asarray(s, dtype=np.float32)


def _moba(qproj3, kv3):
    batch, seq, _ = qproj3.shape
    n_pairs = MOBA_HEADS * MOBA_HEAD_DIM // LANES
    nb = seq // MOBA_BLOCK
    slopes = jnp.asarray(_alibi_slopes(MOBA_HEADS))
    grid_spec = pltpu.PrefetchScalarGridSpec(
        num_scalar_prefetch=1,
        grid=(n_pairs, batch),
        in_specs=[pl.BlockSpec((None, seq, LANES), lambda p, b, *_: (b, 0, p)),
                  pl.BlockSpec((None, seq, LANES), lambda p, b, *_: (b, 0, p)),
                  pl.BlockSpec((None, seq, LANES), lambda p, b, *_: (b, 0, n_pairs + p))],
        out_specs=pl.BlockSpec((None, seq, LANES), lambda p, b, *_: (b, 0, p)),
        scratch_shapes=[pltpu.VMEM((LANES // MOBA_HEAD_DIM, seq, LANES), BF16),
                        pltpu.VMEM((nb, LANES), F32)],
    )
    return pl.pallas_call(
        _moba_kernel,
        grid_spec=grid_spec,
        out_shape=jax.ShapeDtypeStruct((batch, seq, n_pairs * LANES), BF16),
        compiler_params=pltpu.CompilerParams(
            dimension_semantics=("parallel", "arbitrary"),
            vmem_limit_bytes=V7X_VMEM_LIMIT_BYTES),
        name="moba",
    )(slopes, qproj3, kv3, kv3)


def _pad_heads(w, axis):
    shape = list(w.shape)
    shape[axis:axis + 1] = [RET_HEADS, RET_V_DIM]
    w = w.reshape(shape)
    pad = [(0, 0)] * w.ndim
    pad[axis + 1] = (0, RET_V_PAD - RET_V_DIM)
    w = jnp.pad(w, pad)
    shape[axis:axis + 2] = [RET_HEADS * RET_V_PAD]
    return w.reshape(shape)


def kernel(x, mem, w_in_a, ret_norm_gain, w_out_a, kv_norm_gain, w_kv_shared, w_in_b, w_out_b,
           w_mem_kv, norm_pre_mix, norm_post_mix, norm_pre_mlp, norm_post_mlp, w_up, w_down):
    batch, seq, d = x.shape
    m = batch * seq
    depth = norm_pre_mix.shape[0]
    n_a = depth // 2
    qk_w = RET_HEADS * RET_QK_DIM
    v_w = RET_HEADS * RET_V_DIM
    mem_w = MEM_HEADS * MEM_HEAD_DIM
    moba_w = MOBA_HEADS * MOBA_HEAD_DIM
    row_vec = lambda g: g.reshape(1, -1)

    h = x.reshape(m, d)
    kv3 = None
    for l in range(depth):
        if l < n_a:
            w_in = w_in_a[l]
            w_in_pad = jnp.concatenate(
                [w_in[:, :2 * qk_w],
                 _pad_heads(w_in[:, 2 * qk_w:2 * qk_w + v_w], 1),
                 _pad_heads(w_in[:, 2 * qk_w + v_w:2 * qk_w + 2 * v_w], 1),
                 w_in[:, 2 * qk_w + 2 * v_w:]], axis=1).astype(BF16)
            w_out = w_out_a[l]
            w_out_pad = jnp.concatenate(
                [_pad_heads(w_out[:v_w], 0), w_out[v_w:]], axis=0).astype(BF16)
            (proj,) = _norm_proj(h, norm_pre_mix[l:l + 1], [w_in_pad])
            y = _retention(proj, row_vec(_pad_heads(ret_norm_gain[l], 0)), batch, seq)
            qm_block = (2 * qk_w + 2 * RET_HEADS * RET_V_PAD) // mem_w
            h = _mix_out(y, proj, qm_block, mem, w_mem_kv[l].astype(BF16), w_out_pad, h,
                         row_vec(norm_post_mix[l]), batch, seq)
        else:
            jb = l - n_a
            if l == n_a:
                qproj, kv = _norm_proj(
                    h, jnp.stack([norm_pre_mix[l], kv_norm_gain]),
                    [w_in_b[jb].astype(BF16), w_kv_shared.astype(BF16)])
                kv3 = kv.reshape(batch, seq, 2 * moba_w)
            else:
                (qproj,) = _norm_proj(h, norm_pre_mix[l:l + 1], [w_in_b[jb].astype(BF16)])
            a = _moba(qproj.reshape(batch, seq, moba_w + mem_w), kv3)
            h = _mix_out(a.reshape(m, moba_w), qproj, moba_w // mem_w, mem,
                         w_mem_kv[l].astype(BF16), w_out_b[jb].astype(BF16), h,
                         row_vec(norm_post_mix[l]), batch, seq)
        h = _mlp(h, row_vec(norm_pre_mlp[l]), row_vec(norm_post_mlp[l]),
                 w_up[l].astype(BF16), w_down[l].astype(BF16))
    return h.reshape(batch, seq, d)
```

```python
import functools
import math

import numpy as np
import jax
import jax.numpy as jnp
from jax import lax
from jax.experimental import pallas as pl
from jax.experimental.pallas import tpu as pltpu

F32 = jnp.float32
BF16 = jnp.bfloat16

EPS = 1e-6
NEG_INF = -1e30
LOG2_E = math.log2(math.e)

LANES = 128
V7X_VMEM_LIMIT_BYTES = 56 * 1024 * 1024

RET_HEADS = 4
RET_QK_DIM = 128
RET_V_DIM = 192
RET_V_PAD = 256
RET_CHUNK = 128
MOBA_HEADS = 12
MOBA_HEAD_DIM = 64
MOBA_BLOCK = 256
MOBA_TOPK = 3
MEM_HEADS = 4
MEM_HEAD_DIM = 64

ROW_TILE = 512
RET_ROW_TILE = 256
FF_CHUNK = 1024


def _rms_scale(x):
    return lax.rsqrt(jnp.mean(x * x, axis=-1, keepdims=True) + EPS)


def _dot(a, b):
    return jnp.dot(a, b, preferred_element_type=F32)


def _dot_nt(a, b):
    return lax.dot_general(a, b, (((1,), (1,)), ((), ())), preferred_element_type=F32)


def _dot_tn(a, b):
    return lax.dot_general(a, b, (((0,), (0,)), ((), ())), preferred_element_type=F32)


def _resident(shape):
    zeros = (0,) * len(shape)
    return pl.BlockSpec(shape, lambda *_: zeros, pipeline_mode=pl.Buffered(1))


def _norm_proj_kernel(x_ref, g_ref, *refs, n_out):
    w_refs, s_refs, o_refs = refs[:n_out], refs[n_out:2 * n_out], refs[2 * n_out:]
    x = x_ref[...]
    xhat = x * _rms_scale(x)
    for i in range(n_out):
        hn = (xhat * g_ref[i:i + 1, :]).astype(BF16)
        o_refs[i][...] = (_dot(hn, w_refs[i][...]) * s_refs[i][...]).astype(o_refs[i].dtype)


def _norm_proj(x2d, gains, weights, col_scales):
    m, d = x2d.shape
    n_out = len(weights)
    return pl.pallas_call(
        functools.partial(_norm_proj_kernel, n_out=n_out),
        grid=(m // ROW_TILE,),
        in_specs=[pl.BlockSpec((ROW_TILE, d), lambda i: (i, 0)),
                  _resident(gains.shape)]
                 + [_resident(w.shape) for w in weights]
                 + [_resident(s.shape) for s in col_scales],
        out_specs=[pl.BlockSpec((ROW_TILE, w.shape[1]), lambda i: (i, 0)) for w in weights],
        out_shape=[jax.ShapeDtypeStruct((m, w.shape[1]), BF16) for w in weights],
        compiler_params=pltpu.CompilerParams(
            dimension_semantics=("parallel",), vmem_limit_bytes=V7X_VMEM_LIMIT_BYTES),
        name="norm_proj",
    )(x2d, gains, *weights, *col_scales)


def _mlp_kernel(h_ref, g_pre_ref, g_post_ref, wu_ref, wd_ref, o_ref):
    h = h_ref[...]
    hn = (h * _rms_scale(h) * g_pre_ref[...]).astype(BF16)
    d_ff = wu_ref.shape[1]
    y = jnp.zeros(h.shape, F32)
    for c in range(d_ff // FF_CHUNK):
        cols = slice(c * FF_CHUNK, (c + 1) * FF_CHUNK)
        u = _dot(hn, wu_ref[:, cols])
        a = jnp.square(jnp.maximum(u, 0.0)).astype(BF16)
        y = y + _dot(a, wd_ref[cols, :])
    o_ref[...] = h + y * _rms_scale(y) * g_post_ref[...]


def _mlp(h2d, g_pre, g_post, w_up, w_down):
    m, d = h2d.shape
    row_spec = pl.BlockSpec((ROW_TILE, d), lambda i: (i, 0))
    return pl.pallas_call(
        _mlp_kernel,
        grid=(m // ROW_TILE,),
        in_specs=[row_spec, _resident(g_pre.shape), _resident(g_post.shape),
                  _resident(w_up.shape), _resident(w_down.shape)],
        out_specs=row_spec,
        out_shape=jax.ShapeDtypeStruct((m, d), F32),
        compiler_params=pltpu.CompilerParams(
            dimension_semantics=("parallel",), vmem_limit_bytes=V7X_VMEM_LIMIT_BYTES),
        name="mlp",
    )(h2d, g_pre, g_post, w_up, w_down)


def _retention_kernel(q_ref, k_ref, v_ref, g_ref, gain_ref, o_ref, state_ref):
    c = RET_CHUNK

    @pl.when(pl.program_id(1) == 0)
    def _():
        state_ref[...] = jnp.zeros(state_ref.shape, F32)

    row = lax.broadcasted_iota(jnp.int32, (c, c), 0)
    col = lax.broadcasted_iota(jnp.int32, (c, c), 1)
    rel = (row - col).astype(F32)
    pos = lax.broadcasted_iota(jnp.int32, (c, 1), 0).astype(F32)
    qk_scale = RET_QK_DIM ** -0.5

    for h in range(RET_HEADS):
        log_g = float(np.log1p(-np.exp2(np.float32(-5.0 - h))))
        d_intra = jnp.where(rel >= 0, jnp.exp(log_g * jnp.maximum(rel, 0.0)), 0.0) * qk_scale
        zeta = jnp.exp(log_g * (c - 1.0 - pos)) * qk_scale
        xi = jnp.exp(log_g * (pos + 1.0))
        chunk_decay = float(np.exp(np.float32(log_g * c)))
        gain = gain_ref[:, h * RET_V_PAD:(h + 1) * RET_V_PAD]
        qk_cols = slice(h * RET_QK_DIM, (h + 1) * RET_QK_DIM)
        v_cols = slice(h * RET_V_PAD, (h + 1) * RET_V_PAD)
        for ci in range(RET_ROW_TILE // c):
            rows = slice(ci * c, (ci + 1) * c)
            q = q_ref[rows, qk_cols]
            k = k_ref[rows, qk_cols]
            v = v_ref[rows, v_cols]
            state = state_ref[h]
            s = _dot_nt(q, k) * d_intra
            y = _dot(s.astype(BF16), v) + _dot(q, state.astype(BF16)) * xi
            kz = (k.astype(F32) * zeta).astype(BF16)
            state_ref[h] = state * chunk_decay + _dot_tn(kz, v)
            ms = jnp.sum(y * y, axis=-1, keepdims=True) * (1.0 / RET_V_DIM)
            yn = y * lax.rsqrt(ms + EPS) * gain
            g = g_ref[rows, v_cols].astype(F32)
            o_ref[rows, v_cols] = (g / (1.0 + jnp.exp(-g)) * yn).astype(o_ref.dtype)


def _retention(proj, gain_pad, batch, seq):
    m = proj.shape[0]
    steps = seq // RET_ROW_TILE
    qk_w = RET_HEADS * RET_QK_DIM
    v_w = RET_HEADS * RET_V_PAD
    row = lambda b, s: b * steps + s
    return pl.pallas_call(
        _retention_kernel,
        grid=(batch, steps),
        in_specs=[pl.BlockSpec((RET_ROW_TILE, qk_w), lambda b, s: (row(b, s), 0)),
                  pl.BlockSpec((RET_ROW_TILE, qk_w), lambda b, s: (row(b, s), 1)),
                  pl.BlockSpec((RET_ROW_TILE, v_w), lambda b, s: (row(b, s), 1)),
                  pl.BlockSpec((RET_ROW_TILE, v_w), lambda b, s: (row(b, s), 2)),
                  _resident(gain_pad.shape)],
        out_specs=pl.BlockSpec((RET_ROW_TILE, v_w), lambda b, s: (row(b, s), 0)),
        out_shape=jax.ShapeDtypeStruct((m, v_w), BF16),
        scratch_shapes=[pltpu.VMEM((RET_HEADS, RET_QK_DIM, RET_V_PAD), F32)],
        compiler_params=pltpu.CompilerParams(
            dimension_semantics=("parallel", "arbitrary"),
            vmem_limit_bytes=V7X_VMEM_LIMIT_BYTES),
        name="retention",
    )(proj, proj, proj, proj, gain_pad)


def _mix_out_kernel(a_ref, qm_ref, mem_ref, wmkv_ref, wout_ref, h_ref, gain_ref, o_ref,
                    mk_ref, mv_ref):
    mem_w = MEM_HEADS * MEM_HEAD_DIM

    @pl.when(pl.program_id(1) == 0)
    def _():
        mkv = _dot(mem_ref[...].astype(BF16), wmkv_ref[...])
        mk_ref[...] = mkv[:, :mem_w].astype(BF16)
        mv_ref[...] = mkv[:, mem_w:].astype(BF16)

    lane = lax.broadcasted_iota(jnp.int32, (1, LANES), 1)
    pairs = []
    for p in range(mem_w // LANES):
        cols = slice(p * LANES, (p + 1) * LANES)
        qp = qm_ref[:, cols]
        mkp = mk_ref[:, cols]
        mvp = mv_ref[:, cols]
        heads = []
        for hh in range(LANES // MEM_HEAD_DIM):
            in_head = (lane >= hh * MEM_HEAD_DIM) & (lane < (hh + 1) * MEM_HEAD_DIM)
            qh = jnp.where(in_head, qp, jnp.zeros_like(qp))
            s = _dot_nt(qh, mkp)
            e = jnp.exp(s - jnp.max(s, axis=-1, keepdims=True))
            o = _dot(e.astype(BF16), mvp) / jnp.sum(e, axis=-1, keepdims=True)
            heads.append(jnp.where(in_head, o, 0.0))
        pairs.append((heads[0] + heads[1]).astype(BF16))
    mem_out = jnp.concatenate(pairs, axis=-1)
    a_w = a_ref.shape[1]
    mix = _dot(a_ref[...], wout_ref[:a_w, :]) + _dot(mem_out, wout_ref[a_w:, :])
    o_ref[...] = h_ref[...] + mix * _rms_scale(mix) * gain_ref[...]


def _mix_out(a2d, qm_src, qm_col_block, mem, w_mkv, w_out, h2d, gain, batch, seq):
    m, d = h2d.shape
    a_w = a2d.shape[1]
    mem_w = MEM_HEADS * MEM_HEAD_DIM
    mem_len = mem.shape[1]
    steps = seq // ROW_TILE
    row = lambda b, s: b * steps + s
    return pl.pallas_call(
        _mix_out_kernel,
        grid=(batch, steps),
        in_specs=[pl.BlockSpec((ROW_TILE, a_w), lambda b, s: (row(b, s), 0)),
                  pl.BlockSpec((ROW_TILE, mem_w), lambda b, s: (row(b, s), qm_col_block)),
                  pl.BlockSpec((None, mem_len, d), lambda b, s: (b, 0, 0)),
                  _resident(w_mkv.shape), _resident(w_out.shape),
                  pl.BlockSpec((ROW_TILE, d), lambda b, s: (row(b, s), 0)),
                  _resident(gain.shape)],
        out_specs=pl.BlockSpec((ROW_TILE, d), lambda b, s: (row(b, s), 0)),
        out_shape=jax.ShapeDtypeStruct((m, d), F32),
        scratch_shapes=[pltpu.VMEM((mem_len, mem_w), BF16), pltpu.VMEM((mem_len, mem_w), BF16)],
        compiler_params=pltpu.CompilerParams(
            dimension_semantics=("parallel", "arbitrary"),
            vmem_limit_bytes=V7X_VMEM_LIMIT_BYTES),
        name="mix_out",
    )(a2d, qm_src, mem, w_mkv, w_out, h2d, gain)


def _moba_kernel(slopes_ref, q_ref, k_ref, v_ref, o_ref, tmpl_ref, km_ref):
    seq = k_ref.shape[0]
    bs = MOBA_BLOCK
    nb = seq // bs
    heads_per_group = LANES // MOBA_HEAD_DIM
    first_gated = MOBA_TOPK + 1
    p = pl.program_id(0)
    lane = lax.broadcasted_iota(jnp.int32, (1, LANES), 1)
    spare = [((hh + 1) % heads_per_group) * MOBA_HEAD_DIM for hh in range(heads_per_group)]

    @pl.when(pl.program_id(1) == 0)
    def _():
        posi = lax.broadcasted_iota(jnp.int32, (seq, 1), 0)
        blk = posi // bs
        pos = posi.astype(F32)
        for hh in range(heads_per_group):
            x = (slopes_ref[heads_per_group * p + hh] * LOG2_E) * pos
            hi = x.astype(BF16).astype(F32)
            mid = (x - hi).astype(BF16).astype(F32)
            lo = (x - hi - mid).astype(BF16).astype(F32)
            rel_lane = lane - spare[hh]
            t = jnp.where(rel_lane == blk, 1.0, 0.0)
            t = jnp.where(rel_lane == nb, hi, t)
            t = jnp.where(rel_lane == nb + 1, mid, t)
            t = jnp.where(rel_lane == nb + 2, lo, t)
            tmpl_ref[hh] = t.astype(BF16)

    k_all = k_ref[...]
    for n in range(nb):
        kb = k_all[n * bs:(n + 1) * bs, :].astype(F32)
        km_ref[n:n + 1, :] = jnp.sum(kb, axis=0, keepdims=True) * (1.0 / bs)
    km = km_ref[...]
    km_hi = km.astype(BF16)
    km_lo = (km - km_hi.astype(F32)).astype(BF16)

    rowi = lax.broadcasted_iota(jnp.int32, (bs, bs), 0)
    coli = lax.broadcasted_iota(jnp.int32, (bs, bs), 1)
    causal = coli <= rowi
    blk_id = lax.broadcasted_iota(jnp.int32, (nb, bs), 0)
    q_all = q_ref[...]
    v_all = v_ref[...]
    gated_rows = slice(first_gated * bs, seq)
    in_head, k_aug, q_aug, v_aug = [], [], [], []
    for hh in range(heads_per_group):
        head_lanes = (lane >= hh * MOBA_HEAD_DIM) & (lane < (hh + 1) * MOBA_HEAD_DIM)
        rel_lane = lane - spare[hh]
        qh = jnp.where(head_lanes, q_all, jnp.zeros_like(q_all))
        gate_t = _dot_nt(km_hi, qh[gated_rows]) + _dot_nt(km_lo, qh[gated_rows])
        bias_t = []
        for j in range(first_gated, nb):
            g = gate_t[:, (j - first_gated) * bs:(j - first_gated + 1) * bs]
            beaten_by = jnp.zeros((nb, bs), F32)
            for mth in range(j):
                gm = g[mth:mth + 1, :]
                wins = (gm > g) | ((gm == g) & (blk_id > mth))
                beaten_by = beaten_by + jnp.where(wins, 1.0, 0.0)
            keep = (beaten_by < float(MOBA_TOPK)) | (blk_id >= j)
            bias_t.append(jnp.where(keep, 0.0, NEG_INF))
        bias_t = jnp.concatenate(bias_t, axis=1)
        parts = [bias_t, jnp.zeros((LANES - spare[hh] - nb, bias_t.shape[1]), F32)]
        if spare[hh]:
            parts.insert(0, jnp.zeros((spare[hh], bias_t.shape[1]), F32))
        block_bias = jnp.concatenate(parts, axis=0).T
        ones_lanes = jnp.where((rel_lane >= nb) & (rel_lane < nb + 3), 1.0, 0.0)
        aug = jnp.concatenate(
            [jnp.broadcast_to(ones_lanes, (first_gated * bs, LANES)), block_bias + ones_lanes],
            axis=0).astype(BF16)
        in_head.append(head_lanes)
        q_aug.append(jnp.where(head_lanes, qh, aug))
        k_aug.append(jnp.where(head_lanes, k_all, tmpl_ref[hh]))
        v_aug.append(jnp.where(head_lanes, v_all, jnp.ones_like(v_all)))

    def scores(j, hh):
        rows = slice(j * bs, (j + 1) * bs)
        qa = q_aug[hh][rows]
        s_own = jnp.where(causal, _dot_nt(qa, k_aug[hh][rows]), NEG_INF)
        m_row = jnp.max(s_own, axis=-1, keepdims=True)
        s_past = None
        if j:
            s_past = _dot_nt(qa, k_aug[hh][:j * bs])
            m_row = jnp.maximum(m_row, jnp.max(s_past, axis=-1, keepdims=True))
        return s_own, s_past, m_row

    def attend(j, hh, s_own, s_past, m_row):
        rows = slice(j * bs, (j + 1) * bs)
        acc = _dot(jnp.exp2(s_own - m_row).astype(BF16), v_aug[hh][rows])
        if j:
            acc = acc + _dot(jnp.exp2(s_past - m_row).astype(BF16), v_aug[hh][:j * bs])
        row_sum = pltpu.roll(acc, MOBA_HEAD_DIM, 1)
        return acc / row_sum

    order = [j for pair in zip(range(nb - 1, -1, -1), range(nb)) for j in pair][:nb]
    work = [(j, hh) for j in order for hh in range(heads_per_group)]
    head_out = {}
    pending = scores(*work[0])
    for i, (j, hh) in enumerate(work):
        current = pending
        if i + 1 < len(work):
            pending = scores(*work[i + 1])
        head_out[hh] = attend(j, hh, *current)
        if hh == heads_per_group - 1:
            rows = slice(j * bs, (j + 1) * bs)
            o_ref[rows, :] = jnp.where(in_head[0], head_out[0], head_out[1]).astype(o_ref.dtype)


def _alibi_slopes(n):
    def pow2(m):
        return [2.0 ** (-8.0 * (i + 1) / m) for i in range(m)]
    p = 2 ** int(math.floor(math.log2(n)))
    s = pow2(p)
    if p < n:
        s = s + pow2(2 * p)[0::2][: n - p]
    return np.asarray(s, dtype=np.float32)


def _moba(qproj3, kv3):
    batch, seq, _ = qproj3.shape
    n_pairs = MOBA_HEADS * MOBA_HEAD_DIM // LANES
    nb = seq // MOBA_BLOCK
    slopes = jnp.asarray(_alibi_slopes(MOBA_HEADS))
    grid_spec = pltpu.PrefetchScalarGridSpec(
        num_scalar_prefetch=1,
        grid=(n_pairs, batch),
        in_specs=[pl.BlockSpec((None, seq, LANES), lambda p, b, *_: (b, 0, p)),
                  pl.BlockSpec((None, seq, LANES), lambda p, b, *_: (b, 0, p)),
                  pl.BlockSpec((None, seq, LANES), lambda p, b, *_: (b, 0, n_pairs + p))],
        out_specs=pl.BlockSpec((None, seq, LANES), lambda p, b, *_: (b, 0, p)),
        scratch_shapes=[pltpu.VMEM((LANES // MOBA_HEAD_DIM, seq, LANES), BF16),
                        pltpu.VMEM((nb, LANES), F32)],
    )
    return pl.pallas_call(
        _moba_kernel,
        grid_spec=grid_spec,
        out_shape=jax.ShapeDtypeStruct((batch, seq, n_pairs * LANES), BF16),
        compiler_params=pltpu.CompilerParams(
            dimension_semantics=("parallel", "arbitrary"),
            vmem_limit_bytes=V7X_VMEM_LIMIT_BYTES),
        name="moba",
    )(slopes, qproj3, kv3, kv3)


def _pad_heads(w, axis):
    shape = list(w.shape)
    shape[axis:axis + 1] = [RET_HEADS, RET_V_DIM]
    w = w.reshape(shape)
    pad = [(0, 0)] * w.ndim
    pad[axis + 1] = (0, RET_V_PAD - RET_V_DIM)
    w = jnp.pad(w, pad)
    shape[axis:axis + 2] = [RET_HEADS * RET_V_PAD]
    return w.reshape(shape)


def kernel(x, mem, w_in_a, ret_norm_gain, w_out_a, kv_norm_gain, w_kv_shared, w_in_b, w_out_b,
           w_mem_kv, norm_pre_mix, norm_post_mix, norm_pre_mlp, norm_post_mlp, w_up, w_down):
    batch, seq, d = x.shape
    m = batch * seq
    depth = norm_pre_mix.shape[0]
    n_a = depth // 2
    qk_w = RET_HEADS * RET_QK_DIM
    v_w = RET_HEADS * RET_V_DIM
    mem_w = MEM_HEADS * MEM_HEAD_DIM
    moba_w = MOBA_HEADS * MOBA_HEAD_DIM
    row_vec = lambda g: g.reshape(1, -1)
    mem_q_scale = jnp.full((1, mem_w), MEM_HEAD_DIM ** -0.5, F32)
    moba_q_scale = jnp.full((1, moba_w), MOBA_HEAD_DIM ** -0.5 * LOG2_E, F32)
    scale_a = jnp.concatenate(
        [jnp.ones((1, 2 * qk_w + 2 * RET_HEADS * RET_V_PAD), F32), mem_q_scale], axis=1)
    scale_b = jnp.concatenate([moba_q_scale, mem_q_scale], axis=1)
    scale_kv = jnp.ones((1, 2 * moba_w), F32)

    h = x.reshape(m, d)
    kv3 = None
    for l in range(depth):
        if l < n_a:
            w_in = w_in_a[l]
            w_in_pad = jnp.concatenate(
                [w_in[:, :2 * qk_w],
                 _pad_heads(w_in[:, 2 * qk_w:2 * qk_w + v_w], 1),
                 _pad_heads(w_in[:, 2 * qk_w + v_w:2 * qk_w + 2 * v_w], 1),
                 w_in[:, 2 * qk_w + 2 * v_w:]], axis=1).astype(BF16)
            w_out = w_out_a[l]
            w_out_pad = jnp.concatenate(
                [_pad_heads(w_out[:v_w], 0), w_out[v_w:]], axis=0).astype(BF16)
            (proj,) = _norm_proj(h, norm_pre_mix[l:l + 1], [w_in_pad], [scale_a])
            y = _retention(proj, row_vec(_pad_heads(ret_norm_gain[l], 0)), batch, seq)
            qm_block = (2 * qk_w + 2 * RET_HEADS * RET_V_PAD) // mem_w
            h = _mix_out(y, proj, qm_block, mem, w_mem_kv[l].astype(BF16), w_out_pad, h,
                         row_vec(norm_post_mix[l]), batch, seq)
        else:
            jb = l - n_a
            if l == n_a:
                qproj, kv = _norm_proj(
                    h, jnp.stack([norm_pre_mix[l], kv_norm_gain]),
                    [w_in_b[jb].astype(BF16), w_kv_shared.astype(BF16)], [scale_b, scale_kv])
                kv3 = kv.reshape(batch, seq, 2 * moba_w)
            else:
                (qproj,) = _norm_proj(h, norm_pre_mix[l:l + 1], [w_in_b[jb].astype(BF16)],
                                      [scale_b])
            a = _moba(qproj.reshape(batch, seq, moba_w + mem_w), kv3)
            h = _mix_out(a.reshape(m, moba_w), qproj, moba_w // mem_w, mem,
                         w_mem_kv[l].astype(BF16), w_out_b[jb].astype(BF16), h,
                         row_vec(norm_post_mix[l]), batch, seq)
        h = _mlp(h, row_vec(norm_pre_mlp[l]), row_vec(norm_post_mlp[l]),
                 w_up[l].astype(BF16), w_down[l].astype(BF16))
    return h.reshape(batch, seq, d)
```

```python
import functools
import math

import numpy as np
import jax
import jax.numpy as jnp
from jax import lax
from jax.experimental import pallas as pl
from jax.experimental.pallas import tpu as pltpu

F32 = jnp.float32
BF16 = jnp.bfloat16

EPS = 1e-6
NEG_INF = -1e30
LOG2_E = math.log2(math.e)

LANES = 128
V7X_VMEM_LIMIT_BYTES = 56 * 1024 * 1024

RET_HEADS = 4
RET_QK_DIM = 128
RET_V_DIM = 192
RET_V_PAD = 256
RET_CHUNK = 128
MOBA_HEADS = 12
MOBA_HEAD_DIM = 64
MOBA_BLOCK = 256
MOBA_TOPK = 3
MEM_HEADS = 4
MEM_HEAD_DIM = 64

ROW_TILE = 1024
DENSE_SUB_ROWS = 512
RET_ROW_TILE = 256
MIX_ROW_TILE = 1024
MIX_SUB_ROWS = 256
FF_CHUNK = 1024


def _rms_scale(x):
    return lax.rsqrt(jnp.mean(x * x, axis=-1, keepdims=True) + EPS)


def _dot(a, b):
    return jnp.dot(a, b, preferred_element_type=F32)


def _dot_nt(a, b):
    return lax.dot_general(a, b, (((1,), (1,)), ((), ())), preferred_element_type=F32)


def _dot_tn(a, b):
    return lax.dot_general(a, b, (((0,), (0,)), ((), ())), preferred_element_type=F32)


def _resident(shape):
    zeros = (0,) * len(shape)
    return pl.BlockSpec(shape, lambda *_: zeros, pipeline_mode=pl.Buffered(1))


def _norm_proj_kernel(x_ref, g_ref, *refs, n_out):
    w_refs, s_refs, o_refs = refs[:n_out], refs[n_out:2 * n_out], refs[2 * n_out:]
    sub = DENSE_SUB_ROWS
    n_sub = x_ref.shape[0] // sub

    def normed(r):
        x = x_ref[r * sub:(r + 1) * sub, :]
        return x * _rms_scale(x)

    xhat, xhat_next = normed(0), None
    for r in range(n_sub):
        rows = slice(r * sub, (r + 1) * sub)
        for i in range(n_out):
            hn = (xhat * g_ref[i:i + 1, :]).astype(BF16)
            out = _dot(hn, w_refs[i][...])
            if i == 0 and r + 1 < n_sub:
                xhat_next = normed(r + 1)
            o_refs[i][rows, :] = (out * s_refs[i][...]).astype(o_refs[i].dtype)
        xhat = xhat_next


def _norm_proj(x2d, gains, weights, col_scales):
    m, d = x2d.shape
    n_out = len(weights)
    return pl.pallas_call(
        functools.partial(_norm_proj_kernel, n_out=n_out),
        grid=(m // ROW_TILE,),
        in_specs=[pl.BlockSpec((ROW_TILE, d), lambda i: (i, 0)),
                  _resident(gains.shape)]
                 + [_resident(w.shape) for w in weights]
                 + [_resident(s.shape) for s in col_scales],
        out_specs=[pl.BlockSpec((ROW_TILE, w.shape[1]), lambda i: (i, 0)) for w in weights],
        out_shape=[jax.ShapeDtypeStruct((m, w.shape[1]), BF16) for w in weights],
        compiler_params=pltpu.CompilerParams(
            dimension_semantics=("parallel",), vmem_limit_bytes=V7X_VMEM_LIMIT_BYTES),
        name="norm_proj",
    )(x2d, gains, *weights, *col_scales)


def _mlp_kernel(h_ref, g_pre_ref, g_post_ref, wu_ref, wd_ref, o_ref):
    sub = DENSE_SUB_ROWS
    n_sub = h_ref.shape[0] // sub
    d_ff = wu_ref.shape[1]

    def normed(r):
        h = h_ref[r * sub:(r + 1) * sub, :]
        return (h * _rms_scale(h) * g_pre_ref[...]).astype(BF16)

    def epilogue(r, y):
        rows = slice(r * sub, (r + 1) * sub)
        o_ref[rows, :] = h_ref[rows, :] + y * _rms_scale(y) * g_post_ref[...]

    hn, hn_next, finished = normed(0), None, None
    for r in range(n_sub):
        y = None
        for c in range(d_ff // FF_CHUNK):
            cols = slice(c * FF_CHUNK, (c + 1) * FF_CHUNK)
            u = _dot(hn, wu_ref[:, cols])
            if c == 0:
                if r + 1 < n_sub:
                    hn_next = normed(r + 1)
                if finished is not None:
                    epilogue(*finished)
            a = jnp.square(jnp.maximum(u, 0.0)).astype(BF16)
            d = _dot(a, wd_ref[cols, :])
            y = d if y is None else y + d
        finished, hn = (r, y), hn_next
    epilogue(*finished)


def _mlp(h2d, g_pre, g_post, w_up, w_down):
    m, d = h2d.shape
    row_spec = pl.BlockSpec((ROW_TILE, d), lambda i: (i, 0))
    return pl.pallas_call(
        _mlp_kernel,
        grid=(m // ROW_TILE,),
        in_specs=[row_spec, _resident(g_pre.shape), _resident(g_post.shape),
                  _resident(w_up.shape), _resident(w_down.shape)],
        out_specs=row_spec,
        out_shape=jax.ShapeDtypeStruct((m, d), F32),
        compiler_params=pltpu.CompilerParams(
            dimension_semantics=("parallel",), vmem_limit_bytes=V7X_VMEM_LIMIT_BYTES),
        name="mlp",
    )(h2d, g_pre, g_post, w_up, w_down)


def _retention_kernel(q_ref, k_ref, v_ref, g_ref, gain_ref, o_ref, state_ref):
    c = RET_CHUNK

    @pl.when(pl.program_id(1) == 0)
    def _():
        state_ref[...] = jnp.zeros(state_ref.shape, F32)

    row = lax.broadcasted_iota(jnp.int32, (c, c), 0)
    col = lax.broadcasted_iota(jnp.int32, (c, c), 1)
    rel = (row - col).astype(F32)
    pos = lax.broadcasted_iota(jnp.int32, (c, 1), 0).astype(F32)
    qk_scale = RET_QK_DIM ** -0.5

    for h in range(RET_HEADS):
        log_g = float(np.log1p(-np.exp2(np.float32(-5.0 - h))))
        d_intra = jnp.where(rel >= 0, jnp.exp(log_g * jnp.maximum(rel, 0.0)), 0.0) * qk_scale
        zeta = jnp.exp(log_g * (c - 1.0 - pos)) * qk_scale
        xi = jnp.exp(log_g * (pos + 1.0))
        chunk_decay = float(np.exp(np.float32(log_g * c)))
        gain = gain_ref[:, h * RET_V_PAD:(h + 1) * RET_V_PAD]
        qk_cols = slice(h * RET_QK_DIM, (h + 1) * RET_QK_DIM)
        v_cols = slice(h * RET_V_PAD, (h + 1) * RET_V_PAD)
        for ci in range(RET_ROW_TILE // c):
            rows = slice(ci * c, (ci + 1) * c)
            q = q_ref[rows, qk_cols]
            k = k_ref[rows, qk_cols]
            v = v_ref[rows, v_cols]
            state = state_ref[h]
            s = _dot_nt(q, k) * d_intra
            y = _dot(s.astype(BF16), v) + _dot(q, state.astype(BF16)) * xi
            kz = (k.astype(F32) * zeta).astype(BF16)
            state_ref[h] = state * chunk_decay + _dot_tn(kz, v)
            ms = jnp.sum(y * y, axis=-1, keepdims=True) * (1.0 / RET_V_DIM)
            yn = y * lax.rsqrt(ms + EPS) * gain
            g = g_ref[rows, v_cols].astype(F32)
            o_ref[rows, v_cols] = (g / (1.0 + jnp.exp(-g)) * yn).astype(o_ref.dtype)


def _retention(proj, gain_pad, batch, seq):
    m = proj.shape[0]
    steps = seq // RET_ROW_TILE
    qk_w = RET_HEADS * RET_QK_DIM
    v_w = RET_HEADS * RET_V_PAD
    row = lambda b, s: b * steps + s
    return pl.pallas_call(
        _retention_kernel,
        grid=(batch, steps),
        in_specs=[pl.BlockSpec((RET_ROW_TILE, qk_w), lambda b, s: (row(b, s), 0)),
                  pl.BlockSpec((RET_ROW_TILE, qk_w), lambda b, s: (row(b, s), 1)),
                  pl.BlockSpec((RET_ROW_TILE, v_w), lambda b, s: (row(b, s), 1)),
                  pl.BlockSpec((RET_ROW_TILE, v_w), lambda b, s: (row(b, s), 2)),
                  _resident(gain_pad.shape)],
        out_specs=pl.BlockSpec((RET_ROW_TILE, v_w), lambda b, s: (row(b, s), 0)),
        out_shape=jax.ShapeDtypeStruct((m, v_w), BF16),
        scratch_shapes=[pltpu.VMEM((RET_HEADS, RET_QK_DIM, RET_V_PAD), F32)],
        compiler_params=pltpu.CompilerParams(
            dimension_semantics=("parallel", "arbitrary"),
            vmem_limit_bytes=V7X_VMEM_LIMIT_BYTES),
        name="retention",
    )(proj, proj, proj, proj, gain_pad)


def _mix_out_kernel(a_ref, qm_ref, mem_ref, wmkv_ref, wout_ref, h_ref, gain_ref, o_ref,
                    mk_ref, mv_ref):
    mem_w = MEM_HEADS * MEM_HEAD_DIM
    groups = mem_w // LANES
    heads_per_group = LANES // MEM_HEAD_DIM
    lane = lax.broadcasted_iota(jnp.int32, (1, LANES), 1)
    head_lanes = [(lane >= hh * MEM_HEAD_DIM) & (lane < (hh + 1) * MEM_HEAD_DIM)
                  for hh in range(heads_per_group)]

    @pl.when(pl.program_id(1) == 0)
    def _():
        mkv = _dot(mem_ref[...].astype(BF16), wmkv_ref[...])
        mk_ref[...] = mkv[:, :mem_w].astype(BF16)
        for p in range(groups):
            mvp = mkv[:, mem_w + p * LANES:mem_w + (p + 1) * LANES].astype(BF16)
            for hh in range(heads_per_group):
                mv_ref[p * heads_per_group + hh] = jnp.where(head_lanes[hh], mvp,
                                                             jnp.ones_like(mvp))

    a_w = a_ref.shape[1]
    sub = MIX_SUB_ROWS

    n_sub = a_ref.shape[0] // sub

    def scores(p, hh):
        cols = slice(p * LANES, (p + 1) * LANES)
        qp = qm_ref[:, cols]
        qh = jnp.where(head_lanes[hh], qp, jnp.zeros_like(qp))
        s = _dot_nt(qh, mk_ref[:, cols])
        return s, jnp.max(s, axis=-1, keepdims=True)

    def attend(p, hh, s, m_row):
        acc = _dot(jnp.exp2(s - m_row).astype(BF16), mv_ref[p * heads_per_group + hh])
        return acc / pltpu.roll(acc, MEM_HEAD_DIM, 1)

    def project_a(r):
        return _dot(a_ref[r * sub:(r + 1) * sub, :], wout_ref[:a_w, :])

    def epilogue(r, mix):
        rows = slice(r * sub, (r + 1) * sub)
        o_ref[rows, :] = h_ref[rows, :] + mix * _rms_scale(mix) * gain_ref[...]

    work = [(p, hh) for p in range(groups) for hh in range(heads_per_group)]
    outs, mix_a = {}, {}
    pending = scores(*work[0])
    for i, (p, hh) in enumerate(work):
        current = pending
        if i + 1 < len(work):
            pending = scores(*work[i + 1])
        if i < n_sub // 2:
            mix_a[i] = project_a(i)
        outs[p, hh] = attend(p, hh, *current)
    mem_out = jnp.concatenate(
        [jnp.where(head_lanes[0], outs[p, 0], outs[p, 1]).astype(BF16) for p in range(groups)],
        axis=-1)
    projected = []
    for r in range(n_sub):
        if r not in mix_a:
            mix_a[r] = project_a(r)
            while projected:
                epilogue(*projected.pop(0))
        mix = mix_a.pop(r) + _dot(mem_out[r * sub:(r + 1) * sub], wout_ref[a_w:, :])
        projected.append((r, mix))
    while projected:
        epilogue(*projected.pop(0))


def _mix_out(a2d, qm_src, qm_col_block, mem, w_mkv, w_out, h2d, gain, batch, seq):
    m, d = h2d.shape
    a_w = a2d.shape[1]
    mem_w = MEM_HEADS * MEM_HEAD_DIM
    mem_len = mem.shape[1]
    steps = seq // MIX_ROW_TILE
    row = lambda b, s: b * steps + s
    return pl.pallas_call(
        _mix_out_kernel,
        grid=(batch, steps),
        in_specs=[pl.BlockSpec((MIX_ROW_TILE, a_w), lambda b, s: (row(b, s), 0)),
                  pl.BlockSpec((MIX_ROW_TILE, mem_w), lambda b, s: (row(b, s), qm_col_block)),
                  pl.BlockSpec((None, mem_len, d), lambda b, s: (b, 0, 0)),
                  _resident(w_mkv.shape), _resident(w_out.shape),
                  pl.BlockSpec((MIX_ROW_TILE, d), lambda b, s: (row(b, s), 0)),
                  _resident(gain.shape)],
        out_specs=pl.BlockSpec((MIX_ROW_TILE, d), lambda b, s: (row(b, s), 0)),
        out_shape=jax.ShapeDtypeStruct((m, d), F32),
        scratch_shapes=[pltpu.VMEM((mem_len, mem_w), BF16),
                        pltpu.VMEM((MEM_HEADS, mem_len, LANES), BF16)],
        compiler_params=pltpu.CompilerParams(
            dimension_semantics=("parallel", "arbitrary"),
            vmem_limit_bytes=V7X_VMEM_LIMIT_BYTES),
        name="mix_out",
    )(a2d, qm_src, mem, w_mkv, w_out, h2d, gain)


def _moba_kernel(slopes_ref, q_ref, k_ref, v_ref, o_ref, tmpl_ref, km_ref):
    seq = k_ref.shape[0]
    bs = MOBA_BLOCK
    nb = seq // bs
    heads_per_group = LANES // MOBA_HEAD_DIM
    first_gated = MOBA_TOPK + 1
    p = pl.program_id(0)
    lane = lax.broadcasted_iota(jnp.int32, (1, LANES), 1)
    spare = [((hh + 1) % heads_per_group) * MOBA_HEAD_DIM for hh in range(heads_per_group)]

    @pl.when(pl.program_id(1) == 0)
    def _():
        posi = lax.broadcasted_iota(jnp.int32, (seq, 1), 0)
        blk = posi // bs
        pos = posi.astype(F32)
        for hh in range(heads_per_group):
            x = (slopes_ref[heads_per_group * p + hh] * LOG2_E) * pos
            hi = x.astype(BF16).astype(F32)
            mid = (x - hi).astype(BF16).astype(F32)
            lo = (x - hi - mid).astype(BF16).astype(F32)
            rel_lane = lane - spare[hh]
            t = jnp.where(rel_lane == blk, 1.0, 0.0)
            t = jnp.where(rel_lane == nb, hi, t)
            t = jnp.where(rel_lane == nb + 1, mid, t)
            t = jnp.where(rel_lane == nb + 2, lo, t)
            tmpl_ref[hh] = t.astype(BF16)

    k_all = k_ref[...]
    for n in range(nb):
        kb = k_all[n * bs:(n + 1) * bs, :].astype(F32)
        km_ref[n:n + 1, :] = jnp.sum(kb, axis=0, keepdims=True) * (1.0 / bs)
    km = km_ref[...]
    km_hi = km.astype(BF16)
    km_lo = (km - km_hi.astype(F32)).astype(BF16)

    rowi = lax.broadcasted_iota(jnp.int32, (bs, bs), 0)
    coli = lax.broadcasted_iota(jnp.int32, (bs, bs), 1)
    causal = coli <= rowi
    blk_id = lax.broadcasted_iota(jnp.int32, (nb, bs), 0)
    q_all = q_ref[...]
    v_all = v_ref[...]
    gated_rows = slice(first_gated * bs, seq)
    in_head, k_aug, q_aug, v_aug = [], [], [], []
    for hh in range(heads_per_group):
        head_lanes = (lane >= hh * MOBA_HEAD_DIM) & (lane < (hh + 1) * MOBA_HEAD_DIM)
        rel_lane = lane - spare[hh]
        qh = jnp.where(head_lanes, q_all, jnp.zeros_like(q_all))
        gate_t = _dot_nt(km_hi, qh[gated_rows]) + _dot_nt(km_lo, qh[gated_rows])
        bias_t = []
        for j in range(first_gated, nb):
            g = gate_t[:, (j - first_gated) * bs:(j - first_gated + 1) * bs]
            beaten_by = jnp.zeros((nb, bs), F32)
            for mth in range(j):
                gm = g[mth:mth + 1, :]
                wins = (gm > g) | ((gm == g) & (blk_id > mth))
                beaten_by = beaten_by + jnp.where(wins, 1.0, 0.0)
            keep = (beaten_by < float(MOBA_TOPK)) | (blk_id >= j)
            bias_t.append(jnp.where(keep, 0.0, NEG_INF))
        bias_t = jnp.concatenate(bias_t, axis=1)
        parts = [bias_t, jnp.zeros((LANES - spare[hh] - nb, bias_t.shape[1]), F32)]
        if spare[hh]:
            parts.insert(0, jnp.zeros((spare[hh], bias_t.shape[1]), F32))
        block_bias = jnp.concatenate(parts, axis=0).T
        ones_lanes = jnp.where((rel_lane >= nb) & (rel_lane < nb + 3), 1.0, 0.0)
        gated_aug = (block_bias + ones_lanes).astype(BF16)
        blocks = []
        for j in range(nb):
            rows = slice(j * bs, (j + 1) * bs)
            aug = (gated_aug[(j - first_gated) * bs:(j - first_gated + 1) * bs]
                   if j >= first_gated else ones_lanes.astype(BF16))
            blocks.append(jnp.where(head_lanes, qh[rows], aug))
        in_head.append(head_lanes)
        q_aug.append(blocks)
        k_aug.append(jnp.where(head_lanes, k_all, tmpl_ref[hh]))
        v_aug.append(jnp.where(head_lanes, v_all, jnp.ones_like(v_all)))

    def scores(j, hh):
        rows = slice(j * bs, (j + 1) * bs)
        qa = q_aug[hh][j]
        s_own = jnp.where(causal, _dot_nt(qa, k_aug[hh][rows]), NEG_INF)
        m_row = jnp.max(s_own, axis=-1, keepdims=True)
        s_past = None
        if j:
            s_past = _dot_nt(qa, k_aug[hh][:j * bs])
            m_row = jnp.maximum(m_row, jnp.max(s_past, axis=-1, keepdims=True))
        return s_own, s_past, m_row

    def attend(j, hh, s_own, s_past, m_row):
        rows = slice(j * bs, (j + 1) * bs)
        acc = _dot(jnp.exp2(s_own - m_row).astype(BF16), v_aug[hh][rows])
        if j:
            acc = acc + _dot(jnp.exp2(s_past - m_row).astype(BF16), v_aug[hh][:j * bs])
        row_sum = pltpu.roll(acc, MOBA_HEAD_DIM, 1)
        return acc / row_sum

    order = [j for pair in zip(range(nb - 1, -1, -1), range(nb)) for j in pair][:nb]
    work = [(j, hh) for j in order for hh in range(heads_per_group)]
    head_out = {}
    pending = scores(*work[0])
    for i, (j, hh) in enumerate(work):
        current = pending
        if i + 1 < len(work):
            pending = scores(*work[i + 1])
        head_out[hh] = attend(j, hh, *current)
        if hh == heads_per_group - 1:
            rows = slice(j * bs, (j + 1) * bs)
            o_ref[rows, :] = jnp.where(in_head[0], head_out[0], head_out[1]).astype(o_ref.dtype)


def _alibi_slopes(n):
    def pow2(m):
        return [2.0 ** (-8.0 * (i + 1) / m) for i in range(m)]
    p = 2 ** int(math.floor(math.log2(n)))
    s = pow2(p)
    if p < n:
        s = s + pow2(2 * p)[0::2][: n - p]
    return np.asarray(s, dtype=np.float32)


def _moba(qproj3, kv3):
    batch, seq, _ = qproj3.shape
    n_pairs = MOBA_HEADS * MOBA_HEAD_DIM // LANES
    nb = seq // MOBA_BLOCK
    slopes = jnp.asarray(_alibi_slopes(MOBA_HEADS))
    grid_spec = pltpu.PrefetchScalarGridSpec(
        num_scalar_prefetch=1,
        grid=(n_pairs, batch),
        in_specs=[pl.BlockSpec((None, seq, LANES), lambda p, b, *_: (b, 0, p)),
                  pl.BlockSpec((None, seq, LANES), lambda p, b, *_: (b, 0, p)),
                  pl.BlockSpec((None, seq, LANES), lambda p, b, *_: (b, 0, n_pairs + p))],
        out_specs=pl.BlockSpec((None, seq, LANES), lambda p, b, *_: (b, 0, p)),
        scratch_shapes=[pltpu.VMEM((LANES // MOBA_HEAD_DIM, seq, LANES), BF16),
                        pltpu.VMEM((nb, LANES), F32)],
    )
    return pl.pallas_call(
        _moba_kernel,
        grid_spec=grid_spec,
        out_shape=jax.ShapeDtypeStruct((batch, seq, n_pairs * LANES), BF16),
        compiler_params=pltpu.CompilerParams(
            dimension_semantics=("parallel", "arbitrary"),
            vmem_limit_bytes=V7X_VMEM_LIMIT_BYTES),
        name="moba",
    )(slopes, qproj3, kv3, kv3)


def _pad_heads(w, axis):
    shape = list(w.shape)
    shape[axis:axis + 1] = [RET_HEADS, RET_V_DIM]
    w = w.reshape(shape)
    pad = [(0, 0)] * w.ndim
    pad[axis + 1] = (0, RET_V_PAD - RET_V_DIM)
    w = jnp.pad(w, pad)
    shape[axis:axis + 2] = [RET_HEADS * RET_V_PAD]
    return w.reshape(shape)


def kernel(x, mem, w_in_a, ret_norm_gain, w_out_a, kv_norm_gain, w_kv_shared, w_in_b, w_out_b,
           w_mem_kv, norm_pre_mix, norm_post_mix, norm_pre_mlp, norm_post_mlp, w_up, w_down):
    batch, seq, d = x.shape
    m = batch * seq
    depth = norm_pre_mix.shape[0]
    n_a = depth // 2
    qk_w = RET_HEADS * RET_QK_DIM
    v_w = RET_HEADS * RET_V_DIM
    mem_w = MEM_HEADS * MEM_HEAD_DIM
    moba_w = MOBA_HEADS * MOBA_HEAD_DIM
    row_vec = lambda g: g.reshape(1, -1)
    mem_q_scale = jnp.full((1, mem_w), MEM_HEAD_DIM ** -0.5 * LOG2_E, F32)
    moba_q_scale = jnp.full((1, moba_w), MOBA_HEAD_DIM ** -0.5 * LOG2_E, F32)
    scale_a = jnp.concatenate(
        [jnp.ones((1, 2 * qk_w + 2 * RET_HEADS * RET_V_PAD), F32), mem_q_scale], axis=1)
    scale_b = jnp.concatenate([moba_q_scale, mem_q_scale], axis=1)
    scale_kv = jnp.ones((1, 2 * moba_w), F32)

    h = x.reshape(m, d)
    kv3 = None
    for l in range(depth):
        if l < n_a:
            w_in = w_in_a[l]
            w_in_pad = jnp.concatenate(
                [w_in[:, :2 * qk_w],
                 _pad_heads(w_in[:, 2 * qk_w:2 * qk_w + v_w], 1),
                 _pad_heads(w_in[:, 2 * qk_w + v_w:2 * qk_w + 2 * v_w], 1),
                 w_in[:, 2 * qk_w + 2 * v_w:]], axis=1).astype(BF16)
            w_out = w_out_a[l]
            w_out_pad = jnp.concatenate(
                [_pad_heads(w_out[:v_w], 0), w_out[v_w:]], axis=0).astype(BF16)
            (proj,) = _norm_proj(h, norm_pre_mix[l:l + 1], [w_in_pad], [scale_a])
            y = _retention(proj, row_vec(_pad_heads(ret_norm_gain[l], 0)), batch, seq)
            qm_block = (2 * qk_w + 2 * RET_HEADS * RET_V_PAD) // mem_w
            h = _mix_out(y, proj, qm_block, mem, w_mem_kv[l].astype(BF16), w_out_pad, h,
                         row_vec(norm_post_mix[l]), batch, seq)
        else:
            jb = l - n_a
            if l == n_a:
                qproj, kv = _norm_proj(
                    h, jnp.stack([norm_pre_mix[l], kv_norm_gain]),
                    [w_in_b[jb].astype(BF16), w_kv_shared.astype(BF16)], [scale_b, scale_kv])
                kv3 = kv.reshape(batch, seq, 2 * moba_w)
            else:
                (qproj,) = _norm_proj(h, norm_pre_mix[l:l + 1], [w_in_b[jb].astype(BF16)],
                                      [scale_b])
            a = _moba(qproj.reshape(batch, seq, moba_w + mem_w), kv3)
            h = _mix_out(a.reshape(m, moba_w), qproj, moba_w // mem_w, mem,
                         w_mem_kv[l].astype(BF16), w_out_b[jb].astype(BF16), h,
                         row_vec(norm_post_mix[l]), batch, seq)
        h = _mlp(h, row_vec(norm_pre_mlp[l]), row_vec(norm_post_mlp[l]),
                 w_up[l].astype(BF16), w_down[l].astype(BF16))
    return h.reshape(batch, seq, d)
```

```python
import functools
import math

import numpy as np
import jax
import jax.numpy as jnp
from jax import lax
from jax.experimental import pallas as pl
from jax.experimental.pallas import tpu as pltpu

F32 = jnp.float32
BF16 = jnp.bfloat16

EPS = 1e-6
NEG_INF = -1e30
LOG2_E = math.log2(math.e)

LANES = 128
V7X_VMEM_LIMIT_BYTES = 56 * 1024 * 1024

RET_HEADS = 4
RET_QK_DIM = 128
RET_V_DIM = 192
RET_V_PAD = 256
RET_CHUNK = 128
MOBA_HEADS = 12
MOBA_HEAD_DIM = 64
MOBA_BLOCK = 256
MOBA_TOPK = 3
MEM_HEADS = 4
MEM_HEAD_DIM = 64

ROW_TILE = 1024
DENSE_SUB_ROWS = 512
RET_ROW_TILE = 1024
MIX_ROW_TILE = 1024
MIX_SUB_ROWS = 256
FF_CHUNK = 1024


def _rms_scale(x):
    return lax.rsqrt(jnp.mean(x * x, axis=-1, keepdims=True) + EPS)


def _dot(a, b):
    return jnp.dot(a, b, preferred_element_type=F32)


def _dot_nt(a, b):
    return lax.dot_general(a, b, (((1,), (1,)), ((), ())), preferred_element_type=F32)


def _dot_tn(a, b):
    return lax.dot_general(a, b, (((0,), (0,)), ((), ())), preferred_element_type=F32)


def _resident(shape):
    zeros = (0,) * len(shape)
    return pl.BlockSpec(shape, lambda *_: zeros, pipeline_mode=pl.Buffered(1))


def _norm_proj_kernel(x_ref, g_ref, *refs, n_out):
    w_refs, s_refs, o_refs = refs[:n_out], refs[n_out:2 * n_out], refs[2 * n_out:]
    sub = DENSE_SUB_ROWS
    n_sub = x_ref.shape[0] // sub

    def normed(r):
        x = x_ref[r * sub:(r + 1) * sub, :]
        return x * _rms_scale(x)

    xhat, xhat_next = normed(0), None
    for r in range(n_sub):
        rows = slice(r * sub, (r + 1) * sub)
        for i in range(n_out):
            hn = (xhat * g_ref[i:i + 1, :]).astype(BF16)
            out = _dot(hn, w_refs[i][...])
            if i == 0 and r + 1 < n_sub:
                xhat_next = normed(r + 1)
            o_refs[i][rows, :] = (out * s_refs[i][...]).astype(o_refs[i].dtype)
        xhat = xhat_next


def _norm_proj(x2d, gains, weights, col_scales):
    m, d = x2d.shape
    n_out = len(weights)
    return pl.pallas_call(
        functools.partial(_norm_proj_kernel, n_out=n_out),
        grid=(m // ROW_TILE,),
        in_specs=[pl.BlockSpec((ROW_TILE, d), lambda i: (i, 0)),
                  _resident(gains.shape)]
                 + [_resident(w.shape) for w in weights]
                 + [_resident(s.shape) for s in col_scales],
        out_specs=[pl.BlockSpec((ROW_TILE, w.shape[1]), lambda i: (i, 0)) for w in weights],
        out_shape=[jax.ShapeDtypeStruct((m, w.shape[1]), BF16) for w in weights],
        compiler_params=pltpu.CompilerParams(
            dimension_semantics=("parallel",), vmem_limit_bytes=V7X_VMEM_LIMIT_BYTES),
        name="norm_proj",
    )(x2d, gains, *weights, *col_scales)


def _mlp_kernel(h_ref, g_pre_ref, g_post_ref, wu_ref, wd_ref, o_ref):
    sub = DENSE_SUB_ROWS
    n_sub = h_ref.shape[0] // sub
    d_ff = wu_ref.shape[1]

    def normed(r):
        h = h_ref[r * sub:(r + 1) * sub, :]
        return (h * _rms_scale(h) * g_pre_ref[...]).astype(BF16)

    def epilogue(r, y):
        rows = slice(r * sub, (r + 1) * sub)
        o_ref[rows, :] = h_ref[rows, :] + y * _rms_scale(y) * g_post_ref[...]

    hn, hn_next, finished = normed(0), None, None
    for r in range(n_sub):
        y = None
        for c in range(d_ff // FF_CHUNK):
            cols = slice(c * FF_CHUNK, (c + 1) * FF_CHUNK)
            u = _dot(hn, wu_ref[:, cols])
            if c == 0:
                if r + 1 < n_sub:
                    hn_next = normed(r + 1)
                if finished is not None:
                    epilogue(*finished)
            a = jnp.square(jnp.maximum(u, 0.0)).astype(BF16)
            d = _dot(a, wd_ref[cols, :])
            y = d if y is None else y + d
        finished, hn = (r, y), hn_next
    epilogue(*finished)


def _mlp(h2d, g_pre, g_post, w_up, w_down):
    m, d = h2d.shape
    row_spec = pl.BlockSpec((ROW_TILE, d), lambda i: (i, 0))
    return pl.pallas_call(
        _mlp_kernel,
        grid=(m // ROW_TILE,),
        in_specs=[row_spec, _resident(g_pre.shape), _resident(g_post.shape),
                  _resident(w_up.shape), _resident(w_down.shape)],
        out_specs=row_spec,
        out_shape=jax.ShapeDtypeStruct((m, d), F32),
        compiler_params=pltpu.CompilerParams(
            dimension_semantics=("parallel",), vmem_limit_bytes=V7X_VMEM_LIMIT_BYTES),
        name="mlp",
    )(h2d, g_pre, g_post, w_up, w_down)


def _retention_kernel(q_ref, k_ref, v_ref, g_ref, gain_ref, o_ref, state_ref):
    c = RET_CHUNK

    @pl.when(pl.program_id(1) == 0)
    def _():
        state_ref[...] = jnp.zeros(state_ref.shape, F32)

    row = lax.broadcasted_iota(jnp.int32, (c, c), 0)
    col = lax.broadcasted_iota(jnp.int32, (c, c), 1)
    rel = (row - col).astype(F32)
    pos = lax.broadcasted_iota(jnp.int32, (c, 1), 0).astype(F32)
    qk_scale = RET_QK_DIM ** -0.5

    tables = []
    for h in range(RET_HEADS):
        log_g = float(np.log1p(-np.exp2(np.float32(-5.0 - h))))
        d_intra = jnp.where(rel >= 0, jnp.exp(log_g * jnp.maximum(rel, 0.0)), 0.0) * qk_scale
        zeta = jnp.exp(log_g * (c - 1.0 - pos)) * qk_scale
        xi = jnp.exp(log_g * (pos + 1.0))
        chunk_decay = float(np.exp(np.float32(log_g * c)))
        tables.append((d_intra, zeta, xi, chunk_decay))

    for ci in range(q_ref.shape[0] // c):
        rows = slice(ci * c, (ci + 1) * c)
        for h in range(RET_HEADS):
            d_intra, zeta, xi, chunk_decay = tables[h]
            qk_cols = slice(h * RET_QK_DIM, (h + 1) * RET_QK_DIM)
            v_cols = slice(h * RET_V_PAD, (h + 1) * RET_V_PAD)
            q = q_ref[rows, qk_cols]
            k = k_ref[rows, qk_cols]
            v = v_ref[rows, v_cols]
            state = state_ref[h]
            s = _dot_nt(q, k) * d_intra
            y = _dot(s.astype(BF16), v) + _dot(q, state.astype(BF16)) * xi
            kz = (k.astype(F32) * zeta).astype(BF16)
            state_ref[h] = state * chunk_decay + _dot_tn(kz, v)
            ms = jnp.sum(y * y, axis=-1, keepdims=True) * (1.0 / RET_V_DIM)
            yn = y * lax.rsqrt(ms + EPS) * gain_ref[:, v_cols]
            g = g_ref[rows, v_cols].astype(F32)
            o_ref[rows, v_cols] = (g / (1.0 + jnp.exp2(g * -LOG2_E)) * yn).astype(o_ref.dtype)


def _retention(proj, gain_pad, batch, seq):
    m = proj.shape[0]
    steps = seq // RET_ROW_TILE
    qk_w = RET_HEADS * RET_QK_DIM
    v_w = RET_HEADS * RET_V_PAD
    row = lambda b, s: b * steps + s
    return pl.pallas_call(
        _retention_kernel,
        grid=(batch, steps),
        in_specs=[pl.BlockSpec((RET_ROW_TILE, qk_w), lambda b, s: (row(b, s), 0)),
                  pl.BlockSpec((RET_ROW_TILE, qk_w), lambda b, s: (row(b, s), 1)),
                  pl.BlockSpec((RET_ROW_TILE, v_w), lambda b, s: (row(b, s), 1)),
                  pl.BlockSpec((RET_ROW_TILE, v_w), lambda b, s: (row(b, s), 2)),
                  _resident(gain_pad.shape)],
        out_specs=pl.BlockSpec((RET_ROW_TILE, v_w), lambda b, s: (row(b, s), 0)),
        out_shape=jax.ShapeDtypeStruct((m, v_w), BF16),
        scratch_shapes=[pltpu.VMEM((RET_HEADS, RET_QK_DIM, RET_V_PAD), F32)],
        compiler_params=pltpu.CompilerParams(
            dimension_semantics=("parallel", "arbitrary"),
            vmem_limit_bytes=V7X_VMEM_LIMIT_BYTES),
        name="retention",
    )(proj, proj, proj, proj, gain_pad)


def _mix_out_kernel(a_ref, qm_ref, mem_ref, wmkv_ref, wout_ref, h_ref, gain_ref, o_ref,
                    mk_ref, mv_ref):
    mem_w = MEM_HEADS * MEM_HEAD_DIM
    groups = mem_w // LANES
    heads_per_group = LANES // MEM_HEAD_DIM
    lane = lax.broadcasted_iota(jnp.int32, (1, LANES), 1)
    head_lanes = [(lane >= hh * MEM_HEAD_DIM) & (lane < (hh + 1) * MEM_HEAD_DIM)
                  for hh in range(heads_per_group)]

    @pl.when(pl.program_id(1) == 0)
    def _():
        mkv = _dot(mem_ref[...].astype(BF16), wmkv_ref[...])
        mk_ref[...] = mkv[:, :mem_w].astype(BF16)
        for p in range(groups):
            mvp = mkv[:, mem_w + p * LANES:mem_w + (p + 1) * LANES].astype(BF16)
            for hh in range(heads_per_group):
                mv_ref[p * heads_per_group + hh] = jnp.where(head_lanes[hh], mvp,
                                                             jnp.ones_like(mvp))

    a_w = a_ref.shape[1]
    sub = MIX_SUB_ROWS

    n_sub = a_ref.shape[0] // sub

    def scores(p, hh):
        cols = slice(p * LANES, (p + 1) * LANES)
        qp = qm_ref[:, cols]
        qh = jnp.where(head_lanes[hh], qp, jnp.zeros_like(qp))
        s = _dot_nt(qh, mk_ref[:, cols])
        return s, jnp.max(s, axis=-1, keepdims=True)

    def attend(p, hh, s, m_row):
        acc = _dot(jnp.exp2(s - m_row).astype(BF16), mv_ref[p * heads_per_group + hh])
        return acc / pltpu.roll(acc, MEM_HEAD_DIM, 1)

    def project_a(r):
        return _dot(a_ref[r * sub:(r + 1) * sub, :], wout_ref[:a_w, :])

    def epilogue(r, mix):
        rows = slice(r * sub, (r + 1) * sub)
        o_ref[rows, :] = h_ref[rows, :] + mix * _rms_scale(mix) * gain_ref[...]

    work = [(p, hh) for p in range(groups) for hh in range(heads_per_group)]
    outs, mix_a = {}, {}
    pending = scores(*work[0])
    for i, (p, hh) in enumerate(work):
        current = pending
        if i + 1 < len(work):
            pending = scores(*work[i + 1])
        if i < n_sub // 2:
            mix_a[i] = project_a(i)
        outs[p, hh] = attend(p, hh, *current)
    mem_out = jnp.concatenate(
        [jnp.where(head_lanes[0], outs[p, 0], outs[p, 1]).astype(BF16) for p in range(groups)],
        axis=-1)
    projected = []
    for r in range(n_sub):
        if r not in mix_a:
            mix_a[r] = project_a(r)
            while projected:
                epilogue(*projected.pop(0))
        mix = mix_a.pop(r) + _dot(mem_out[r * sub:(r + 1) * sub], wout_ref[a_w:, :])
        projected.append((r, mix))
    while projected:
        epilogue(*projected.pop(0))


def _mix_out(a2d, qm_src, qm_col_block, mem, w_mkv, w_out, h2d, gain, batch, seq):
    m, d = h2d.shape
    a_w = a2d.shape[1]
    mem_w = MEM_HEADS * MEM_HEAD_DIM
    mem_len = mem.shape[1]
    steps = seq // MIX_ROW_TILE
    row = lambda b, s: b * steps + s
    return pl.pallas_call(
        _mix_out_kernel,
        grid=(batch, steps),
        in_specs=[pl.BlockSpec((MIX_ROW_TILE, a_w), lambda b, s: (row(b, s), 0)),
                  pl.BlockSpec((MIX_ROW_TILE, mem_w), lambda b, s: (row(b, s), qm_col_block)),
                  pl.BlockSpec((None, mem_len, d), lambda b, s: (b, 0, 0)),
                  _resident(w_mkv.shape), _resident(w_out.shape),
                  pl.BlockSpec((MIX_ROW_TILE, d), lambda b, s: (row(b, s), 0)),
                  _resident(gain.shape)],
        out_specs=pl.BlockSpec((MIX_ROW_TILE, d), lambda b, s: (row(b, s), 0)),
        out_shape=jax.ShapeDtypeStruct((m, d), F32),
        scratch_shapes=[pltpu.VMEM((mem_len, mem_w), BF16),
                        pltpu.VMEM((MEM_HEADS, mem_len, LANES), BF16)],
        compiler_params=pltpu.CompilerParams(
            dimension_semantics=("parallel", "arbitrary"),
            vmem_limit_bytes=V7X_VMEM_LIMIT_BYTES),
        name="mix_out",
    )(a2d, qm_src, mem, w_mkv, w_out, h2d, gain)


def _moba_kernel(slopes_ref, q_ref, k_ref, v_ref, o_ref, tmpl_ref, km_ref):
    seq = k_ref.shape[0]
    bs = MOBA_BLOCK
    nb = seq // bs
    heads_per_group = LANES // MOBA_HEAD_DIM
    first_gated = MOBA_TOPK + 1
    p = pl.program_id(0)
    lane = lax.broadcasted_iota(jnp.int32, (1, LANES), 1)
    spare = [((hh + 1) % heads_per_group) * MOBA_HEAD_DIM for hh in range(heads_per_group)]

    @pl.when(pl.program_id(1) == 0)
    def _():
        posi = lax.broadcasted_iota(jnp.int32, (seq, 1), 0)
        blk = posi // bs
        pos = posi.astype(F32)
        for hh in range(heads_per_group):
            x = (slopes_ref[heads_per_group * p + hh] * LOG2_E) * pos
            hi = x.astype(BF16).astype(F32)
            mid = (x - hi).astype(BF16).astype(F32)
            lo = (x - hi - mid).astype(BF16).astype(F32)
            rel_lane = lane - spare[hh]
            t = jnp.where(rel_lane == blk, 1.0, 0.0)
            t = jnp.where(rel_lane == nb, hi, t)
            t = jnp.where(rel_lane == nb + 1, mid, t)
            t = jnp.where(rel_lane == nb + 2, lo, t)
            tmpl_ref[hh] = t.astype(BF16)

    k_all = k_ref[...]
    for n in range(nb):
        kb = k_all[n * bs:(n + 1) * bs, :].astype(F32)
        km_ref[n:n + 1, :] = jnp.sum(kb, axis=0, keepdims=True) * (1.0 / bs)
    km = km_ref[...]
    km_hi = km.astype(BF16)
    km_lo = (km - km_hi.astype(F32)).astype(BF16)

    rowi = lax.broadcasted_iota(jnp.int32, (bs, bs), 0)
    coli = lax.broadcasted_iota(jnp.int32, (bs, bs), 1)
    causal = coli <= rowi
    blk_id = lax.broadcasted_iota(jnp.int32, (nb, bs), 0)
    q_all = q_ref[...]
    v_all = v_ref[...]
    gated_rows = slice(first_gated * bs, seq)
    in_head, k_aug, q_aug, v_aug = [], [], [], []
    for hh in range(heads_per_group):
        head_lanes = (lane >= hh * MOBA_HEAD_DIM) & (lane < (hh + 1) * MOBA_HEAD_DIM)
        rel_lane = lane - spare[hh]
        qh = jnp.where(head_lanes, q_all, jnp.zeros_like(q_all))
        gate_t = _dot_nt(km_hi, qh[gated_rows]) + _dot_nt(km_lo, qh[gated_rows])
        bias_t = []
        for j in range(first_gated, nb):
            g = gate_t[:, (j - first_gated) * bs:(j - first_gated + 1) * bs]
            beaten_by = jnp.zeros((nb, bs), F32)
            for mth in range(j):
                gm = g[mth:mth + 1, :]
                wins = (gm > g) | ((gm == g) & (blk_id > mth))
                beaten_by = beaten_by + jnp.where(wins, 1.0, 0.0)
            keep = (beaten_by < float(MOBA_TOPK)) | (blk_id >= j)
            bias_t.append(jnp.where(keep, 0.0, NEG_INF))
        bias_t = jnp.concatenate(bias_t, axis=1)
        parts = [bias_t, jnp.zeros((LANES - spare[hh] - nb, bias_t.shape[1]), F32)]
        if spare[hh]:
            parts.insert(0, jnp.zeros((spare[hh], bias_t.shape[1]), F32))
        block_bias = jnp.concatenate(parts, axis=0).T
        ones_lanes = jnp.where((rel_lane >= nb) & (rel_lane < nb + 3), 1.0, 0.0)
        gated_aug = (block_bias + ones_lanes).astype(BF16)
        blocks = []
        for j in range(nb):
            rows = slice(j * bs, (j + 1) * bs)
            aug = (gated_aug[(j - first_gated) * bs:(j - first_gated + 1) * bs]
                   if j >= first_gated else ones_lanes.astype(BF16))
            blocks.append(jnp.where(head_lanes, qh[rows], aug))
        in_head.append(head_lanes)
        q_aug.append(blocks)
        k_aug.append(jnp.where(head_lanes, k_all, tmpl_ref[hh]))
        v_aug.append(jnp.where(head_lanes, v_all, jnp.ones_like(v_all)))

    def scores(j, hh):
        rows = slice(j * bs, (j + 1) * bs)
        qa = q_aug[hh][j]
        s_own = jnp.where(causal, _dot_nt(qa, k_aug[hh][rows]), NEG_INF)
        m_row = jnp.max(s_own, axis=-1, keepdims=True)
        s_past = None
        if j:
            s_past = _dot_nt(qa, k_aug[hh][:j * bs])
            m_row = jnp.maximum(m_row, jnp.max(s_past, axis=-1, keepdims=True))
        return s_own, s_past, m_row

    def attend(j, hh, s_own, s_past, m_row):
        rows = slice(j * bs, (j + 1) * bs)
        acc = _dot(jnp.exp2(s_own - m_row).astype(BF16), v_aug[hh][rows])
        if j:
            acc = acc + _dot(jnp.exp2(s_past - m_row).astype(BF16), v_aug[hh][:j * bs])
        row_sum = pltpu.roll(acc, MOBA_HEAD_DIM, 1)
        return acc / row_sum

    order = [j for pair in zip(range(nb - 1, -1, -1), range(nb)) for j in pair][:nb]
    work = [(j, hh) for j in order for hh in range(heads_per_group)]
    head_out = {}
    pending = scores(*work[0])
    for i, (j, hh) in enumerate(work):
        current = pending
        if i + 1 < len(work):
            pending = scores(*work[i + 1])
        head_out[hh] = attend(j, hh, *current)
        if hh == heads_per_group - 1:
            rows = slice(j * bs, (j + 1) * bs)
            o_ref[rows, :] = jnp.where(in_head[0], head_out[0], head_out[1]).astype(o_ref.dtype)


def _alibi_slopes(n):
    def pow2(m):
        return [2.0 ** (-8.0 * (i + 1) / m) for i in range(m)]
    p = 2 ** int(math.floor(math.log2(n)))
    s = pow2(p)
    if p < n:
        s = s + pow2(2 * p)[0::2][: n - p]
    return np.asarray(s, dtype=np.float32)


def _moba(qproj3, kv3):
    batch, seq, _ = qproj3.shape
    n_pairs = MOBA_HEADS * MOBA_HEAD_DIM // LANES
    nb = seq // MOBA_BLOCK
    slopes = jnp.asarray(_alibi_slopes(MOBA_HEADS))
    grid_spec = pltpu.PrefetchScalarGridSpec(
        num_scalar_prefetch=1,
        grid=(n_pairs, batch),
        in_specs=[pl.BlockSpec((None, seq, LANES), lambda p, b, *_: (b, 0, p)),
                  pl.BlockSpec((None, seq, LANES), lambda p, b, *_: (b, 0, p)),
                  pl.BlockSpec((None, seq, LANES), lambda p, b, *_: (b, 0, n_pairs + p))],
        out_specs=pl.BlockSpec((None, seq, LANES), lambda p, b, *_: (b, 0, p)),
        scratch_shapes=[pltpu.VMEM((LANES // MOBA_HEAD_DIM, seq, LANES), BF16),
                        pltpu.VMEM((nb, LANES), F32)],
    )
    return pl.pallas_call(
        _moba_kernel,
        grid_spec=grid_spec,
        out_shape=jax.ShapeDtypeStruct((batch, seq, n_pairs * LANES), BF16),
        compiler_params=pltpu.CompilerParams(
            dimension_semantics=("parallel", "arbitrary"),
            vmem_limit_bytes=V7X_VMEM_LIMIT_BYTES),
        name="moba",
    )(slopes, qproj3, kv3, kv3)


def _pad_heads(w, axis):
    shape = list(w.shape)
    shape[axis:axis + 1] = [RET_HEADS, RET_V_DIM]
    w = w.reshape(shape)
    pad = [(0, 0)] * w.ndim
    pad[axis + 1] = (0, RET_V_PAD - RET_V_DIM)
    w = jnp.pad(w, pad)
    shape[axis:axis + 2] = [RET_HEADS * RET_V_PAD]
    return w.reshape(shape)


def kernel(x, mem, w_in_a, ret_norm_gain, w_out_a, kv_norm_gain, w_kv_shared, w_in_b, w_out_b,
           w_mem_kv, norm_pre_mix, norm_post_mix, norm_pre_mlp, norm_post_mlp, w_up, w_down):
    batch, seq, d = x.shape
    m = batch * seq
    depth = norm_pre_mix.shape[0]
    n_a = depth // 2
    qk_w = RET_HEADS * RET_QK_DIM
    v_w = RET_HEADS * RET_V_DIM
    mem_w = MEM_HEADS * MEM_HEAD_DIM
    moba_w = MOBA_HEADS * MOBA_HEAD_DIM
    row_vec = lambda g: g.reshape(1, -1)
    mem_q_scale = jnp.full((1, mem_w), MEM_HEAD_DIM ** -0.5 * LOG2_E, F32)
    moba_q_scale = jnp.full((1, moba_w), MOBA_HEAD_DIM ** -0.5 * LOG2_E, F32)
    scale_a = jnp.concatenate(
        [jnp.ones((1, 2 * qk_w + 2 * RET_HEADS * RET_V_PAD), F32), mem_q_scale], axis=1)
    scale_b = jnp.concatenate([moba_q_scale, mem_q_scale], axis=1)
    scale_kv = jnp.ones((1, 2 * moba_w), F32)

    h = x.reshape(m, d)
    kv3 = None
    for l in range(depth):
        if l < n_a:
            w_in = w_in_a[l]
            w_in_pad = jnp.concatenate(
                [w_in[:, :2 * qk_w],
                 _pad_heads(w_in[:, 2 * qk_w:2 * qk_w + v_w], 1),
                 _pad_heads(w_in[:, 2 * qk_w + v_w:2 * qk_w + 2 * v_w], 1),
                 w_in[:, 2 * qk_w + 2 * v_w:]], axis=1).astype(BF16)
            w_out = w_out_a[l]
            w_out_pad = jnp.concatenate(
                [_pad_heads(w_out[:v_w], 0), w_out[v_w:]], axis=0).astype(BF16)
            (proj,) = _norm_proj(h, norm_pre_mix[l:l + 1], [w_in_pad], [scale_a])
            y = _retention(proj, row_vec(_pad_heads(ret_norm_gain[l], 0)), batch, seq)
            qm_block = (2 * qk_w + 2 * RET_HEADS * RET_V_PAD) // mem_w
            h = _mix_out(y, proj, qm_block, mem, w_mem_kv[l].astype(BF16), w_out_pad, h,
                         row_vec(norm_post_mix[l]), batch, seq)
        else:
            jb = l - n_a
            if l == n_a:
                qproj, kv = _norm_proj(
                    h, jnp.stack([norm_pre_mix[l], kv_norm_gain]),
                    [w_in_b[jb].astype(BF16), w_kv_shared.astype(BF16)], [scale_b, scale_kv])
                kv3 = kv.reshape(batch, seq, 2 * moba_w)
            else:
                (qproj,) = _norm_proj(h, norm_pre_mix[l:l + 1], [w_in_b[jb].astype(BF16)],
                                      [scale_b])
            a = _moba(qproj.reshape(batch, seq, moba_w + mem_w), kv3)
            h = _mix_out(a.reshape(m, moba_w), qproj, moba_w // mem_w, mem,
                         w_mem_kv[l].astype(BF16), w_out_b[jb].astype(BF16), h,
                         row_vec(norm_post_mix[l]), batch, seq)
        h = _mlp(h, row_vec(norm_pre_mlp[l]), row_vec(norm_post_mlp[l]),
                 w_up[l].astype(BF16), w_down[l].astype(BF16))
    return h.reshape(batch, seq, d)
```

```python
import functools
import math

import numpy as np
import jax
import jax.numpy as jnp
from jax import lax
from jax.experimental import pallas as pl
from jax.experimental.pallas import tpu as pltpu

F32 = jnp.float32
BF16 = jnp.bfloat16

EPS = 1e-6
NEG_INF = -1e30
LOG2_E = math.log2(math.e)

LANES = 128
V7X_VMEM_LIMIT_BYTES = 56 * 1024 * 1024

RET_HEADS = 4
RET_QK_DIM = 128
RET_V_DIM = 192
RET_V_PAD = 256
RET_CHUNK = 128
MOBA_HEADS = 12
MOBA_HEAD_DIM = 64
MOBA_BLOCK = 256
MOBA_TOPK = 3
MEM_HEADS = 4
MEM_HEAD_DIM = 64

ROW_TILE = 1024
DENSE_SUB_ROWS = 512
RET_ROW_TILE = 1024
RET_SUB_ROWS = 256
PROJ_PIECE_COLS = 256
MIX_ROW_TILE = 1024
MIX_SUB_ROWS = 256
FF_CHUNK = 1024


def _rms_scale(x):
    return lax.rsqrt(jnp.mean(x * x, axis=-1, keepdims=True) + EPS)


def _dot(a, b):
    return jnp.dot(a, b, preferred_element_type=F32)


def _dot_nt(a, b):
    return lax.dot_general(a, b, (((1,), (1,)), ((), ())), preferred_element_type=F32)


def _dot_tn(a, b):
    return lax.dot_general(a, b, (((0,), (0,)), ((), ())), preferred_element_type=F32)


def _resident(shape):
    zeros = (0,) * len(shape)
    return pl.BlockSpec(shape, lambda *_: zeros, pipeline_mode=pl.Buffered(1))


def _norm_proj_kernel(x_ref, g_ref, *refs, n_out):
    w_refs, s_refs, o_refs = refs[:n_out], refs[n_out:2 * n_out], refs[2 * n_out:]
    sub = DENSE_SUB_ROWS
    n_sub = x_ref.shape[0] // sub

    def normed(r):
        x = x_ref[r * sub:(r + 1) * sub, :]
        return x * _rms_scale(x)

    xhat, xhat_next = normed(0), None
    for r in range(n_sub):
        rows = slice(r * sub, (r + 1) * sub)
        for i in range(n_out):
            hn = (xhat * g_ref[i:i + 1, :]).astype(BF16)
            out = _dot(hn, w_refs[i][...])
            if i == 0 and r + 1 < n_sub:
                xhat_next = normed(r + 1)
            o_refs[i][rows, :] = (out * s_refs[i][...]).astype(o_refs[i].dtype)
        xhat = xhat_next


def _norm_proj(x2d, gains, weights, col_scales):
    m, d = x2d.shape
    n_out = len(weights)
    return pl.pallas_call(
        functools.partial(_norm_proj_kernel, n_out=n_out),
        grid=(m // ROW_TILE,),
        in_specs=[pl.BlockSpec((ROW_TILE, d), lambda i: (i, 0)),
                  _resident(gains.shape)]
                 + [_resident(w.shape) for w in weights]
                 + [_resident(s.shape) for s in col_scales],
        out_specs=[pl.BlockSpec((ROW_TILE, w.shape[1]), lambda i: (i, 0)) for w in weights],
        out_shape=[jax.ShapeDtypeStruct((m, w.shape[1]), BF16) for w in weights],
        compiler_params=pltpu.CompilerParams(
            dimension_semantics=("parallel",), vmem_limit_bytes=V7X_VMEM_LIMIT_BYTES),
        name="norm_proj",
    )(x2d, gains, *weights, *col_scales)


def _mlp_kernel(h_ref, g_pre_ref, g_post_ref, wu_ref, wd_ref, o_ref):
    sub = DENSE_SUB_ROWS
    n_sub = h_ref.shape[0] // sub
    d_ff = wu_ref.shape[1]

    def normed(r):
        h = h_ref[r * sub:(r + 1) * sub, :]
        return (h * _rms_scale(h) * g_pre_ref[...]).astype(BF16)

    def epilogue(r, y):
        rows = slice(r * sub, (r + 1) * sub)
        o_ref[rows, :] = h_ref[rows, :] + y * _rms_scale(y) * g_post_ref[...]

    hn, hn_next, finished = normed(0), None, None
    for r in range(n_sub):
        y = None
        for c in range(d_ff // FF_CHUNK):
            cols = slice(c * FF_CHUNK, (c + 1) * FF_CHUNK)
            u = _dot(hn, wu_ref[:, cols])
            if c == 0:
                if r + 1 < n_sub:
                    hn_next = normed(r + 1)
                if finished is not None:
                    epilogue(*finished)
            a = jnp.square(jnp.maximum(u, 0.0)).astype(BF16)
            d = _dot(a, wd_ref[cols, :])
            y = d if y is None else y + d
        finished, hn = (r, y), hn_next
    epilogue(*finished)


def _mlp(h2d, g_pre, g_post, w_up, w_down):
    m, d = h2d.shape
    row_spec = pl.BlockSpec((ROW_TILE, d), lambda i: (i, 0))
    return pl.pallas_call(
        _mlp_kernel,
        grid=(m // ROW_TILE,),
        in_specs=[row_spec, _resident(g_pre.shape), _resident(g_post.shape),
                  _resident(w_up.shape), _resident(w_down.shape)],
        out_specs=row_spec,
        out_shape=jax.ShapeDtypeStruct((m, d), F32),
        compiler_params=pltpu.CompilerParams(
            dimension_semantics=("parallel",), vmem_limit_bytes=V7X_VMEM_LIMIT_BYTES),
        name="mlp",
    )(h2d, g_pre, g_post, w_up, w_down)


def _proj_retention_kernel(x_ref, g_pre_ref, w_ref, scale_ref, gain_ref, y_ref, qm_ref,
                           state_ref):
    c = RET_CHUNK
    sub = RET_SUB_ROWS
    qk_w = RET_HEADS * RET_QK_DIM
    v_w = RET_HEADS * RET_V_PAD

    @pl.when(pl.program_id(1) == 0)
    def _():
        state_ref[...] = jnp.zeros(state_ref.shape, F32)

    row = lax.broadcasted_iota(jnp.int32, (c, c), 0)
    col = lax.broadcasted_iota(jnp.int32, (c, c), 1)
    rel = (row - col).astype(F32)
    pos = lax.broadcasted_iota(jnp.int32, (c, 1), 0).astype(F32)
    qk_scale = RET_QK_DIM ** -0.5

    tables = []
    for h in range(RET_HEADS):
        log_g = float(np.log1p(-np.exp2(np.float32(-5.0 - h))))
        d_intra = jnp.where(rel >= 0, jnp.exp(log_g * jnp.maximum(rel, 0.0)), 0.0) * qk_scale
        zeta = jnp.exp(log_g * (c - 1.0 - pos)) * qk_scale
        xi = jnp.exp(log_g * (pos + 1.0))
        chunk_decay = float(np.exp(np.float32(log_g * c)))
        tables.append((d_intra, zeta, xi, chunk_decay))

    n_pieces = w_ref.shape[1] // PROJ_PIECE_COLS

    def project_pieces(r):
        x = x_ref[r * sub:(r + 1) * sub, :]
        hn = (x * _rms_scale(x) * g_pre_ref[...]).astype(BF16)

        def piece(n):
            cols = slice(n * PROJ_PIECE_COLS, (n + 1) * PROJ_PIECE_COLS)
            return (_dot(hn, w_ref[:, cols]) * scale_ref[:, cols]).astype(BF16)
        return [functools.partial(piece, n) for n in range(n_pieces)]

    def retain(r, proj, next_pieces):
        qm_ref[r * sub:(r + 1) * sub, :] = proj[:, 2 * qk_w + 2 * v_w:]
        bodies = [(ci, h) for ci in range(sub // c) for h in range(RET_HEADS)]
        done = []
        for i, (ci, h) in enumerate(bodies):
            while len(done) < (i + 1) * len(next_pieces) // len(bodies):
                done.append(next_pieces[len(done)]())
            rows = slice(ci * c, (ci + 1) * c)
            out_rows = slice(r * sub + ci * c, r * sub + (ci + 1) * c)
            d_intra, zeta, xi, chunk_decay = tables[h]
            v_cols = slice(h * RET_V_PAD, (h + 1) * RET_V_PAD)
            q = proj[rows, h * RET_QK_DIM:(h + 1) * RET_QK_DIM]
            k = proj[rows, qk_w + h * RET_QK_DIM:qk_w + (h + 1) * RET_QK_DIM]
            v = proj[rows, 2 * qk_w + h * RET_V_PAD:2 * qk_w + (h + 1) * RET_V_PAD]
            g = proj[rows, 2 * qk_w + v_w + h * RET_V_PAD:
                     2 * qk_w + v_w + (h + 1) * RET_V_PAD].astype(F32)
            state = state_ref[h]
            s = _dot_nt(q, k) * d_intra
            y = _dot(s.astype(BF16), v) + _dot(q, state.astype(BF16)) * xi
            kz = (k.astype(F32) * zeta).astype(BF16)
            state_ref[h] = state * chunk_decay + _dot_tn(kz, v)
            ms = jnp.sum(y * y, axis=-1, keepdims=True) * (1.0 / RET_V_DIM)
            yn = y * lax.rsqrt(ms + EPS) * gain_ref[:, v_cols]
            y_ref[out_rows, v_cols] = (g / (1.0 + jnp.exp2(g * -LOG2_E)) * yn).astype(
                y_ref.dtype)
        return done

    n_groups = x_ref.shape[0] // sub
    proj = jnp.concatenate([piece() for piece in project_pieces(0)], axis=1)
    for r in range(n_groups):
        next_pieces = project_pieces(r + 1) if r + 1 < n_groups else []
        done = retain(r, proj, next_pieces)
        if done:
            proj = jnp.concatenate(done, axis=1)


def _proj_retention(x2d, g_pre, w_in_pad, col_scale, gain_pad, batch, seq):
    m, d = x2d.shape
    steps = seq // RET_ROW_TILE
    v_w = RET_HEADS * RET_V_PAD
    mem_w = MEM_HEADS * MEM_HEAD_DIM
    row_block = lambda width: pl.BlockSpec((RET_ROW_TILE, width), lambda b, s: (b * steps + s, 0))
    return pl.pallas_call(
        _proj_retention_kernel,
        grid=(batch, steps),
        in_specs=[row_block(d), _resident(g_pre.shape), _resident(w_in_pad.shape),
                  _resident(col_scale.shape), _resident(gain_pad.shape)],
        out_specs=[row_block(v_w), row_block(mem_w)],
        out_shape=[jax.ShapeDtypeStruct((m, v_w), BF16), jax.ShapeDtypeStruct((m, mem_w), BF16)],
        scratch_shapes=[pltpu.VMEM((RET_HEADS, RET_QK_DIM, RET_V_PAD), F32)],
        compiler_params=pltpu.CompilerParams(
            dimension_semantics=("parallel", "arbitrary"),
            vmem_limit_bytes=V7X_VMEM_LIMIT_BYTES),
        name="proj_retention",
    )(x2d, g_pre, w_in_pad, col_scale, gain_pad)


def _mix_out_kernel(a_ref, qm_ref, mem_ref, wmkv_ref, wout_ref, h_ref, gain_ref, o_ref,
                    mk_ref, mv_ref):
    mem_w = MEM_HEADS * MEM_HEAD_DIM
    groups = mem_w // LANES
    heads_per_group = LANES // MEM_HEAD_DIM
    lane = lax.broadcasted_iota(jnp.int32, (1, LANES), 1)
    head_lanes = [(lane >= hh * MEM_HEAD_DIM) & (lane < (hh + 1) * MEM_HEAD_DIM)
                  for hh in range(heads_per_group)]

    @pl.when(pl.program_id(1) == 0)
    def _():
        mkv = _dot(mem_ref[...].astype(BF16), wmkv_ref[...])
        mk_ref[...] = mkv[:, :mem_w].astype(BF16)
        for p in range(groups):
            mvp = mkv[:, mem_w + p * LANES:mem_w + (p + 1) * LANES].astype(BF16)
            for hh in range(heads_per_group):
                mv_ref[p * heads_per_group + hh] = jnp.where(head_lanes[hh], mvp,
                                                             jnp.ones_like(mvp))

    a_w = a_ref.shape[1]
    sub = MIX_SUB_ROWS

    n_sub = a_ref.shape[0] // sub

    def scores(p, hh):
        cols = slice(p * LANES, (p + 1) * LANES)
        qp = qm_ref[:, cols]
        qh = jnp.where(head_lanes[hh], qp, jnp.zeros_like(qp))
        s = _dot_nt(qh, mk_ref[:, cols])
        return s, jnp.max(s, axis=-1, keepdims=True)

    def attend(p, hh, s, m_row):
        acc = _dot(jnp.exp2(s - m_row).astype(BF16), mv_ref[p * heads_per_group + hh])
        return acc / pltpu.roll(acc, MEM_HEAD_DIM, 1)

    def project_a(r):
        return _dot(a_ref[r * sub:(r + 1) * sub, :], wout_ref[:a_w, :])

    def epilogue(r, mix):
        rows = slice(r * sub, (r + 1) * sub)
        o_ref[rows, :] = h_ref[rows, :] + mix * _rms_scale(mix) * gain_ref[...]

    work = [(p, hh) for p in range(groups) for hh in range(heads_per_group)]
    outs, mix_a = {}, {}
    pending = scores(*work[0])
    for i, (p, hh) in enumerate(work):
        current = pending
        if i + 1 < len(work):
            pending = scores(*work[i + 1])
        if i < n_sub // 2:
            mix_a[i] = project_a(i)
        outs[p, hh] = attend(p, hh, *current)
    mem_out = jnp.concatenate(
        [jnp.where(head_lanes[0], outs[p, 0], outs[p, 1]).astype(BF16) for p in range(groups)],
        axis=-1)
    projected = []
    for r in range(n_sub):
        if r not in mix_a:
            mix_a[r] = project_a(r)
            while projected:
                epilogue(*projected.pop(0))
        mix = mix_a.pop(r) + _dot(mem_out[r * sub:(r + 1) * sub], wout_ref[a_w:, :])
        projected.append((r, mix))
    while projected:
        epilogue(*projected.pop(0))


def _mix_out(a2d, qm_src, qm_col_block, mem, w_mkv, w_out, h2d, gain, batch, seq):
    m, d = h2d.shape
    a_w = a2d.shape[1]
    mem_w = MEM_HEADS * MEM_HEAD_DIM
    mem_len = mem.shape[1]
    steps = seq // MIX_ROW_TILE
    row = lambda b, s: b * steps + s
    return pl.pallas_call(
        _mix_out_kernel,
        grid=(batch, steps),
        in_specs=[pl.BlockSpec((MIX_ROW_TILE, a_w), lambda b, s: (row(b, s), 0)),
                  pl.BlockSpec((MIX_ROW_TILE, mem_w), lambda b, s: (row(b, s), qm_col_block)),
                  pl.BlockSpec((None, mem_len, d), lambda b, s: (b, 0, 0)),
                  _resident(w_mkv.shape), _resident(w_out.shape),
                  pl.BlockSpec((MIX_ROW_TILE, d), lambda b, s: (row(b, s), 0)),
                  _resident(gain.shape)],
        out_specs=pl.BlockSpec((MIX_ROW_TILE, d), lambda b, s: (row(b, s), 0)),
        out_shape=jax.ShapeDtypeStruct((m, d), F32),
        scratch_shapes=[pltpu.VMEM((mem_len, mem_w), BF16),
                        pltpu.VMEM((MEM_HEADS, mem_len, LANES), BF16)],
        compiler_params=pltpu.CompilerParams(
            dimension_semantics=("parallel", "arbitrary"),
            vmem_limit_bytes=V7X_VMEM_LIMIT_BYTES),
        name="mix_out",
    )(a2d, qm_src, mem, w_mkv, w_out, h2d, gain)


def _moba_kernel(slopes_ref, q_ref, k_ref, v_ref, o_ref, tmpl_ref, km_ref):
    seq = k_ref.shape[0]
    bs = MOBA_BLOCK
    nb = seq // bs
    heads_per_group = LANES // MOBA_HEAD_DIM
    first_gated = MOBA_TOPK + 1
    p = pl.program_id(0)
    lane = lax.broadcasted_iota(jnp.int32, (1, LANES), 1)
    spare = [((hh + 1) % heads_per_group) * MOBA_HEAD_DIM for hh in range(heads_per_group)]

    @pl.when(pl.program_id(1) == 0)
    def _():
        posi = lax.broadcasted_iota(jnp.int32, (seq, 1), 0)
        blk = posi // bs
        pos = posi.astype(F32)
        for hh in range(heads_per_group):
            x = (slopes_ref[heads_per_group * p + hh] * LOG2_E) * pos
            hi = x.astype(BF16).astype(F32)
            mid = (x - hi).astype(BF16).astype(F32)
            lo = (x - hi - mid).astype(BF16).astype(F32)
            rel_lane = lane - spare[hh]
            t = jnp.where(rel_lane == blk, 1.0, 0.0)
            t = jnp.where(rel_lane == nb, hi, t)
            t = jnp.where(rel_lane == nb + 1, mid, t)
            t = jnp.where(rel_lane == nb + 2, lo, t)
            tmpl_ref[hh] = t.astype(BF16)

    k_all = k_ref[...]
    for n in range(nb):
        kb = k_all[n * bs:(n + 1) * bs, :].astype(F32)
        km_ref[n:n + 1, :] = jnp.sum(kb, axis=0, keepdims=True) * (1.0 / bs)
    km = km_ref[...]
    km_hi = km.astype(BF16)
    km_lo = (km - km_hi.astype(F32)).astype(BF16)

    rowi = lax.broadcasted_iota(jnp.int32, (bs, bs), 0)
    coli = lax.broadcasted_iota(jnp.int32, (bs, bs), 1)
    causal = coli <= rowi
    blk_id = lax.broadcasted_iota(jnp.int32, (nb, bs), 0)
    q_all = q_ref[...]
    v_all = v_ref[...]
    gated_rows = slice(first_gated * bs, seq)
    in_head, k_aug, q_aug, v_aug = [], [], [], []
    for hh in range(heads_per_group):
        head_lanes = (lane >= hh * MOBA_HEAD_DIM) & (lane < (hh + 1) * MOBA_HEAD_DIM)
        rel_lane = lane - spare[hh]
        qh = jnp.where(head_lanes, q_all, jnp.zeros_like(q_all))
        gate_t = _dot_nt(km_hi, qh[gated_rows]) + _dot_nt(km_lo, qh[gated_rows])
        bias_t = []
        for j in range(first_gated, nb):
            g = gate_t[:, (j - first_gated) * bs:(j - first_gated + 1) * bs]
            beaten_by = jnp.zeros((nb, bs), F32)
            for mth in range(j):
                gm = g[mth:mth + 1, :]
                wins = (gm > g) | ((gm == g) & (blk_id > mth))
                beaten_by = beaten_by + jnp.where(wins, 1.0, 0.0)
            keep = (beaten_by < float(MOBA_TOPK)) | (blk_id >= j)
            bias_t.append(jnp.where(keep, 0.0, NEG_INF))
        bias_t = jnp.concatenate(bias_t, axis=1)
        parts = [bias_t, jnp.zeros((LANES - spare[hh] - nb, bias_t.shape[1]), F32)]
        if spare[hh]:
            parts.insert(0, jnp.zeros((spare[hh], bias_t.shape[1]), F32))
        block_bias = jnp.concatenate(parts, axis=0).T
        ones_lanes = jnp.where((rel_lane >= nb) & (rel_lane < nb + 3), 1.0, 0.0)
        gated_aug = (block_bias + ones_lanes).astype(BF16)
        blocks = []
        for j in range(nb):
            rows = slice(j * bs, (j + 1) * bs)
            aug = (gated_aug[(j - first_gated) * bs:(j - first_gated + 1) * bs]
                   if j >= first_gated else ones_lanes.astype(BF16))
            blocks.append(jnp.where(head_lanes, qh[rows], aug))
        in_head.append(head_lanes)
        q_aug.append(blocks)
        k_aug.append(jnp.where(head_lanes, k_all, tmpl_ref[hh]))
        v_aug.append(jnp.where(head_lanes, v_all, jnp.ones_like(v_all)))

    def scores(j, hh):
        rows = slice(j * bs, (j + 1) * bs)
        qa = q_aug[hh][j]
        s_own = jnp.where(causal, _dot_nt(qa, k_aug[hh][rows]), NEG_INF)
        m_row = jnp.max(s_own, axis=-1, keepdims=True)
        s_past = None
        if j:
            s_past = _dot_nt(qa, k_aug[hh][:j * bs])
            m_row = jnp.maximum(m_row, jnp.max(s_past, axis=-1, keepdims=True))
        return s_own, s_past, m_row

    def attend(j, hh, s_own, s_past, m_row):
        rows = slice(j * bs, (j + 1) * bs)
        acc = _dot(jnp.exp2(s_own - m_row).astype(BF16), v_aug[hh][rows])
        if j:
            acc = acc + _dot(jnp.exp2(s_past - m_row).astype(BF16), v_aug[hh][:j * bs])
        row_sum = pltpu.roll(acc, MOBA_HEAD_DIM, 1)
        return acc / row_sum

    order = [j for pair in zip(range(nb - 1, -1, -1), range(nb)) for j in pair][:nb]
    work = [(j, hh) for j in order for hh in range(heads_per_group)]
    head_out = {}
    pending = scores(*work[0])
    for i, (j, hh) in enumerate(work):
        current = pending
        if i + 1 < len(work):
            pending = scores(*work[i + 1])
        head_out[hh] = attend(j, hh, *current)
        if hh == heads_per_group - 1:
            rows = slice(j * bs, (j + 1) * bs)
            o_ref[rows, :] = jnp.where(in_head[0], head_out[0], head_out[1]).astype(o_ref.dtype)


def _alibi_slopes(n):
    def pow2(m):
        return [2.0 ** (-8.0 * (i + 1) / m) for i in range(m)]
    p = 2 ** int(math.floor(math.log2(n)))
    s = pow2(p)
    if p < n:
        s = s + pow2(2 * p)[0::2][: n - p]
    return np.asarray(s, dtype=np.float32)


def _moba(qproj3, kv3):
    batch, seq, _ = qproj3.shape
    n_pairs = MOBA_HEADS * MOBA_HEAD_DIM // LANES
    nb = seq // MOBA_BLOCK
    slopes = jnp.asarray(_alibi_slopes(MOBA_HEADS))
    grid_spec = pltpu.PrefetchScalarGridSpec(
        num_scalar_prefetch=1,
        grid=(n_pairs, batch),
        in_specs=[pl.BlockSpec((None, seq, LANES), lambda p, b, *_: (b, 0, p)),
                  pl.BlockSpec((None, seq, LANES), lambda p, b, *_: (b, 0, p)),
                  pl.BlockSpec((None, seq, LANES), lambda p, b, *_: (b, 0, n_pairs + p))],
        out_specs=pl.BlockSpec((None, seq, LANES), lambda p, b, *_: (b, 0, p)),
        scratch_shapes=[pltpu.VMEM((LANES // MOBA_HEAD_DIM, seq, LANES), BF16),
                        pltpu.VMEM((nb, LANES), F32)],
    )
    return pl.pallas_call(
        _moba_kernel,
        grid_spec=grid_spec,
        out_shape=jax.ShapeDtypeStruct((batch, seq, n_pairs * LANES), BF16),
        compiler_params=pltpu.CompilerParams(
            dimension_semantics=("parallel", "arbitrary"),
            vmem_limit_bytes=V7X_VMEM_LIMIT_BYTES),
        name="moba",
    )(slopes, qproj3, kv3, kv3)


def _pad_heads(w, axis):
    shape = list(w.shape)
    shape[axis:axis + 1] = [RET_HEADS, RET_V_DIM]
    w = w.reshape(shape)
    pad = [(0, 0)] * w.ndim
    pad[axis + 1] = (0, RET_V_PAD - RET_V_DIM)
    w = jnp.pad(w, pad)
    shape[axis:axis + 2] = [RET_HEADS * RET_V_PAD]
    return w.reshape(shape)


def kernel(x, mem, w_in_a, ret_norm_gain, w_out_a, kv_norm_gain, w_kv_shared, w_in_b, w_out_b,
           w_mem_kv, norm_pre_mix, norm_post_mix, norm_pre_mlp, norm_post_mlp, w_up, w_down):
    batch, seq, d = x.shape
    m = batch * seq
    depth = norm_pre_mix.shape[0]
    n_a = depth // 2
    qk_w = RET_HEADS * RET_QK_DIM
    v_w = RET_HEADS * RET_V_DIM
    mem_w = MEM_HEADS * MEM_HEAD_DIM
    moba_w = MOBA_HEADS * MOBA_HEAD_DIM
    row_vec = lambda g: g.reshape(1, -1)
    mem_q_scale = jnp.full((1, mem_w), MEM_HEAD_DIM ** -0.5 * LOG2_E, F32)
    moba_q_scale = jnp.full((1, moba_w), MOBA_HEAD_DIM ** -0.5 * LOG2_E, F32)
    scale_a = jnp.concatenate(
        [jnp.ones((1, 2 * qk_w + 2 * RET_HEADS * RET_V_PAD), F32), mem_q_scale], axis=1)
    scale_b = jnp.concatenate([moba_q_scale, mem_q_scale], axis=1)
    scale_kv = jnp.ones((1, 2 * moba_w), F32)

    h = x.reshape(m, d)
    kv3 = None
    for l in range(depth):
        if l < n_a:
            w_in = w_in_a[l]
            w_in_pad = jnp.concatenate(
                [w_in[:, :2 * qk_w],
                 _pad_heads(w_in[:, 2 * qk_w:2 * qk_w + v_w], 1),
                 _pad_heads(w_in[:, 2 * qk_w + v_w:2 * qk_w + 2 * v_w], 1),
                 w_in[:, 2 * qk_w + 2 * v_w:]], axis=1).astype(BF16)
            w_out = w_out_a[l]
            w_out_pad = jnp.concatenate(
                [_pad_heads(w_out[:v_w], 0), w_out[v_w:]], axis=0).astype(BF16)
            y, q_m = _proj_retention(h, norm_pre_mix[l:l + 1], w_in_pad, scale_a,
                                     row_vec(_pad_heads(ret_norm_gain[l], 0)), batch, seq)
            h = _mix_out(y, q_m, 0, mem, w_mem_kv[l].astype(BF16), w_out_pad, h,
                         row_vec(norm_post_mix[l]), batch, seq)
        else:
            jb = l - n_a
            if l == n_a:
                qproj, kv = _norm_proj(
                    h, jnp.stack([norm_pre_mix[l], kv_norm_gain]),
                    [w_in_b[jb].astype(BF16), w_kv_shared.astype(BF16)], [scale_b, scale_kv])
                kv3 = kv.reshape(batch, seq, 2 * moba_w)
            else:
                (qproj,) = _norm_proj(h, norm_pre_mix[l:l + 1], [w_in_b[jb].astype(BF16)],
                                      [scale_b])
            a = _moba(qproj.reshape(batch, seq, moba_w + mem_w), kv3)
            h = _mix_out(a.reshape(m, moba_w), qproj, moba_w // mem_w, mem,
                         w_mem_kv[l].astype(BF16), w_out_b[jb].astype(BF16), h,
                         row_vec(norm_post_mix[l]), batch, seq)
        h = _mlp(h, row_vec(norm_pre_mlp[l]), row_vec(norm_post_mlp[l]),
                 w_up[l].astype(BF16), w_down[l].astype(BF16))
    return h.reshape(batch, seq, d)
```

```python
import functools
import math

import numpy as np
import jax
import jax.numpy as jnp
from jax import lax
from jax.experimental import pallas as pl
from jax.experimental.pallas import tpu as pltpu

F32 = jnp.float32
BF16 = jnp.bfloat16

EPS = 1e-6
NEG_INF = -1e30
LOG2_E = math.log2(math.e)

LANES = 128
V7X_VMEM_LIMIT_BYTES = 56 * 1024 * 1024

RET_HEADS = 4
RET_QK_DIM = 128
RET_V_DIM = 192
RET_CHUNK = 128
MOBA_HEADS = 12
MOBA_HEAD_DIM = 64
MOBA_BLOCK = 256
MOBA_TOPK = 3
MEM_HEADS = 4
MEM_HEAD_DIM = 64

ROW_TILE = 1024
DENSE_SUB_ROWS = 512
RET_ROW_TILE = 1024
RET_SUB_ROWS = 256
PROJ_PIECE_COLS = 256
MIX_ROW_TILE = 1024
MIX_SUB_ROWS = 256
FF_CHUNK = 1024


def _rms_scale(x):
    return lax.rsqrt(jnp.mean(x * x, axis=-1, keepdims=True) + EPS)


def _dot(a, b):
    return jnp.dot(a, b, preferred_element_type=F32)


def _dot_nt(a, b):
    return lax.dot_general(a, b, (((1,), (1,)), ((), ())), preferred_element_type=F32)


def _dot_tn(a, b):
    return lax.dot_general(a, b, (((0,), (0,)), ((), ())), preferred_element_type=F32)


def _resident(shape):
    zeros = (0,) * len(shape)
    return pl.BlockSpec(shape, lambda *_: zeros, pipeline_mode=pl.Buffered(1))


def _norm_proj_kernel(x_ref, g_ref, *refs, n_out):
    w_refs, s_refs, o_refs = refs[:n_out], refs[n_out:2 * n_out], refs[2 * n_out:]
    sub = DENSE_SUB_ROWS
    n_sub = x_ref.shape[0] // sub

    def normed(r):
        x = x_ref[r * sub:(r + 1) * sub, :]
        return x * _rms_scale(x)

    xhat, xhat_next = normed(0), None
    for r in range(n_sub):
        rows = slice(r * sub, (r + 1) * sub)
        for i in range(n_out):
            hn = (xhat * g_ref[i:i + 1, :]).astype(BF16)
            out = _dot(hn, w_refs[i][...])
            if i == 0 and r + 1 < n_sub:
                xhat_next = normed(r + 1)
            o_refs[i][rows, :] = (out * s_refs[i][...]).astype(o_refs[i].dtype)
        xhat = xhat_next


def _norm_proj(x2d, gains, weights, col_scales):
    m, d = x2d.shape
    n_out = len(weights)
    return pl.pallas_call(
        functools.partial(_norm_proj_kernel, n_out=n_out),
        grid=(m // ROW_TILE,),
        in_specs=[pl.BlockSpec((ROW_TILE, d), lambda i: (i, 0)),
                  _resident(gains.shape)]
                 + [_resident(w.shape) for w in weights]
                 + [_resident(s.shape) for s in col_scales],
        out_specs=[pl.BlockSpec((ROW_TILE, w.shape[1]), lambda i: (i, 0)) for w in weights],
        out_shape=[jax.ShapeDtypeStruct((m, w.shape[1]), BF16) for w in weights],
        compiler_params=pltpu.CompilerParams(
            dimension_semantics=("parallel",), vmem_limit_bytes=V7X_VMEM_LIMIT_BYTES),
        name="norm_proj",
    )(x2d, gains, *weights, *col_scales)


def _mlp_kernel(h_ref, g_pre_ref, g_post_ref, wu_ref, wd_ref, o_ref):
    sub = DENSE_SUB_ROWS
    n_sub = h_ref.shape[0] // sub
    d_ff = wu_ref.shape[1]

    def normed(r):
        h = h_ref[r * sub:(r + 1) * sub, :]
        return (h * _rms_scale(h) * g_pre_ref[...]).astype(BF16)

    def epilogue(r, y):
        rows = slice(r * sub, (r + 1) * sub)
        o_ref[rows, :] = h_ref[rows, :] + y * _rms_scale(y) * g_post_ref[...]

    hn, hn_next, finished = normed(0), None, None
    for r in range(n_sub):
        y = None
        for c in range(d_ff // FF_CHUNK):
            cols = slice(c * FF_CHUNK, (c + 1) * FF_CHUNK)
            u = _dot(hn, wu_ref[:, cols])
            if c == 0:
                if r + 1 < n_sub:
                    hn_next = normed(r + 1)
                if finished is not None:
                    epilogue(*finished)
            a = jnp.square(jnp.maximum(u, 0.0)).astype(BF16)
            d = _dot(a, wd_ref[cols, :])
            y = d if y is None else y + d
        finished, hn = (r, y), hn_next
    epilogue(*finished)


def _mlp(h2d, g_pre, g_post, w_up, w_down):
    m, d = h2d.shape
    row_spec = pl.BlockSpec((ROW_TILE, d), lambda i: (i, 0))
    return pl.pallas_call(
        _mlp_kernel,
        grid=(m // ROW_TILE,),
        in_specs=[row_spec, _resident(g_pre.shape), _resident(g_post.shape),
                  _resident(w_up.shape), _resident(w_down.shape)],
        out_specs=row_spec,
        out_shape=jax.ShapeDtypeStruct((m, d), F32),
        compiler_params=pltpu.CompilerParams(
            dimension_semantics=("parallel",), vmem_limit_bytes=V7X_VMEM_LIMIT_BYTES),
        name="mlp",
    )(h2d, g_pre, g_post, w_up, w_down)


def _proj_retention_kernel(x_ref, g_pre_ref, w_ref, scale_ref, gain_ref, y_ref, qm_ref,
                           state_ref):
    c = RET_CHUNK
    sub = RET_SUB_ROWS
    qk_w = RET_HEADS * RET_QK_DIM
    v_w = RET_HEADS * RET_V_DIM

    @pl.when(pl.program_id(1) == 0)
    def _():
        state_ref[...] = jnp.zeros(state_ref.shape, F32)

    row = lax.broadcasted_iota(jnp.int32, (c, c), 0)
    col = lax.broadcasted_iota(jnp.int32, (c, c), 1)
    rel = (row - col).astype(F32)
    pos = lax.broadcasted_iota(jnp.int32, (c, 1), 0).astype(F32)
    qk_scale = RET_QK_DIM ** -0.5

    tables = []
    for h in range(RET_HEADS):
        log_g = float(np.log1p(-np.exp2(np.float32(-5.0 - h))))
        d_intra = jnp.where(rel >= 0, jnp.exp(log_g * jnp.maximum(rel, 0.0)), 0.0) * qk_scale
        zeta = jnp.exp(log_g * (c - 1.0 - pos)) * qk_scale
        xi = jnp.exp(log_g * (pos + 1.0))
        chunk_decay = float(np.exp(np.float32(log_g * c)))
        tables.append((d_intra, zeta, xi, chunk_decay))

    n_pieces = w_ref.shape[1] // PROJ_PIECE_COLS

    def project_pieces(r):
        x = x_ref[r * sub:(r + 1) * sub, :]
        hn = (x * _rms_scale(x) * g_pre_ref[...]).astype(BF16)

        def piece(n):
            cols = slice(n * PROJ_PIECE_COLS, (n + 1) * PROJ_PIECE_COLS)
            return (_dot(hn, w_ref[:, cols]) * scale_ref[:, cols]).astype(BF16)
        return [functools.partial(piece, n) for n in range(n_pieces)]

    def retain(r, proj, next_pieces):
        qm_ref[r * sub:(r + 1) * sub, :] = proj[:, 2 * qk_w + 2 * v_w:]
        bodies = [(ci, h) for ci in range(sub // c) for h in range(RET_HEADS)]
        done = []
        for i, (ci, h) in enumerate(bodies):
            while len(done) < (i + 1) * len(next_pieces) // len(bodies):
                done.append(next_pieces[len(done)]())
            rows = slice(ci * c, (ci + 1) * c)
            out_rows = slice(r * sub + ci * c, r * sub + (ci + 1) * c)
            d_intra, zeta, xi, chunk_decay = tables[h]
            v_cols = slice(h * RET_V_DIM, (h + 1) * RET_V_DIM)
            q = proj[rows, h * RET_QK_DIM:(h + 1) * RET_QK_DIM]
            k = proj[rows, qk_w + h * RET_QK_DIM:qk_w + (h + 1) * RET_QK_DIM]
            v = proj[rows, 2 * qk_w + h * RET_V_DIM:2 * qk_w + (h + 1) * RET_V_DIM]
            g = proj[rows, 2 * qk_w + v_w + h * RET_V_DIM:
                     2 * qk_w + v_w + (h + 1) * RET_V_DIM].astype(F32)
            state = state_ref[h]
            s = _dot_nt(q, k) * d_intra
            y = _dot(s.astype(BF16), v) + _dot(q, state.astype(BF16)) * xi
            kz = (k.astype(F32) * zeta).astype(BF16)
            state_ref[h] = state * chunk_decay + _dot_tn(kz, v)
            ms = jnp.mean(y * y, axis=-1, keepdims=True)
            yn = y * lax.rsqrt(ms + EPS) * gain_ref[:, v_cols]
            y_ref[out_rows, v_cols] = (g / (1.0 + jnp.exp2(g * -LOG2_E)) * yn).astype(
                y_ref.dtype)
        return done

    n_groups = x_ref.shape[0] // sub
    proj = jnp.concatenate([piece() for piece in project_pieces(0)], axis=1)
    for r in range(n_groups):
        next_pieces = project_pieces(r + 1) if r + 1 < n_groups else []
        done = retain(r, proj, next_pieces)
        if done:
            proj = jnp.concatenate(done, axis=1)


def _proj_retention(x2d, g_pre, w_in, col_scale, gn_gain, batch, seq):
    m, d = x2d.shape
    steps = seq // RET_ROW_TILE
    v_w = RET_HEADS * RET_V_DIM
    mem_w = MEM_HEADS * MEM_HEAD_DIM
    row_block = lambda width: pl.BlockSpec((RET_ROW_TILE, width), lambda b, s: (b * steps + s, 0))
    return pl.pallas_call(
        _proj_retention_kernel,
        grid=(batch, steps),
        in_specs=[row_block(d), _resident(g_pre.shape), _resident(w_in.shape),
                  _resident(col_scale.shape), _resident(gn_gain.shape)],
        out_specs=[row_block(v_w), row_block(mem_w)],
        out_shape=[jax.ShapeDtypeStruct((m, v_w), BF16), jax.ShapeDtypeStruct((m, mem_w), BF16)],
        scratch_shapes=[pltpu.VMEM((RET_HEADS, RET_QK_DIM, RET_V_DIM), F32)],
        compiler_params=pltpu.CompilerParams(
            dimension_semantics=("parallel", "arbitrary"),
            vmem_limit_bytes=V7X_VMEM_LIMIT_BYTES),
        name="proj_retention",
    )(x2d, g_pre, w_in, col_scale, gn_gain)


def _mix_out_kernel(a_ref, qm_ref, mem_ref, wmkv_ref, wout_ref, h_ref, gain_ref, o_ref,
                    mk_ref, mv_ref):
    mem_w = MEM_HEADS * MEM_HEAD_DIM
    groups = mem_w // LANES
    heads_per_group = LANES // MEM_HEAD_DIM
    lane = lax.broadcasted_iota(jnp.int32, (1, LANES), 1)
    head_lanes = [(lane >= hh * MEM_HEAD_DIM) & (lane < (hh + 1) * MEM_HEAD_DIM)
                  for hh in range(heads_per_group)]

    @pl.when(pl.program_id(1) == 0)
    def _():
        mkv = _dot(mem_ref[...].astype(BF16), wmkv_ref[...])
        mk_ref[...] = mkv[:, :mem_w].astype(BF16)
        for p in range(groups):
            mvp = mkv[:, mem_w + p * LANES:mem_w + (p + 1) * LANES].astype(BF16)
            for hh in range(heads_per_group):
                mv_ref[p * heads_per_group + hh] = jnp.where(head_lanes[hh], mvp,
                                                             jnp.ones_like(mvp))

    a_w = a_ref.shape[1]
    sub = MIX_SUB_ROWS

    n_sub = a_ref.shape[0] // sub

    def scores(p, hh):
        cols = slice(p * LANES, (p + 1) * LANES)
        qp = qm_ref[:, cols]
        qh = jnp.where(head_lanes[hh], qp, jnp.zeros_like(qp))
        s = _dot_nt(qh, mk_ref[:, cols])
        return s, jnp.max(s, axis=-1, keepdims=True)

    def attend(p, hh, s, m_row):
        acc = _dot(jnp.exp2(s - m_row).astype(BF16), mv_ref[p * heads_per_group + hh])
        return acc / pltpu.roll(acc, MEM_HEAD_DIM, 1)

    def project_a(r):
        return _dot(a_ref[r * sub:(r + 1) * sub, :], wout_ref[:a_w, :])

    def epilogue(r, mix):
        rows = slice(r * sub, (r + 1) * sub)
        o_ref[rows, :] = h_ref[rows, :] + mix * _rms_scale(mix) * gain_ref[...]

    work = [(p, hh) for p in range(groups) for hh in range(heads_per_group)]
    outs, mix_a = {}, {}
    pending = scores(*work[0])
    for i, (p, hh) in enumerate(work):
        current = pending
        if i + 1 < len(work):
            pending = scores(*work[i + 1])
        if i < n_sub // 2:
            mix_a[i] = project_a(i)
        outs[p, hh] = attend(p, hh, *current)
    mem_out = jnp.concatenate(
        [jnp.where(head_lanes[0], outs[p, 0], outs[p, 1]).astype(BF16) for p in range(groups)],
        axis=-1)
    projected = []
    for r in range(n_sub):
        if r not in mix_a:
            mix_a[r] = project_a(r)
            while projected:
                epilogue(*projected.pop(0))
        mix = mix_a.pop(r) + _dot(mem_out[r * sub:(r + 1) * sub], wout_ref[a_w:, :])
        projected.append((r, mix))
    while projected:
        epilogue(*projected.pop(0))


def _mix_out(a2d, qm_src, qm_col_block, mem, w_mkv, w_out, h2d, gain, batch, seq):
    m, d = h2d.shape
    a_w = a2d.shape[1]
    mem_w = MEM_HEADS * MEM_HEAD_DIM
    mem_len = mem.shape[1]
    steps = seq // MIX_ROW_TILE
    row = lambda b, s: b * steps + s
    return pl.pallas_call(
        _mix_out_kernel,
        grid=(batch, steps),
        in_specs=[pl.BlockSpec((MIX_ROW_TILE, a_w), lambda b, s: (row(b, s), 0)),
                  pl.BlockSpec((MIX_ROW_TILE, mem_w), lambda b, s: (row(b, s), qm_col_block)),
                  pl.BlockSpec((None, mem_len, d), lambda b, s: (b, 0, 0)),
                  _resident(w_mkv.shape), _resident(w_out.shape),
                  pl.BlockSpec((MIX_ROW_TILE, d), lambda b, s: (row(b, s), 0)),
                  _resident(gain.shape)],
        out_specs=pl.BlockSpec((MIX_ROW_TILE, d), lambda b, s: (row(b, s), 0)),
        out_shape=jax.ShapeDtypeStruct((m, d), F32),
        scratch_shapes=[pltpu.VMEM((mem_len, mem_w), BF16),
                        pltpu.VMEM((MEM_HEADS, mem_len, LANES), BF16)],
        compiler_params=pltpu.CompilerParams(
            dimension_semantics=("parallel", "arbitrary"),
            vmem_limit_bytes=V7X_VMEM_LIMIT_BYTES),
        name="mix_out",
    )(a2d, qm_src, mem, w_mkv, w_out, h2d, gain)


def _moba_kernel(slopes_ref, q_ref, k_ref, v_ref, o_ref, tmpl_ref, km_ref):
    seq = k_ref.shape[0]
    bs = MOBA_BLOCK
    nb = seq // bs
    heads_per_group = LANES // MOBA_HEAD_DIM
    first_gated = MOBA_TOPK + 1
    p = pl.program_id(0)
    lane = lax.broadcasted_iota(jnp.int32, (1, LANES), 1)
    spare = [((hh + 1) % heads_per_group) * MOBA_HEAD_DIM for hh in range(heads_per_group)]

    @pl.when(pl.program_id(1) == 0)
    def _():
        posi = lax.broadcasted_iota(jnp.int32, (seq, 1), 0)
        blk = posi // bs
        pos = posi.astype(F32)
        for hh in range(heads_per_group):
            x = (slopes_ref[heads_per_group * p + hh] * LOG2_E) * pos
            hi = x.astype(BF16).astype(F32)
            mid = (x - hi).astype(BF16).astype(F32)
            lo = (x - hi - mid).astype(BF16).astype(F32)
            rel_lane = lane - spare[hh]
            t = jnp.where(rel_lane == blk, 1.0, 0.0)
            t = jnp.where(rel_lane == nb, hi, t)
            t = jnp.where(rel_lane == nb + 1, mid, t)
            t = jnp.where(rel_lane == nb + 2, lo, t)
            tmpl_ref[hh] = t.astype(BF16)

    k_all = k_ref[...]
    for n in range(nb):
        kb = k_all[n * bs:(n + 1) * bs, :].astype(F32)
        km_ref[n:n + 1, :] = jnp.sum(kb, axis=0, keepdims=True) * (1.0 / bs)
    km = km_ref[...]
    km_hi = km.astype(BF16)
    km_lo = (km - km_hi.astype(F32)).astype(BF16)

    rowi = lax.broadcasted_iota(jnp.int32, (bs, bs), 0)
    coli = lax.broadcasted_iota(jnp.int32, (bs, bs), 1)
    causal = coli <= rowi
    blk_id = lax.broadcasted_iota(jnp.int32, (nb, bs), 0)
    q_all = q_ref[...]
    v_all = v_ref[...]
    gated_rows = slice(first_gated * bs, seq)
    in_head, k_aug, q_aug, v_aug = [], [], [], []
    for hh in range(heads_per_group):
        head_lanes = (lane >= hh * MOBA_HEAD_DIM) & (lane < (hh + 1) * MOBA_HEAD_DIM)
        rel_lane = lane - spare[hh]
        qh = jnp.where(head_lanes, q_all, jnp.zeros_like(q_all))
        gate_t = _dot_nt(km_hi, qh[gated_rows]) + _dot_nt(km_lo, qh[gated_rows])
        bias_t = []
        for j in range(first_gated, nb):
            g = gate_t[:, (j - first_gated) * bs:(j - first_gated + 1) * bs]
            beaten_by = jnp.zeros((nb, bs), F32)
            for mth in range(j):
                gm = g[mth:mth + 1, :]
                wins = (gm > g) | ((gm == g) & (blk_id > mth))
                beaten_by = beaten_by + jnp.where(wins, 1.0, 0.0)
            keep = (beaten_by < float(MOBA_TOPK)) | (blk_id >= j)
            bias_t.append(jnp.where(keep, 0.0, NEG_INF))
        bias_t = jnp.concatenate(bias_t, axis=1)
        parts = [bias_t, jnp.zeros((LANES - spare[hh] - nb, bias_t.shape[1]), F32)]
        if spare[hh]:
            parts.insert(0, jnp.zeros((spare[hh], bias_t.shape[1]), F32))
        block_bias = jnp.concatenate(parts, axis=0).T
        ones_lanes = jnp.where((rel_lane >= nb) & (rel_lane < nb + 3), 1.0, 0.0)
        gated_aug = (block_bias + ones_lanes).astype(BF16)
        blocks = []
        for j in range(nb):
            rows = slice(j * bs, (j + 1) * bs)
            aug = (gated_aug[(j - first_gated) * bs:(j - first_gated + 1) * bs]
                   if j >= first_gated else ones_lanes.astype(BF16))
            blocks.append(jnp.where(head_lanes, qh[rows], aug))
        in_head.append(head_lanes)
        q_aug.append(blocks)
        k_aug.append(jnp.where(head_lanes, k_all, tmpl_ref[hh]))
        v_aug.append(jnp.where(head_lanes, v_all, jnp.ones_like(v_all)))

    def scores(j, hh):
        rows = slice(j * bs, (j + 1) * bs)
        qa = q_aug[hh][j]
        s_own = jnp.where(causal, _dot_nt(qa, k_aug[hh][rows]), NEG_INF)
        m_row = jnp.max(s_own, axis=-1, keepdims=True)
        s_past = None
        if j:
            s_past = _dot_nt(qa, k_aug[hh][:j * bs])
            m_row = jnp.maximum(m_row, jnp.max(s_past, axis=-1, keepdims=True))
        return s_own, s_past, m_row

    def attend(j, hh, s_own, s_past, m_row):
        rows = slice(j * bs, (j + 1) * bs)
        acc = _dot(jnp.exp2(s_own - m_row).astype(BF16), v_aug[hh][rows])
        if j:
            acc = acc + _dot(jnp.exp2(s_past - m_row).astype(BF16), v_aug[hh][:j * bs])
        row_sum = pltpu.roll(acc, MOBA_HEAD_DIM, 1)
        return acc / row_sum

    order = [j for pair in zip(range(nb - 1, -1, -1), range(nb)) for j in pair][:nb]
    work = [(j, hh) for j in order for hh in range(heads_per_group)]
    head_out = {}
    pending = scores(*work[0])
    for i, (j, hh) in enumerate(work):
        current = pending
        if i + 1 < len(work):
            pending = scores(*work[i + 1])
        head_out[hh] = attend(j, hh, *current)
        if hh == heads_per_group - 1:
            rows = slice(j * bs, (j + 1) * bs)
            o_ref[rows, :] = jnp.where(in_head[0], head_out[0], head_out[1]).astype(o_ref.dtype)


def _alibi_slopes(n):
    def pow2(m):
        return [2.0 ** (-8.0 * (i + 1) / m) for i in range(m)]
    p = 2 ** int(math.floor(math.log2(n)))
    s = pow2(p)
    if p < n:
        s = s + pow2(2 * p)[0::2][: n - p]
    return np.asarray(s, dtype=np.float32)


def _moba(qproj3, kv3):
    batch, seq, _ = qproj3.shape
    n_pairs = MOBA_HEADS * MOBA_HEAD_DIM // LANES
    nb = seq // MOBA_BLOCK
    slopes = jnp.asarray(_alibi_slopes(MOBA_HEADS))
    grid_spec = pltpu.PrefetchScalarGridSpec(
        num_scalar_prefetch=1,
        grid=(n_pairs, batch),
        in_specs=[pl.BlockSpec((None, seq, LANES), lambda p, b, *_: (b, 0, p)),
                  pl.BlockSpec((None, seq, LANES), lambda p, b, *_: (b, 0, p)),
                  pl.BlockSpec((None, seq, LANES), lambda p, b, *_: (b, 0, n_pairs + p))],
        out_specs=pl.BlockSpec((None, seq, LANES), lambda p, b, *_: (b, 0, p)),
        scratch_shapes=[pltpu.VMEM((LANES // MOBA_HEAD_DIM, seq, LANES), BF16),
                        pltpu.VMEM((nb, LANES), F32)],
    )
    return pl.pallas_call(
        _moba_kernel,
        grid_spec=grid_spec,
        out_shape=jax.ShapeDtypeStruct((batch, seq, n_pairs * LANES), BF16),
        compiler_params=pltpu.CompilerParams(
            dimension_semantics=("parallel", "arbitrary"),
            vmem_limit_bytes=V7X_VMEM_LIMIT_BYTES),
        name="moba",
    )(slopes, qproj3, kv3, kv3)


def kernel(x, mem, w_in_a, ret_norm_gain, w_out_a, kv_norm_gain, w_kv_shared, w_in_b, w_out_b,
           w_mem_kv, norm_pre_mix, norm_post_mix, norm_pre_mlp, norm_post_mlp, w_up, w_down):
    batch, seq, d = x.shape
    m = batch * seq
    depth = norm_pre_mix.shape[0]
    n_a = depth // 2
    qk_w = RET_HEADS * RET_QK_DIM
    v_w = RET_HEADS * RET_V_DIM
    mem_w = MEM_HEADS * MEM_HEAD_DIM
    moba_w = MOBA_HEADS * MOBA_HEAD_DIM
    row_vec = lambda g: g.reshape(1, -1)
    mem_q_scale = jnp.full((1, mem_w), MEM_HEAD_DIM ** -0.5 * LOG2_E, F32)
    moba_q_scale = jnp.full((1, moba_w), MOBA_HEAD_DIM ** -0.5 * LOG2_E, F32)
    scale_a = jnp.concatenate(
        [jnp.ones((1, 2 * qk_w + 2 * v_w), F32), mem_q_scale], axis=1)
    scale_b = jnp.concatenate([moba_q_scale, mem_q_scale], axis=1)
    scale_kv = jnp.ones((1, 2 * moba_w), F32)

    h = x.reshape(m, d)
    kv3 = None
    for l in range(depth):
        if l < n_a:
            y, q_m = _proj_retention(h, norm_pre_mix[l:l + 1], w_in_a[l].astype(BF16), scale_a,
                                     row_vec(ret_norm_gain[l]), batch, seq)
            h = _mix_out(y, q_m, 0, mem, w_mem_kv[l].astype(BF16), w_out_a[l].astype(BF16), h,
                         row_vec(norm_post_mix[l]), batch, seq)
        else:
            jb = l - n_a
            if l == n_a:
                qproj, kv = _norm_proj(
                    h, jnp.stack([norm_pre_mix[l], kv_norm_gain]),
                    [w_in_b[jb].astype(BF16), w_kv_shared.astype(BF16)], [scale_b, scale_kv])
                kv3 = kv.reshape(batch, seq, 2 * moba_w)
            else:
                (qproj,) = _norm_proj(h, norm_pre_mix[l:l + 1], [w_in_b[jb].astype(BF16)],
                                      [scale_b])
            a = _moba(qproj.reshape(batch, seq, moba_w + mem_w), kv3)
            h = _mix_out(a.reshape(m, moba_w), qproj, moba_w // mem_w, mem,
                         w_mem_kv[l].astype(BF16), w_out_b[jb].astype(BF16), h,
                         row_vec(norm_post_mix[l]), batch, seq)
        h = _mlp(h, row_vec(norm_pre_mlp[l]), row_vec(norm_post_mlp[l]),
                 w_up[l].astype(BF16), w_down[l].astype(BF16))
    return h.reshape(batch, seq, d)
```

```python
import functools
import math

import numpy as np
import jax
import jax.numpy as jnp
from jax import lax
from jax.experimental import pallas as pl
from jax.experimental.pallas import tpu as pltpu

F32 = jnp.float32
BF16 = jnp.bfloat16

EPS = 1e-6
NEG_INF = -1e30
LOG2_E = math.log2(math.e)

LANES = 128
V7X_VMEM_LIMIT_BYTES = 56 * 1024 * 1024

RET_HEADS = 4
RET_QK_DIM = 128
RET_V_DIM = 192
RET_CHUNK = 128
MOBA_HEADS = 12
MOBA_HEAD_DIM = 64
MOBA_BLOCK = 256
MOBA_TOPK = 3
MEM_HEADS = 4
MEM_HEAD_DIM = 64

ROW_TILE = 1024
DENSE_SUB_ROWS = 512
RET_ROW_TILE = 1024
RET_SUB_ROWS = 256
PROJ_PIECE_COLS = 256
MIX_ROW_TILE = 1024
MIX_SUB_ROWS = 256
FF_CHUNK = 1024


def _rms_scale(x):
    return lax.rsqrt(jnp.mean(x * x, axis=-1, keepdims=True) + EPS)


def _dot(a, b):
    return jnp.dot(a, b, preferred_element_type=F32)


def _dot_nt(a, b):
    return lax.dot_general(a, b, (((1,), (1,)), ((), ())), preferred_element_type=F32)


def _dot_tn(a, b):
    return lax.dot_general(a, b, (((0,), (0,)), ((), ())), preferred_element_type=F32)


def _resident(shape):
    zeros = (0,) * len(shape)
    return pl.BlockSpec(shape, lambda *_: zeros, pipeline_mode=pl.Buffered(1))


def _norm_proj_kernel(x_ref, g_ref, *refs, n_out):
    w_refs, s_refs, o_refs = refs[:n_out], refs[n_out:2 * n_out], refs[2 * n_out:]
    sub = DENSE_SUB_ROWS
    n_sub = x_ref.shape[0] // sub

    def normed(r):
        x = x_ref[r * sub:(r + 1) * sub, :]
        return x * _rms_scale(x)

    xhat, xhat_next = normed(0), None
    for r in range(n_sub):
        rows = slice(r * sub, (r + 1) * sub)
        for i in range(n_out):
            hn = (xhat * g_ref[i:i + 1, :]).astype(BF16)
            out = _dot(hn, w_refs[i][...])
            if i == 0 and r + 1 < n_sub:
                xhat_next = normed(r + 1)
            o_refs[i][rows, :] = (out * s_refs[i][...]).astype(o_refs[i].dtype)
        xhat = xhat_next


def _norm_proj(x2d, gains, weights, col_scales):
    m, d = x2d.shape
    n_out = len(weights)
    return pl.pallas_call(
        functools.partial(_norm_proj_kernel, n_out=n_out),
        grid=(m // ROW_TILE,),
        in_specs=[pl.BlockSpec((ROW_TILE, d), lambda i: (i, 0)),
                  _resident(gains.shape)]
                 + [_resident(w.shape) for w in weights]
                 + [_resident(s.shape) for s in col_scales],
        out_specs=[pl.BlockSpec((ROW_TILE, w.shape[1]), lambda i: (i, 0)) for w in weights],
        out_shape=[jax.ShapeDtypeStruct((m, w.shape[1]), BF16) for w in weights],
        compiler_params=pltpu.CompilerParams(
            dimension_semantics=("parallel",), vmem_limit_bytes=V7X_VMEM_LIMIT_BYTES),
        name="norm_proj",
    )(x2d, gains, *weights, *col_scales)


def _mlp_kernel(h_ref, g_pre_ref, g_post_ref, wu_ref, wd_ref, o_ref):
    sub = DENSE_SUB_ROWS
    n_sub = h_ref.shape[0] // sub
    d_ff = wu_ref.shape[1]

    def normed(r):
        h = h_ref[r * sub:(r + 1) * sub, :]
        return (h * _rms_scale(h) * g_pre_ref[...]).astype(BF16)

    def epilogue(r, y):
        rows = slice(r * sub, (r + 1) * sub)
        o_ref[rows, :] = h_ref[rows, :] + y * _rms_scale(y) * g_post_ref[...]

    hn, hn_next, finished = normed(0), None, None
    for r in range(n_sub):
        y = None
        for c in range(d_ff // FF_CHUNK):
            cols = slice(c * FF_CHUNK, (c + 1) * FF_CHUNK)
            u = _dot(hn, wu_ref[:, cols])
            if c == 0:
                if r + 1 < n_sub:
                    hn_next = normed(r + 1)
                if finished is not None:
                    epilogue(*finished)
            a = jnp.square(jnp.maximum(u, 0.0)).astype(BF16)
            d = _dot(a, wd_ref[cols, :])
            y = d if y is None else y + d
        finished, hn = (r, y), hn_next
    epilogue(*finished)


def _mlp(h2d, g_pre, g_post, w_up, w_down):
    m, d = h2d.shape
    row_spec = pl.BlockSpec((ROW_TILE, d), lambda i: (i, 0))
    return pl.pallas_call(
        _mlp_kernel,
        grid=(m // ROW_TILE,),
        in_specs=[row_spec, _resident(g_pre.shape), _resident(g_post.shape),
                  _resident(w_up.shape), _resident(w_down.shape)],
        out_specs=row_spec,
        out_shape=jax.ShapeDtypeStruct((m, d), F32),
        compiler_params=pltpu.CompilerParams(
            dimension_semantics=("parallel",), vmem_limit_bytes=V7X_VMEM_LIMIT_BYTES),
        name="mlp",
    )(h2d, g_pre, g_post, w_up, w_down)


def _proj_retention_kernel(x_ref, g_pre_ref, w_ref, scale_ref, gain_ref, y_ref, qm_ref,
                           state_ref):
    c = RET_CHUNK
    sub = RET_SUB_ROWS
    qk_w = RET_HEADS * RET_QK_DIM
    v_w = RET_HEADS * RET_V_DIM

    @pl.when(pl.program_id(1) == 0)
    def _():
        state_ref[...] = jnp.zeros(state_ref.shape, F32)

    row = lax.broadcasted_iota(jnp.int32, (c, c), 0)
    col = lax.broadcasted_iota(jnp.int32, (c, c), 1)
    rel = (row - col).astype(F32)
    pos = lax.broadcasted_iota(jnp.int32, (c, 1), 0).astype(F32)
    qk_scale = RET_QK_DIM ** -0.5

    tables = []
    for h in range(RET_HEADS):
        log_g = float(np.log1p(-np.exp2(np.float32(-5.0 - h))))
        d_intra = jnp.where(rel >= 0, jnp.exp(log_g * jnp.maximum(rel, 0.0)), 0.0) * qk_scale
        zeta = jnp.exp(log_g * (c - 1.0 - pos)) * qk_scale
        xi = jnp.exp(log_g * (pos + 1.0))
        chunk_decay = float(np.exp(np.float32(log_g * c)))
        tables.append((d_intra, zeta, xi, chunk_decay))

    n_pieces = w_ref.shape[1] // PROJ_PIECE_COLS

    def project_pieces(r):
        x = x_ref[r * sub:(r + 1) * sub, :]
        hn = (x * _rms_scale(x) * g_pre_ref[...]).astype(BF16)

        def piece(n):
            cols = slice(n * PROJ_PIECE_COLS, (n + 1) * PROJ_PIECE_COLS)
            return (_dot(hn, w_ref[:, cols]) * scale_ref[:, cols]).astype(BF16)
        return [functools.partial(piece, n) for n in range(n_pieces)]

    def retain(r, proj, next_pieces):
        qm_ref[r * sub:(r + 1) * sub, :] = proj[:, 2 * qk_w + 2 * v_w:]
        bodies = [(ci, h) for ci in range(sub // c) for h in range(RET_HEADS)]
        done = []
        for i, (ci, h) in enumerate(bodies):
            while len(done) < (i + 1) * len(next_pieces) // len(bodies):
                done.append(next_pieces[len(done)]())
            rows = slice(ci * c, (ci + 1) * c)
            out_rows = slice(r * sub + ci * c, r * sub + (ci + 1) * c)
            d_intra, zeta, xi, chunk_decay = tables[h]
            v_cols = slice(h * RET_V_DIM, (h + 1) * RET_V_DIM)
            q = proj[rows, h * RET_QK_DIM:(h + 1) * RET_QK_DIM]
            k = proj[rows, qk_w + h * RET_QK_DIM:qk_w + (h + 1) * RET_QK_DIM]
            v = proj[rows, 2 * qk_w + h * RET_V_DIM:2 * qk_w + (h + 1) * RET_V_DIM]
            g = proj[rows, 2 * qk_w + v_w + h * RET_V_DIM:
                     2 * qk_w + v_w + (h + 1) * RET_V_DIM].astype(F32)
            state = state_ref[h]
            s = _dot_nt(q, k) * d_intra
            y = _dot(s.astype(BF16), v) + _dot(q, state.astype(BF16)) * xi
            kz = (k.astype(F32) * zeta).astype(BF16)
            state_ref[h] = state * chunk_decay + _dot_tn(kz, v)
            ms = jnp.mean(y * y, axis=-1, keepdims=True)
            yn = y * lax.rsqrt(ms + EPS) * gain_ref[:, v_cols]
            y_ref[out_rows, v_cols] = (g / (1.0 + jnp.exp2(g * -LOG2_E)) * yn).astype(
                y_ref.dtype)
        return done

    n_groups = x_ref.shape[0] // sub
    proj = jnp.concatenate([piece() for piece in project_pieces(0)], axis=1)
    for r in range(n_groups):
        next_pieces = project_pieces(r + 1) if r + 1 < n_groups else []
        done = retain(r, proj, next_pieces)
        if done:
            proj = jnp.concatenate(done, axis=1)


def _proj_retention(x2d, g_pre, w_in, col_scale, gn_gain, batch, seq):
    m, d = x2d.shape
    steps = seq // RET_ROW_TILE
    v_w = RET_HEADS * RET_V_DIM
    mem_w = MEM_HEADS * MEM_HEAD_DIM
    row_block = lambda width: pl.BlockSpec((RET_ROW_TILE, width), lambda b, s: (b * steps + s, 0))
    return pl.pallas_call(
        _proj_retention_kernel,
        grid=(batch, steps),
        in_specs=[row_block(d), _resident(g_pre.shape), _resident(w_in.shape),
                  _resident(col_scale.shape), _resident(gn_gain.shape)],
        out_specs=[row_block(v_w), row_block(mem_w)],
        out_shape=[jax.ShapeDtypeStruct((m, v_w), BF16), jax.ShapeDtypeStruct((m, mem_w), BF16)],
        scratch_shapes=[pltpu.VMEM((RET_HEADS, RET_QK_DIM, RET_V_DIM), F32)],
        compiler_params=pltpu.CompilerParams(
            dimension_semantics=("parallel", "arbitrary"),
            vmem_limit_bytes=V7X_VMEM_LIMIT_BYTES),
        name="proj_retention",
    )(x2d, g_pre, w_in, col_scale, gn_gain)


def _mix_out_kernel(a_ref, qm_ref, mem_ref, wmkv_ref, wout_ref, h_ref, gain_ref, o_ref,
                    mk_ref, mv_ref):
    mem_w = MEM_HEADS * MEM_HEAD_DIM
    groups = mem_w // LANES
    heads_per_group = LANES // MEM_HEAD_DIM
    lane = lax.broadcasted_iota(jnp.int32, (1, LANES), 1)
    head_lanes = [(lane >= hh * MEM_HEAD_DIM) & (lane < (hh + 1) * MEM_HEAD_DIM)
                  for hh in range(heads_per_group)]

    @pl.when(pl.program_id(1) == 0)
    def _():
        mkv = _dot(mem_ref[...].astype(BF16), wmkv_ref[...])
        mk_ref[...] = mkv[:, :mem_w].astype(BF16)
        for p in range(groups):
            mvp = mkv[:, mem_w + p * LANES:mem_w + (p + 1) * LANES].astype(BF16)
            mv_ref[p] = jnp.concatenate([mvp, jnp.ones_like(mvp)], axis=-1)

    a_w = a_ref.shape[1]
    sub = MIX_SUB_ROWS

    n_sub = a_ref.shape[0] // sub

    def scores(p, hh):
        cols = slice(p * LANES, (p + 1) * LANES)
        qp = qm_ref[:, cols]
        qh = jnp.where(head_lanes[hh], qp, jnp.zeros_like(qp))
        s = _dot_nt(qh, mk_ref[:, cols])
        return s, jnp.max(s, axis=-1, keepdims=True)

    def attend(p, hh, s, m_row):
        acc = _dot(jnp.exp2(s - m_row).astype(BF16), mv_ref[p])
        return acc[:, :LANES] / acc[:, LANES:]

    def project_a(r):
        return _dot(a_ref[r * sub:(r + 1) * sub, :], wout_ref[:a_w, :])

    def epilogue(r, mix):
        rows = slice(r * sub, (r + 1) * sub)
        o_ref[rows, :] = h_ref[rows, :] + mix * _rms_scale(mix) * gain_ref[...]

    work = [(p, hh) for p in range(groups) for hh in range(heads_per_group)]
    outs, mix_a = {}, {}
    pending = scores(*work[0])
    for i, (p, hh) in enumerate(work):
        current = pending
        if i + 1 < len(work):
            pending = scores(*work[i + 1])
        if i < n_sub // 2:
            mix_a[i] = project_a(i)
        outs[p, hh] = attend(p, hh, *current)
    mem_out = jnp.concatenate(
        [jnp.where(head_lanes[0], outs[p, 0], outs[p, 1]).astype(BF16) for p in range(groups)],
        axis=-1)
    projected = []
    for r in range(n_sub):
        if r not in mix_a:
            mix_a[r] = project_a(r)
            while projected:
                epilogue(*projected.pop(0))
        mix = mix_a.pop(r) + _dot(mem_out[r * sub:(r + 1) * sub], wout_ref[a_w:, :])
        projected.append((r, mix))
    while projected:
        epilogue(*projected.pop(0))


def _mix_out(a2d, qm_src, qm_col_block, mem, w_mkv, w_out, h2d, gain, batch, seq):
    m, d = h2d.shape
    a_w = a2d.shape[1]
    mem_w = MEM_HEADS * MEM_HEAD_DIM
    mem_len = mem.shape[1]
    steps = seq // MIX_ROW_TILE
    row = lambda b, s: b * steps + s
    return pl.pallas_call(
        _mix_out_kernel,
        grid=(batch, steps),
        in_specs=[pl.BlockSpec((MIX_ROW_TILE, a_w), lambda b, s: (row(b, s), 0)),
                  pl.BlockSpec((MIX_ROW_TILE, mem_w), lambda b, s: (row(b, s), qm_col_block)),
                  pl.BlockSpec((None, mem_len, d), lambda b, s: (b, 0, 0)),
                  _resident(w_mkv.shape), _resident(w_out.shape),
                  pl.BlockSpec((MIX_ROW_TILE, d), lambda b, s: (row(b, s), 0)),
                  _resident(gain.shape)],
        out_specs=pl.BlockSpec((MIX_ROW_TILE, d), lambda b, s: (row(b, s), 0)),
        out_shape=jax.ShapeDtypeStruct((m, d), F32),
        scratch_shapes=[pltpu.VMEM((mem_len, mem_w), BF16),
                        pltpu.VMEM((mem_w // LANES, mem_len, 2 * LANES), BF16)],
        compiler_params=pltpu.CompilerParams(
            dimension_semantics=("parallel", "arbitrary"),
            vmem_limit_bytes=V7X_VMEM_LIMIT_BYTES),
        name="mix_out",
    )(a2d, qm_src, mem, w_mkv, w_out, h2d, gain)


def _moba_kernel(slopes_ref, q_ref, k_ref, v_ref, o_ref, tmpl_ref, km_ref):
    seq = k_ref.shape[0]
    bs = MOBA_BLOCK
    nb = seq // bs
    heads_per_group = LANES // MOBA_HEAD_DIM
    first_gated = MOBA_TOPK + 1
    p = pl.program_id(0)
    lane = lax.broadcasted_iota(jnp.int32, (1, LANES), 1)
    bias_lane0 = [nb + 3 * hh for hh in range(heads_per_group)]

    @pl.when(pl.program_id(1) == 0)
    def _():
        posi = lax.broadcasted_iota(jnp.int32, (seq, 1), 0)
        pos = posi.astype(F32)
        t = jnp.where(lane == posi // bs, 1.0, 0.0)
        for hh in range(heads_per_group):
            x = (slopes_ref[heads_per_group * p + hh] * LOG2_E) * pos
            hi = x.astype(BF16).astype(F32)
            mid = (x - hi).astype(BF16).astype(F32)
            lo = (x - hi - mid).astype(BF16).astype(F32)
            t = jnp.where(lane == bias_lane0[hh], hi, t)
            t = jnp.where(lane == bias_lane0[hh] + 1, mid, t)
            t = jnp.where(lane == bias_lane0[hh] + 2, lo, t)
        tmpl_ref[...] = t.astype(BF16)

    k_all = k_ref[...]
    for n in range(nb):
        kb = k_all[n * bs:(n + 1) * bs, :].astype(F32)
        km_ref[n:n + 1, :] = jnp.sum(kb, axis=0, keepdims=True) * (1.0 / bs)
    km = km_ref[...]
    km_hi = km.astype(BF16)
    km_lo = (km - km_hi.astype(F32)).astype(BF16)

    rowi = lax.broadcasted_iota(jnp.int32, (bs, bs), 0)
    coli = lax.broadcasted_iota(jnp.int32, (bs, bs), 1)
    causal = coli <= rowi
    blk_id = lax.broadcasted_iota(jnp.int32, (nb, bs), 0)
    q_all = q_ref[...]
    v_all = v_ref[...]
    k_ext = jnp.concatenate([k_all, tmpl_ref[...]], axis=-1)
    v_ext = jnp.concatenate([v_all, jnp.ones_like(v_all)], axis=-1)
    gated_rows = slice(first_gated * bs, seq)
    in_head, q_ext = [], []
    for hh in range(heads_per_group):
        head_lanes = (lane >= hh * MOBA_HEAD_DIM) & (lane < (hh + 1) * MOBA_HEAD_DIM)
        qh = jnp.where(head_lanes, q_all, jnp.zeros_like(q_all))
        gate_t = _dot_nt(km_hi, qh[gated_rows]) + _dot_nt(km_lo, qh[gated_rows])
        bias_t = []
        for j in range(first_gated, nb):
            g = gate_t[:, (j - first_gated) * bs:(j - first_gated + 1) * bs]
            beaten_by = jnp.zeros((nb, bs), F32)
            for mth in range(j):
                gm = g[mth:mth + 1, :]
                wins = (gm > g) | ((gm == g) & (blk_id > mth))
                beaten_by = beaten_by + jnp.where(wins, 1.0, 0.0)
            keep = (beaten_by < float(MOBA_TOPK)) | (blk_id >= j)
            bias_t.append(jnp.where(keep, 0.0, NEG_INF))
        bias_t = jnp.concatenate(bias_t, axis=1)
        block_bias = jnp.concatenate(
            [bias_t, jnp.zeros((LANES - nb, bias_t.shape[1]), F32)], axis=0).T
        ones_lanes = jnp.where((lane >= bias_lane0[hh]) & (lane < bias_lane0[hh] + 3), 1.0, 0.0)
        gated_aug = (block_bias + ones_lanes).astype(BF16)
        blocks = []
        for j in range(nb):
            aug = (gated_aug[(j - first_gated) * bs:(j - first_gated + 1) * bs]
                   if j >= first_gated
                   else jnp.broadcast_to(ones_lanes.astype(BF16), (bs, LANES)))
            blocks.append(jnp.concatenate([qh[j * bs:(j + 1) * bs], aug], axis=-1))
        in_head.append(head_lanes)
        q_ext.append(blocks)

    def scores(j, hh):
        rows = slice(j * bs, (j + 1) * bs)
        qa = q_ext[hh][j]
        s_own = jnp.where(causal, _dot_nt(qa, k_ext[rows]), NEG_INF)
        m_row = jnp.max(s_own, axis=-1, keepdims=True)
        s_past = None
        if j:
            s_past = _dot_nt(qa, k_ext[:j * bs])
            m_row = jnp.maximum(m_row, jnp.max(s_past, axis=-1, keepdims=True))
        return s_own, s_past, m_row

    def attend(j, hh, s_own, s_past, m_row):
        rows = slice(j * bs, (j + 1) * bs)
        acc = _dot(jnp.exp2(s_own - m_row).astype(BF16), v_ext[rows])
        if j:
            acc = acc + _dot(jnp.exp2(s_past - m_row).astype(BF16), v_ext[:j * bs])
        return acc[:, :LANES] / acc[:, LANES:]

    order = [j for pair in zip(range(nb - 1, -1, -1), range(nb)) for j in pair][:nb]
    work = [(j, hh) for j in order for hh in range(heads_per_group)]
    head_out = {}
    pending = scores(*work[0])
    for i, (j, hh) in enumerate(work):
        current = pending
        if i + 1 < len(work):
            pending = scores(*work[i + 1])
        head_out[hh] = attend(j, hh, *current)
        if hh == heads_per_group - 1:
            rows = slice(j * bs, (j + 1) * bs)
            o_ref[rows, :] = jnp.where(in_head[0], head_out[0], head_out[1]).astype(o_ref.dtype)


def _alibi_slopes(n):
    def pow2(m):
        return [2.0 ** (-8.0 * (i + 1) / m) for i in range(m)]
    p = 2 ** int(math.floor(math.log2(n)))
    s = pow2(p)
    if p < n:
        s = s + pow2(2 * p)[0::2][: n - p]
    return np.asarray(s, dtype=np.float32)


def _moba(qproj3, kv3):
    batch, seq, _ = qproj3.shape
    n_pairs = MOBA_HEADS * MOBA_HEAD_DIM // LANES
    nb = seq // MOBA_BLOCK
    slopes = jnp.asarray(_alibi_slopes(MOBA_HEADS))
    grid_spec = pltpu.PrefetchScalarGridSpec(
        num_scalar_prefetch=1,
        grid=(n_pairs, batch),
        in_specs=[pl.BlockSpec((None, seq, LANES), lambda p, b, *_: (b, 0, p)),
                  pl.BlockSpec((None, seq, LANES), lambda p, b, *_: (b, 0, p)),
                  pl.BlockSpec((None, seq, LANES), lambda p, b, *_: (b, 0, n_pairs + p))],
        out_specs=pl.BlockSpec((None, seq, LANES), lambda p, b, *_: (b, 0, p)),
        scratch_shapes=[pltpu.VMEM((seq, LANES), BF16), pltpu.VMEM((nb, LANES), F32)],
    )
    return pl.pallas_call(
        _moba_kernel,
        grid_spec=grid_spec,
        out_shape=jax.ShapeDtypeStruct((batch, seq, n_pairs * LANES), BF16),
        compiler_params=pltpu.CompilerParams(
            dimension_semantics=("parallel", "arbitrary"),
            vmem_limit_bytes=V7X_VMEM_LIMIT_BYTES),
        name="moba",
    )(slopes, qproj3, kv3, kv3)


def kernel(x, mem, w_in_a, ret_norm_gain, w_out_a, kv_norm_gain, w_kv_shared, w_in_b, w_out_b,
           w_mem_kv, norm_pre_mix, norm_post_mix, norm_pre_mlp, norm_post_mlp, w_up, w_down):
    batch, seq, d = x.shape
    m = batch * seq
    depth = norm_pre_mix.shape[0]
    n_a = depth // 2
    qk_w = RET_HEADS * RET_QK_DIM
    v_w = RET_HEADS * RET_V_DIM
    mem_w = MEM_HEADS * MEM_HEAD_DIM
    moba_w = MOBA_HEADS * MOBA_HEAD_DIM
    row_vec = lambda g: g.reshape(1, -1)
    mem_q_scale = jnp.full((1, mem_w), MEM_HEAD_DIM ** -0.5 * LOG2_E, F32)
    moba_q_scale = jnp.full((1, moba_w), MOBA_HEAD_DIM ** -0.5 * LOG2_E, F32)
    scale_a = jnp.concatenate(
        [jnp.ones((1, 2 * qk_w + 2 * v_w), F32), mem_q_scale], axis=1)
    scale_b = jnp.concatenate([moba_q_scale, mem_q_scale], axis=1)
    scale_kv = jnp.ones((1, 2 * moba_w), F32)

    h = x.reshape(m, d)
    kv3 = None
    for l in range(depth):
        if l < n_a:
            y, q_m = _proj_retention(h, norm_pre_mix[l:l + 1], w_in_a[l].astype(BF16), scale_a,
                                     row_vec(ret_norm_gain[l]), batch, seq)
            h = _mix_out(y, q_m, 0, mem, w_mem_kv[l].astype(BF16), w_out_a[l].astype(BF16), h,
                         row_vec(norm_post_mix[l]), batch, seq)
        else:
            jb = l - n_a
            if l == n_a:
                qproj, kv = _norm_proj(
                    h, jnp.stack([norm_pre_mix[l], kv_norm_gain]),
                    [w_in_b[jb].astype(BF16), w_kv_shared.astype(BF16)], [scale_b, scale_kv])
                kv3 = kv.reshape(batch, seq, 2 * moba_w)
            else:
                (qproj,) = _norm_proj(h, norm_pre_mix[l:l + 1], [w_in_b[jb].astype(BF16)],
                                      [scale_b])
            a = _moba(qproj.reshape(batch, seq, moba_w + mem_w), kv3)
            h = _mix_out(a.reshape(m, moba_w), qproj, moba_w // mem_w, mem,
                         w_mem_kv[l].astype(BF16), w_out_b[jb].astype(BF16), h,
                         row_vec(norm_post_mix[l]), batch, seq)
        h = _mlp(h, row_vec(norm_pre_mlp[l]), row_vec(norm_post_mlp[l]),
                 w_up[l].astype(BF16), w_down[l].astype(BF16))
    return h.reshape(batch, seq, d)
```

```python
import functools
import math

import numpy as np
import jax
import jax.numpy as jnp
from jax import lax
from jax.experimental import pallas as pl
from jax.experimental.pallas import tpu as pltpu

F32 = jnp.float32
BF16 = jnp.bfloat16

EPS = 1e-6
NEG_INF = -1e30
LOG2_E = math.log2(math.e)

LANES = 128
V7X_VMEM_LIMIT_BYTES = 56 * 1024 * 1024

RET_HEADS = 4
RET_QK_DIM = 128
RET_V_DIM = 192
RET_CHUNK = 128
MOBA_HEADS = 12
MOBA_HEAD_DIM = 64
MOBA_BLOCK = 256
MOBA_TOPK = 3
MOBA_GROUPS_PER_STEP = 2
MEM_HEADS = 4
MEM_HEAD_DIM = 64

ROW_TILE = 1024
DENSE_SUB_ROWS = 512
RET_ROW_TILE = 1024
RET_SUB_ROWS = 256
PROJ_PIECE_COLS = 256
MIX_ROW_TILE = 1024
MIX_SUB_ROWS = 256
FF_CHUNK = 1024


def _rms_scale(x):
    return lax.rsqrt(jnp.mean(x * x, axis=-1, keepdims=True) + EPS)


def _dot(a, b):
    return jnp.dot(a, b, preferred_element_type=F32)


def _dot_nt(a, b):
    return lax.dot_general(a, b, (((1,), (1,)), ((), ())), preferred_element_type=F32)


def _dot_tn(a, b):
    return lax.dot_general(a, b, (((0,), (0,)), ((), ())), preferred_element_type=F32)


def _resident(shape):
    zeros = (0,) * len(shape)
    return pl.BlockSpec(shape, lambda *_: zeros, pipeline_mode=pl.Buffered(1))


def _norm_proj_kernel(x_ref, g_ref, *refs, n_out):
    w_refs, s_refs, o_refs = refs[:n_out], refs[n_out:2 * n_out], refs[2 * n_out:]
    sub = DENSE_SUB_ROWS
    n_sub = x_ref.shape[0] // sub

    def normed(r):
        x = x_ref[r * sub:(r + 1) * sub, :]
        return x * _rms_scale(x)

    xhat, xhat_next = normed(0), None
    for r in range(n_sub):
        rows = slice(r * sub, (r + 1) * sub)
        for i in range(n_out):
            hn = (xhat * g_ref[i:i + 1, :]).astype(BF16)
            out = _dot(hn, w_refs[i][...])
            if i == 0 and r + 1 < n_sub:
                xhat_next = normed(r + 1)
            o_refs[i][rows, :] = (out * s_refs[i][...]).astype(o_refs[i].dtype)
        xhat = xhat_next


def _norm_proj(x2d, gains, weights, col_scales):
    m, d = x2d.shape
    n_out = len(weights)
    return pl.pallas_call(
        functools.partial(_norm_proj_kernel, n_out=n_out),
        grid=(m // ROW_TILE,),
        in_specs=[pl.BlockSpec((ROW_TILE, d), lambda i: (i, 0)),
                  _resident(gains.shape)]
                 + [_resident(w.shape) for w in weights]
                 + [_resident(s.shape) for s in col_scales],
        out_specs=[pl.BlockSpec((ROW_TILE, w.shape[1]), lambda i: (i, 0)) for w in weights],
        out_shape=[jax.ShapeDtypeStruct((m, w.shape[1]), BF16) for w in weights],
        compiler_params=pltpu.CompilerParams(
            dimension_semantics=("parallel",), vmem_limit_bytes=V7X_VMEM_LIMIT_BYTES),
        name="norm_proj",
    )(x2d, gains, *weights, *col_scales)


def _mlp_kernel(h_ref, g_pre_ref, g_post_ref, wu_ref, wd_ref, o_ref):
    sub = DENSE_SUB_ROWS
    n_sub = h_ref.shape[0] // sub
    d_ff = wu_ref.shape[1]

    def normed(r):
        h = h_ref[r * sub:(r + 1) * sub, :]
        return (h * _rms_scale(h) * g_pre_ref[...]).astype(BF16)

    def epilogue(r, y):
        rows = slice(r * sub, (r + 1) * sub)
        o_ref[rows, :] = h_ref[rows, :] + y * _rms_scale(y) * g_post_ref[...]

    hn, hn_next, finished = normed(0), None, None
    for r in range(n_sub):
        y = None
        for c in range(d_ff // FF_CHUNK):
            cols = slice(c * FF_CHUNK, (c + 1) * FF_CHUNK)
            u = _dot(hn, wu_ref[:, cols])
            if c == 0:
                if r + 1 < n_sub:
                    hn_next = normed(r + 1)
                if finished is not None:
                    epilogue(*finished)
            a = jnp.square(jnp.maximum(u, 0.0)).astype(BF16)
            d = _dot(a, wd_ref[cols, :])
            y = d if y is None else y + d
        finished, hn = (r, y), hn_next
    epilogue(*finished)


def _mlp(h2d, g_pre, g_post, w_up, w_down):
    m, d = h2d.shape
    row_spec = pl.BlockSpec((ROW_TILE, d), lambda i: (i, 0))
    return pl.pallas_call(
        _mlp_kernel,
        grid=(m // ROW_TILE,),
        in_specs=[row_spec, _resident(g_pre.shape), _resident(g_post.shape),
                  _resident(w_up.shape), _resident(w_down.shape)],
        out_specs=row_spec,
        out_shape=jax.ShapeDtypeStruct((m, d), F32),
        compiler_params=pltpu.CompilerParams(
            dimension_semantics=("parallel",), vmem_limit_bytes=V7X_VMEM_LIMIT_BYTES),
        name="mlp",
    )(h2d, g_pre, g_post, w_up, w_down)


def _proj_retention_kernel(x_ref, g_pre_ref, w_ref, scale_ref, gain_ref, y_ref, qm_ref,
                           state_ref):
    c = RET_CHUNK
    sub = RET_SUB_ROWS
    qk_w = RET_HEADS * RET_QK_DIM
    v_w = RET_HEADS * RET_V_DIM

    @pl.when(pl.program_id(1) == 0)
    def _():
        state_ref[...] = jnp.zeros(state_ref.shape, F32)

    row = lax.broadcasted_iota(jnp.int32, (c, c), 0)
    col = lax.broadcasted_iota(jnp.int32, (c, c), 1)
    rel = (row - col).astype(F32)
    pos = lax.broadcasted_iota(jnp.int32, (c, 1), 0).astype(F32)
    qk_scale = RET_QK_DIM ** -0.5

    tables = []
    for h in range(RET_HEADS):
        log_g = float(np.log1p(-np.exp2(np.float32(-5.0 - h))))
        d_intra = jnp.where(rel >= 0, jnp.exp(log_g * jnp.maximum(rel, 0.0)), 0.0) * qk_scale
        zeta = jnp.exp(log_g * (c - 1.0 - pos)) * qk_scale
        xi = jnp.exp(log_g * (pos + 1.0))
        chunk_decay = float(np.exp(np.float32(log_g * c)))
        tables.append((d_intra, zeta, xi, chunk_decay))

    n_pieces = w_ref.shape[1] // PROJ_PIECE_COLS

    def project_pieces(r):
        x = x_ref[r * sub:(r + 1) * sub, :]
        hn = (x * _rms_scale(x) * g_pre_ref[...]).astype(BF16)

        def piece(n):
            cols = slice(n * PROJ_PIECE_COLS, (n + 1) * PROJ_PIECE_COLS)
            return (_dot(hn, w_ref[:, cols]) * scale_ref[:, cols]).astype(BF16)
        return [functools.partial(piece, n) for n in range(n_pieces)]

    def retain(r, proj, next_pieces):
        qm_ref[r * sub:(r + 1) * sub, :] = proj[:, 2 * qk_w + 2 * v_w:]
        bodies = [(ci, h) for ci in range(sub // c) for h in range(RET_HEADS)]
        done = []
        for i, (ci, h) in enumerate(bodies):
            while len(done) < (i + 1) * len(next_pieces) // len(bodies):
                done.append(next_pieces[len(done)]())
            rows = slice(ci * c, (ci + 1) * c)
            out_rows = slice(r * sub + ci * c, r * sub + (ci + 1) * c)
            d_intra, zeta, xi, chunk_decay = tables[h]
            v_cols = slice(h * RET_V_DIM, (h + 1) * RET_V_DIM)
            q = proj[rows, h * RET_QK_DIM:(h + 1) * RET_QK_DIM]
            k = proj[rows, qk_w + h * RET_QK_DIM:qk_w + (h + 1) * RET_QK_DIM]
            v = proj[rows, 2 * qk_w + h * RET_V_DIM:2 * qk_w + (h + 1) * RET_V_DIM]
            g = proj[rows, 2 * qk_w + v_w + h * RET_V_DIM:
                     2 * qk_w + v_w + (h + 1) * RET_V_DIM].astype(F32)
            state = state_ref[h]
            s = _dot_nt(q, k) * d_intra
            y = _dot(s.astype(BF16), v) + _dot(q, state.astype(BF16)) * xi
            kz = (k.astype(F32) * zeta).astype(BF16)
            state_ref[h] = state * chunk_decay + _dot_tn(kz, v)
            ms = jnp.mean(y * y, axis=-1, keepdims=True)
            yn = y * lax.rsqrt(ms + EPS) * gain_ref[:, v_cols]
            y_ref[out_rows, v_cols] = (g / (1.0 + jnp.exp2(g * -LOG2_E)) * yn).astype(
                y_ref.dtype)
        return done

    n_groups = x_ref.shape[0] // sub
    proj = jnp.concatenate([piece() for piece in project_pieces(0)], axis=1)
    for r in range(n_groups):
        next_pieces = project_pieces(r + 1) if r + 1 < n_groups else []
        done = retain(r, proj, next_pieces)
        if done:
            proj = jnp.concatenate(done, axis=1)


def _proj_retention(x2d, g_pre, w_in, col_scale, gn_gain, batch, seq):
    m, d = x2d.shape
    steps = seq // RET_ROW_TILE
    v_w = RET_HEADS * RET_V_DIM
    mem_w = MEM_HEADS * MEM_HEAD_DIM
    row_block = lambda width: pl.BlockSpec((RET_ROW_TILE, width), lambda b, s: (b * steps + s, 0))
    return pl.pallas_call(
        _proj_retention_kernel,
        grid=(batch, steps),
        in_specs=[row_block(d), _resident(g_pre.shape), _resident(w_in.shape),
                  _resident(col_scale.shape), _resident(gn_gain.shape)],
        out_specs=[row_block(v_w), row_block(mem_w)],
        out_shape=[jax.ShapeDtypeStruct((m, v_w), BF16), jax.ShapeDtypeStruct((m, mem_w), BF16)],
        scratch_shapes=[pltpu.VMEM((RET_HEADS, RET_QK_DIM, RET_V_DIM), F32)],
        compiler_params=pltpu.CompilerParams(
            dimension_semantics=("parallel", "arbitrary"),
            vmem_limit_bytes=V7X_VMEM_LIMIT_BYTES),
        name="proj_retention",
    )(x2d, g_pre, w_in, col_scale, gn_gain)


def _mix_out_kernel(a_ref, qm_ref, mem_ref, wmkv_ref, wout_ref, h_ref, gain_ref, o_ref,
                    mk_ref, mv_ref):
    mem_w = MEM_HEADS * MEM_HEAD_DIM
    groups = mem_w // LANES
    heads_per_group = LANES // MEM_HEAD_DIM
    lane = lax.broadcasted_iota(jnp.int32, (1, LANES), 1)
    head_lanes = [(lane >= hh * MEM_HEAD_DIM) & (lane < (hh + 1) * MEM_HEAD_DIM)
                  for hh in range(heads_per_group)]

    @pl.when(pl.program_id(1) == 0)
    def _():
        mkv = _dot(mem_ref[...].astype(BF16), wmkv_ref[...])
        mk_ref[...] = mkv[:, :mem_w].astype(BF16)
        for p in range(groups):
            mvp = mkv[:, mem_w + p * LANES:mem_w + (p + 1) * LANES].astype(BF16)
            mv_ref[p] = jnp.concatenate([mvp, jnp.ones_like(mvp)], axis=-1)

    a_w = a_ref.shape[1]
    sub = MIX_SUB_ROWS

    n_sub = a_ref.shape[0] // sub

    def scores(p, hh):
        cols = slice(p * LANES, (p + 1) * LANES)
        qp = qm_ref[:, cols]
        qh = jnp.where(head_lanes[hh], qp, jnp.zeros_like(qp))
        s = _dot_nt(qh, mk_ref[:, cols])
        return s, jnp.max(s, axis=-1, keepdims=True)

    def attend(p, hh, s, m_row):
        acc = _dot(jnp.exp2(s - m_row).astype(BF16), mv_ref[p])
        return acc[:, :LANES] / acc[:, LANES:]

    def project_a(r):
        return _dot(a_ref[r * sub:(r + 1) * sub, :], wout_ref[:a_w, :])

    def epilogue(r, mix):
        rows = slice(r * sub, (r + 1) * sub)
        o_ref[rows, :] = h_ref[rows, :] + mix * _rms_scale(mix) * gain_ref[...]

    work = [(p, hh) for p in range(groups) for hh in range(heads_per_group)]
    outs, mix_a = {}, {}
    pending = scores(*work[0])
    for i, (p, hh) in enumerate(work):
        current = pending
        if i + 1 < len(work):
            pending = scores(*work[i + 1])
        if i < n_sub // 2:
            mix_a[i] = project_a(i)
        outs[p, hh] = attend(p, hh, *current)
    mem_out = jnp.concatenate(
        [jnp.where(head_lanes[0], outs[p, 0], outs[p, 1]).astype(BF16) for p in range(groups)],
        axis=-1)
    projected = []
    for r in range(n_sub):
        if r not in mix_a:
            mix_a[r] = project_a(r)
            while projected:
                epilogue(*projected.pop(0))
        mix = mix_a.pop(r) + _dot(mem_out[r * sub:(r + 1) * sub], wout_ref[a_w:, :])
        projected.append((r, mix))
    while projected:
        epilogue(*projected.pop(0))


def _mix_out(a2d, qm_src, qm_col_block, mem, w_mkv, w_out, h2d, gain, batch, seq):
    m, d = h2d.shape
    a_w = a2d.shape[1]
    mem_w = MEM_HEADS * MEM_HEAD_DIM
    mem_len = mem.shape[1]
    steps = seq // MIX_ROW_TILE
    row = lambda b, s: b * steps + s
    return pl.pallas_call(
        _mix_out_kernel,
        grid=(batch, steps),
        in_specs=[pl.BlockSpec((MIX_ROW_TILE, a_w), lambda b, s: (row(b, s), 0)),
                  pl.BlockSpec((MIX_ROW_TILE, mem_w), lambda b, s: (row(b, s), qm_col_block)),
                  pl.BlockSpec((None, mem_len, d), lambda b, s: (b, 0, 0)),
                  _resident(w_mkv.shape), _resident(w_out.shape),
                  pl.BlockSpec((MIX_ROW_TILE, d), lambda b, s: (row(b, s), 0)),
                  _resident(gain.shape)],
        out_specs=pl.BlockSpec((MIX_ROW_TILE, d), lambda b, s: (row(b, s), 0)),
        out_shape=jax.ShapeDtypeStruct((m, d), F32),
        scratch_shapes=[pltpu.VMEM((mem_len, mem_w), BF16),
                        pltpu.VMEM((mem_w // LANES, mem_len, 2 * LANES), BF16)],
        compiler_params=pltpu.CompilerParams(
            dimension_semantics=("parallel", "arbitrary"),
            vmem_limit_bytes=V7X_VMEM_LIMIT_BYTES),
        name="mix_out",
    )(a2d, qm_src, mem, w_mkv, w_out, h2d, gain)


def _moba_kernel(slopes_ref, q_ref, k_ref, v_ref, o_ref, tmpl_ref, km_ref):
    seq = k_ref.shape[0]
    bs = MOBA_BLOCK
    nb = seq // bs
    n_groups = k_ref.shape[1] // LANES
    heads_per_group = LANES // MOBA_HEAD_DIM
    first_gated = MOBA_TOPK + 1
    p = pl.program_id(0)
    lane = lax.broadcasted_iota(jnp.int32, (1, LANES), 1)
    bias_lane0 = [nb + 3 * hh for hh in range(heads_per_group)]
    in_head = [(lane >= hh * MOBA_HEAD_DIM) & (lane < (hh + 1) * MOBA_HEAD_DIM)
               for hh in range(heads_per_group)]

    @pl.when(pl.program_id(1) == 0)
    def _():
        posi = lax.broadcasted_iota(jnp.int32, (seq, 1), 0)
        pos = posi.astype(F32)
        for gi in range(n_groups):
            t = jnp.where(lane == posi // bs, 1.0, 0.0)
            for hh in range(heads_per_group):
                head = heads_per_group * (n_groups * p + gi) + hh
                x = (slopes_ref[head] * LOG2_E) * pos
                hi = x.astype(BF16).astype(F32)
                mid = (x - hi).astype(BF16).astype(F32)
                lo = (x - hi - mid).astype(BF16).astype(F32)
                t = jnp.where(lane == bias_lane0[hh], hi, t)
                t = jnp.where(lane == bias_lane0[hh] + 1, mid, t)
                t = jnp.where(lane == bias_lane0[hh] + 2, lo, t)
            tmpl_ref[gi] = t.astype(BF16)

    rowi = lax.broadcasted_iota(jnp.int32, (bs, bs), 0)
    coli = lax.broadcasted_iota(jnp.int32, (bs, bs), 1)
    causal = coli <= rowi
    blk_id = lax.broadcasted_iota(jnp.int32, (nb, bs), 0)
    gated_rows = slice(first_gated * bs, seq)

    def prepare(gi):
        cols = slice(gi * LANES, (gi + 1) * LANES)
        k_all = k_ref[:, cols]
        for n in range(nb):
            kb = k_all[n * bs:(n + 1) * bs, :].astype(F32)
            km_ref[gi, n:n + 1, :] = jnp.sum(kb, axis=0, keepdims=True) * (1.0 / bs)
        km = km_ref[gi]
        km_hi = km.astype(BF16)
        km_lo = (km - km_hi.astype(F32)).astype(BF16)
        q_all = q_ref[:, cols]
        v_all = v_ref[:, cols]
        k_ext = jnp.concatenate([k_all, tmpl_ref[gi]], axis=-1)
        v_ext = jnp.concatenate([v_all, jnp.ones_like(v_all)], axis=-1)
        q_ext = []
        for hh in range(heads_per_group):
            qh = jnp.where(in_head[hh], q_all, jnp.zeros_like(q_all))
            gate_t = _dot_nt(km_hi, qh[gated_rows]) + _dot_nt(km_lo, qh[gated_rows])
            bias_t = []
            for j in range(first_gated, nb):
                g = gate_t[:, (j - first_gated) * bs:(j - first_gated + 1) * bs]
                beaten_by = jnp.zeros((nb, bs), F32)
                for mth in range(j):
                    gm = g[mth:mth + 1, :]
                    wins = (gm > g) | ((gm == g) & (blk_id > mth))
                    beaten_by = beaten_by + jnp.where(wins, 1.0, 0.0)
                keep = (beaten_by < float(MOBA_TOPK)) | (blk_id >= j)
                bias_t.append(jnp.where(keep, 0.0, NEG_INF))
            bias_t = jnp.concatenate(bias_t, axis=1)
            block_bias = jnp.concatenate(
                [bias_t, jnp.zeros((LANES - nb, bias_t.shape[1]), F32)], axis=0).T
            ones_lanes = jnp.where(
                (lane >= bias_lane0[hh]) & (lane < bias_lane0[hh] + 3), 1.0, 0.0)
            gated_aug = (block_bias + ones_lanes).astype(BF16)
            blocks = []
            for j in range(nb):
                aug = (gated_aug[(j - first_gated) * bs:(j - first_gated + 1) * bs]
                       if j >= first_gated
                       else jnp.broadcast_to(ones_lanes.astype(BF16), (bs, LANES)))
                blocks.append(jnp.concatenate([qh[j * bs:(j + 1) * bs], aug], axis=-1))
            q_ext.append(blocks)
        return k_ext, v_ext, q_ext

    def scores(group, j, hh):
        k_ext, _, q_ext = group
        rows = slice(j * bs, (j + 1) * bs)
        qa = q_ext[hh][j]
        s_own = jnp.where(causal, _dot_nt(qa, k_ext[rows]), NEG_INF)
        m_row = jnp.max(s_own, axis=-1, keepdims=True)
        s_past = None
        if j:
            s_past = _dot_nt(qa, k_ext[:j * bs])
            m_row = jnp.maximum(m_row, jnp.max(s_past, axis=-1, keepdims=True))
        return s_own, s_past, m_row

    def attend(group, j, s_own, s_past, m_row):
        v_ext = group[1]
        rows = slice(j * bs, (j + 1) * bs)
        acc = _dot(jnp.exp2(s_own - m_row).astype(BF16), v_ext[rows])
        if j:
            acc = acc + _dot(jnp.exp2(s_past - m_row).astype(BF16), v_ext[:j * bs])
        return acc[:, :LANES] / acc[:, LANES:]

    order = [j for pair in zip(range(nb - 1, -1, -1), range(nb)) for j in pair][:nb]
    work = [(gi, j, hh) for gi in range(n_groups) for j in order
            for hh in range(heads_per_group)]
    groups = {0: prepare(0)}
    head_out = {}
    pending = scores(groups[0], *work[0][1:])
    for i, (gi, j, hh) in enumerate(work):
        current = pending
        if i + 1 < len(work):
            nxt = work[i + 1]
            if nxt[0] not in groups:
                groups[nxt[0]] = prepare(nxt[0])
            pending = scores(groups[nxt[0]], *nxt[1:])
        head_out[hh] = attend(groups[gi], j, *current)
        if i == 0 and n_groups > 1:
            groups[1] = prepare(1)
        if hh == heads_per_group - 1:
            o_ref[j * bs:(j + 1) * bs, gi * LANES:(gi + 1) * LANES] = jnp.where(
                in_head[0], head_out[0], head_out[1]).astype(o_ref.dtype)


def _alibi_slopes(n):
    def pow2(m):
        return [2.0 ** (-8.0 * (i + 1) / m) for i in range(m)]
    p = 2 ** int(math.floor(math.log2(n)))
    s = pow2(p)
    if p < n:
        s = s + pow2(2 * p)[0::2][: n - p]
    return np.asarray(s, dtype=np.float32)


def _moba(qproj3, kv3):
    batch, seq, _ = qproj3.shape
    n_pairs = MOBA_HEADS * MOBA_HEAD_DIM // LANES
    nb = seq // MOBA_BLOCK
    slopes = jnp.asarray(_alibi_slopes(MOBA_HEADS))
    gps = MOBA_GROUPS_PER_STEP
    width = gps * LANES
    steps = n_pairs // gps
    grid_spec = pltpu.PrefetchScalarGridSpec(
        num_scalar_prefetch=1,
        grid=(steps, batch),
        in_specs=[pl.BlockSpec((None, seq, width), lambda p, b, *_: (b, 0, p)),
                  pl.BlockSpec((None, seq, width), lambda p, b, *_: (b, 0, p)),
                  pl.BlockSpec((None, seq, width), lambda p, b, *_: (b, 0, steps + p))],
        out_specs=pl.BlockSpec((None, seq, width), lambda p, b, *_: (b, 0, p)),
        scratch_shapes=[pltpu.VMEM((gps, seq, LANES), BF16), pltpu.VMEM((gps, nb, LANES), F32)],
    )
    return pl.pallas_call(
        _moba_kernel,
        grid_spec=grid_spec,
        out_shape=jax.ShapeDtypeStruct((batch, seq, n_pairs * LANES), BF16),
        compiler_params=pltpu.CompilerParams(
            dimension_semantics=("parallel", "arbitrary"),
            vmem_limit_bytes=V7X_VMEM_LIMIT_BYTES),
        name="moba",
    )(slopes, qproj3, kv3, kv3)


def kernel(x, mem, w_in_a, ret_norm_gain, w_out_a, kv_norm_gain, w_kv_shared, w_in_b, w_out_b,
           w_mem_kv, norm_pre_mix, norm_post_mix, norm_pre_mlp, norm_post_mlp, w_up, w_down):
    batch, seq, d = x.shape
    m = batch * seq
    depth = norm_pre_mix.shape[0]
    n_a = depth // 2
    qk_w = RET_HEADS * RET_QK_DIM
    v_w = RET_HEADS * RET_V_DIM
    mem_w = MEM_HEADS * MEM_HEAD_DIM
    moba_w = MOBA_HEADS * MOBA_HEAD_DIM
    row_vec = lambda g: g.reshape(1, -1)
    mem_q_scale = jnp.full((1, mem_w), MEM_HEAD_DIM ** -0.5 * LOG2_E, F32)
    moba_q_scale = jnp.full((1, moba_w), MOBA_HEAD_DIM ** -0.5 * LOG2_E, F32)
    scale_a = jnp.concatenate(
        [jnp.ones((1, 2 * qk_w + 2 * v_w), F32), mem_q_scale], axis=1)
    scale_b = jnp.concatenate([moba_q_scale, mem_q_scale], axis=1)
    scale_kv = jnp.ones((1, 2 * moba_w), F32)

    h = x.reshape(m, d)
    kv3 = None
    for l in range(depth):
        if l < n_a:
            y, q_m = _proj_retention(h, norm_pre_mix[l:l + 1], w_in_a[l].astype(BF16), scale_a,
                                     row_vec(ret_norm_gain[l]), batch, seq)
            h = _mix_out(y, q_m, 0, mem, w_mem_kv[l].astype(BF16), w_out_a[l].astype(BF16), h,
                         row_vec(norm_post_mix[l]), batch, seq)
        else:
            jb = l - n_a
            if l == n_a:
                qproj, kv = _norm_proj(
                    h, jnp.stack([norm_pre_mix[l], kv_norm_gain]),
                    [w_in_b[jb].astype(BF16), w_kv_shared.astype(BF16)], [scale_b, scale_kv])
                kv3 = kv.reshape(batch, seq, 2 * moba_w)
            else:
                (qproj,) = _norm_proj(h, norm_pre_mix[l:l + 1], [w_in_b[jb].astype(BF16)],
                                      [scale_b])
            a = _moba(qproj.reshape(batch, seq, moba_w + mem_w), kv3)
            h = _mix_out(a.reshape(m, moba_w), qproj, moba_w // mem_w, mem,
                         w_mem_kv[l].astype(BF16), w_out_b[jb].astype(BF16), h,
                         row_vec(norm_post_mix[l]), batch, seq)
        h = _mlp(h, row_vec(norm_pre_mlp[l]), row_vec(norm_post_mlp[l]),
                 w_up[l].astype(BF16), w_down[l].astype(BF16))
    return h.reshape(batch, seq, d)
```

```python
import functools
import math

import numpy as np
import jax
import jax.numpy as jnp
from jax import lax
from jax.experimental import pallas as pl
from jax.experimental.pallas import tpu as pltpu

F32 = jnp.float32
BF16 = jnp.bfloat16

EPS = 1e-6
NEG_INF = -1e30
LOG2_E = math.log2(math.e)

LANES = 128
V7X_VMEM_LIMIT_BYTES = 56 * 1024 * 1024

RET_HEADS = 4
RET_QK_DIM = 128
RET_V_DIM = 192
RET_CHUNK = 128
MOBA_HEADS = 12
MOBA_HEAD_DIM = 64
MOBA_BLOCK = 256
MOBA_TOPK = 3
MOBA_GROUPS_PER_STEP = 2
MEM_HEADS = 4
MEM_HEAD_DIM = 64

ROW_TILE = 1024
DENSE_SUB_ROWS = 512
RET_ROW_TILE = 1024
RET_SUB_ROWS = 256
PROJ_PIECE_COLS = 256
MIX_ROW_TILE = 1024
MIX_SUB_ROWS = 256
FF_CHUNK = 1024


def _rms_scale(x):
    return lax.rsqrt(jnp.mean(x * x, axis=-1, keepdims=True) + EPS)


def _dot(a, b):
    return jnp.dot(a, b, preferred_element_type=F32)


def _dot_nt(a, b):
    return lax.dot_general(a, b, (((1,), (1,)), ((), ())), preferred_element_type=F32)


def _dot_tn(a, b):
    return lax.dot_general(a, b, (((0,), (0,)), ((), ())), preferred_element_type=F32)


def _resident(shape):
    zeros = (0,) * len(shape)
    return pl.BlockSpec(shape, lambda *_: zeros, pipeline_mode=pl.Buffered(1))


def _norm_proj_kernel(x_ref, g_ref, *refs, n_out):
    w_refs, s_refs, o_refs = refs[:n_out], refs[n_out:2 * n_out], refs[2 * n_out:]
    sub = DENSE_SUB_ROWS
    n_sub = x_ref.shape[0] // sub

    def normed(r):
        x = x_ref[r * sub:(r + 1) * sub, :]
        return x * _rms_scale(x)

    xhat, xhat_next = normed(0), None
    for r in range(n_sub):
        rows = slice(r * sub, (r + 1) * sub)
        for i in range(n_out):
            hn = (xhat * g_ref[i:i + 1, :]).astype(BF16)
            out = _dot(hn, w_refs[i][...])
            if i == 0 and r + 1 < n_sub:
                xhat_next = normed(r + 1)
            o_refs[i][rows, :] = (out * s_refs[i][...]).astype(o_refs[i].dtype)
        xhat = xhat_next


def _norm_proj(x2d, gains, weights, col_scales):
    m, d = x2d.shape
    n_out = len(weights)
    return pl.pallas_call(
        functools.partial(_norm_proj_kernel, n_out=n_out),
        grid=(m // ROW_TILE,),
        in_specs=[pl.BlockSpec((ROW_TILE, d), lambda i: (i, 0)),
                  _resident(gains.shape)]
                 + [_resident(w.shape) for w in weights]
                 + [_resident(s.shape) for s in col_scales],
        out_specs=[pl.BlockSpec((ROW_TILE, w.shape[1]), lambda i: (i, 0)) for w in weights],
        out_shape=[jax.ShapeDtypeStruct((m, w.shape[1]), BF16) for w in weights],
        compiler_params=pltpu.CompilerParams(
            dimension_semantics=("parallel",), vmem_limit_bytes=V7X_VMEM_LIMIT_BYTES),
        name="norm_proj",
    )(x2d, gains, *weights, *col_scales)


def _mlp_kernel(h_ref, g_pre_ref, g_post_ref, wu_ref, wd_ref, o_ref):
    sub = DENSE_SUB_ROWS
    n_sub = h_ref.shape[0] // sub
    d_ff = wu_ref.shape[1]

    def normed(r):
        h = h_ref[r * sub:(r + 1) * sub, :]
        return (h * _rms_scale(h) * g_pre_ref[...]).astype(BF16)

    def epilogue(r, y):
        rows = slice(r * sub, (r + 1) * sub)
        o_ref[rows, :] = h_ref[rows, :] + y * _rms_scale(y) * g_post_ref[...]

    hn, hn_next, finished = normed(0), None, None
    for r in range(n_sub):
        y = None
        for c in range(d_ff // FF_CHUNK):
            cols = slice(c * FF_CHUNK, (c + 1) * FF_CHUNK)
            u = _dot(hn, wu_ref[:, cols])
            if c == 0:
                if r + 1 < n_sub:
                    hn_next = normed(r + 1)
                if finished is not None:
                    epilogue(*finished)
            a = jnp.square(jnp.maximum(u, 0.0)).astype(BF16)
            d = _dot(a, wd_ref[cols, :])
            y = d if y is None else y + d
        finished, hn = (r, y), hn_next
    epilogue(*finished)


def _mlp(h2d, g_pre, g_post, w_up, w_down):
    m, d = h2d.shape
    row_spec = pl.BlockSpec((ROW_TILE, d), lambda i: (i, 0))
    return pl.pallas_call(
        _mlp_kernel,
        grid=(m // ROW_TILE,),
        in_specs=[row_spec, _resident(g_pre.shape), _resident(g_post.shape),
                  _resident(w_up.shape), _resident(w_down.shape)],
        out_specs=row_spec,
        out_shape=jax.ShapeDtypeStruct((m, d), F32),
        compiler_params=pltpu.CompilerParams(
            dimension_semantics=("parallel",), vmem_limit_bytes=V7X_VMEM_LIMIT_BYTES),
        name="mlp",
    )(h2d, g_pre, g_post, w_up, w_down)


def _proj_retention_kernel(x_ref, g_pre_ref, w_ref, scale_ref, gain_ref, y_ref, qm_ref,
                           state_ref):
    c = RET_CHUNK
    sub = RET_SUB_ROWS
    qk_w = RET_HEADS * RET_QK_DIM
    v_w = RET_HEADS * RET_V_DIM

    @pl.when(pl.program_id(1) == 0)
    def _():
        state_ref[...] = jnp.zeros(state_ref.shape, F32)

    row = lax.broadcasted_iota(jnp.int32, (c, c), 0)
    col = lax.broadcasted_iota(jnp.int32, (c, c), 1)
    rel = (row - col).astype(F32)
    pos = lax.broadcasted_iota(jnp.int32, (c, 1), 0).astype(F32)
    qk_scale = RET_QK_DIM ** -0.5

    tables = []
    for h in range(RET_HEADS):
        log_g = float(np.log1p(-np.exp2(np.float32(-5.0 - h))))
        d_intra = jnp.where(rel >= 0, jnp.exp(log_g * jnp.maximum(rel, 0.0)), 0.0) * qk_scale
        zeta = jnp.exp(log_g * (c - 1.0 - pos)) * qk_scale
        xi = jnp.exp(log_g * (pos + 1.0))
        chunk_decay = float(np.exp(np.float32(log_g * c)))
        tables.append((d_intra, zeta, xi, chunk_decay))

    n_pieces = w_ref.shape[1] // PROJ_PIECE_COLS

    def project_pieces(r):
        x = x_ref[r * sub:(r + 1) * sub, :]
        hn = (x * _rms_scale(x) * g_pre_ref[...]).astype(BF16)

        def piece(n):
            cols = slice(n * PROJ_PIECE_COLS, (n + 1) * PROJ_PIECE_COLS)
            return (_dot(hn, w_ref[:, cols]) * scale_ref[:, cols]).astype(BF16)
        return [functools.partial(piece, n) for n in range(n_pieces)]

    def retain(r, proj, next_pieces):
        qm_ref[r * sub:(r + 1) * sub, :] = proj[:, 2 * qk_w + 2 * v_w:]
        bodies = [(ci, h) for ci in range(sub // c) for h in range(RET_HEADS)]
        done = []

        def qk_scores(ci, h):
            rows = slice(ci * c, (ci + 1) * c)
            q = proj[rows, h * RET_QK_DIM:(h + 1) * RET_QK_DIM]
            k = proj[rows, qk_w + h * RET_QK_DIM:qk_w + (h + 1) * RET_QK_DIM]
            return q, k, _dot_nt(q, k)

        pending = qk_scores(*bodies[0])
        for i, (ci, h) in enumerate(bodies):
            while len(done) < (i + 1) * len(next_pieces) // len(bodies):
                done.append(next_pieces[len(done)]())
            q, k, qk = pending
            if i + 1 < len(bodies):
                pending = qk_scores(*bodies[i + 1])
            rows = slice(ci * c, (ci + 1) * c)
            out_rows = slice(r * sub + ci * c, r * sub + (ci + 1) * c)
            d_intra, zeta, xi, chunk_decay = tables[h]
            v_cols = slice(h * RET_V_DIM, (h + 1) * RET_V_DIM)
            v = proj[rows, 2 * qk_w + h * RET_V_DIM:2 * qk_w + (h + 1) * RET_V_DIM]
            g = proj[rows, 2 * qk_w + v_w + h * RET_V_DIM:
                     2 * qk_w + v_w + (h + 1) * RET_V_DIM].astype(F32)
            state = state_ref[h]
            s = qk * d_intra
            y = _dot(s.astype(BF16), v) + _dot(q, state.astype(BF16)) * xi
            kz = (k.astype(F32) * zeta).astype(BF16)
            state_ref[h] = state * chunk_decay + _dot_tn(kz, v)
            ms = jnp.mean(y * y, axis=-1, keepdims=True)
            yn = y * lax.rsqrt(ms + EPS) * gain_ref[:, v_cols]
            y_ref[out_rows, v_cols] = (g / (1.0 + jnp.exp2(g * -LOG2_E)) * yn).astype(
                y_ref.dtype)
        return done

    n_groups = x_ref.shape[0] // sub
    proj = jnp.concatenate([piece() for piece in project_pieces(0)], axis=1)
    for r in range(n_groups):
        next_pieces = project_pieces(r + 1) if r + 1 < n_groups else []
        done = retain(r, proj, next_pieces)
        if done:
            proj = jnp.concatenate(done, axis=1)


def _proj_retention(x2d, g_pre, w_in, col_scale, gn_gain, batch, seq):
    m, d = x2d.shape
    steps = seq // RET_ROW_TILE
    v_w = RET_HEADS * RET_V_DIM
    mem_w = MEM_HEADS * MEM_HEAD_DIM
    row_block = lambda width: pl.BlockSpec((RET_ROW_TILE, width), lambda b, s: (b * steps + s, 0))
    return pl.pallas_call(
        _proj_retention_kernel,
        grid=(batch, steps),
        in_specs=[row_block(d), _resident(g_pre.shape), _resident(w_in.shape),
                  _resident(col_scale.shape), _resident(gn_gain.shape)],
        out_specs=[row_block(v_w), row_block(mem_w)],
        out_shape=[jax.ShapeDtypeStruct((m, v_w), BF16), jax.ShapeDtypeStruct((m, mem_w), BF16)],
        scratch_shapes=[pltpu.VMEM((RET_HEADS, RET_QK_DIM, RET_V_DIM), F32)],
        compiler_params=pltpu.CompilerParams(
            dimension_semantics=("parallel", "arbitrary"),
            vmem_limit_bytes=V7X_VMEM_LIMIT_BYTES),
        name="proj_retention",
    )(x2d, g_pre, w_in, col_scale, gn_gain)


def _mix_out_kernel(a_ref, qm_ref, mem_ref, wmkv_ref, wout_ref, h_ref, gain_ref, o_ref,
                    mk_ref, mv_ref):
    mem_w = MEM_HEADS * MEM_HEAD_DIM
    groups = mem_w // LANES
    heads_per_group = LANES // MEM_HEAD_DIM
    lane = lax.broadcasted_iota(jnp.int32, (1, LANES), 1)
    head_lanes = [(lane >= hh * MEM_HEAD_DIM) & (lane < (hh + 1) * MEM_HEAD_DIM)
                  for hh in range(heads_per_group)]

    @pl.when(pl.program_id(1) == 0)
    def _():
        mkv = _dot(mem_ref[...].astype(BF16), wmkv_ref[...])
        mk_ref[...] = mkv[:, :mem_w].astype(BF16)
        for p in range(groups):
            mvp = mkv[:, mem_w + p * LANES:mem_w + (p + 1) * LANES].astype(BF16)
            mv_ref[p] = jnp.concatenate([mvp, jnp.ones_like(mvp)], axis=-1)

    a_w = a_ref.shape[1]
    sub = MIX_SUB_ROWS

    n_sub = a_ref.shape[0] // sub

    def scores(p, hh):
        cols = slice(p * LANES, (p + 1) * LANES)
        qp = qm_ref[:, cols]
        qh = jnp.where(head_lanes[hh], qp, jnp.zeros_like(qp))
        s = _dot_nt(qh, mk_ref[:, cols])
        return s, jnp.max(s, axis=-1, keepdims=True)

    def attend(p, hh, s, m_row):
        acc = _dot(jnp.exp2(s - m_row).astype(BF16), mv_ref[p])
        return acc[:, :LANES] / acc[:, LANES:]

    def project_a(r):
        return _dot(a_ref[r * sub:(r + 1) * sub, :], wout_ref[:a_w, :])

    def epilogue(r, mix):
        rows = slice(r * sub, (r + 1) * sub)
        o_ref[rows, :] = h_ref[rows, :] + mix * _rms_scale(mix) * gain_ref[...]

    work = [(p, hh) for p in range(groups) for hh in range(heads_per_group)]
    outs, mix_a = {}, {}
    pending = scores(*work[0])
    for i, (p, hh) in enumerate(work):
        current = pending
        if i + 1 < len(work):
            pending = scores(*work[i + 1])
        if i < n_sub // 2:
            mix_a[i] = project_a(i)
        outs[p, hh] = attend(p, hh, *current)
    mem_out = jnp.concatenate(
        [jnp.where(head_lanes[0], outs[p, 0], outs[p, 1]).astype(BF16) for p in range(groups)],
        axis=-1)
    projected = []
    for r in range(n_sub):
        if r not in mix_a:
            mix_a[r] = project_a(r)
            while projected:
                epilogue(*projected.pop(0))
        mix = mix_a.pop(r) + _dot(mem_out[r * sub:(r + 1) * sub], wout_ref[a_w:, :])
        projected.append((r, mix))
    while projected:
        epilogue(*projected.pop(0))


def _mix_out(a2d, qm_src, qm_col_block, mem, w_mkv, w_out, h2d, gain, batch, seq):
    m, d = h2d.shape
    a_w = a2d.shape[1]
    mem_w = MEM_HEADS * MEM_HEAD_DIM
    mem_len = mem.shape[1]
    steps = seq // MIX_ROW_TILE
    row = lambda b, s: b * steps + s
    return pl.pallas_call(
        _mix_out_kernel,
        grid=(batch, steps),
        in_specs=[pl.BlockSpec((MIX_ROW_TILE, a_w), lambda b, s: (row(b, s), 0)),
                  pl.BlockSpec((MIX_ROW_TILE, mem_w), lambda b, s: (row(b, s), qm_col_block)),
                  pl.BlockSpec((None, mem_len, d), lambda b, s: (b, 0, 0)),
                  _resident(w_mkv.shape), _resident(w_out.shape),
                  pl.BlockSpec((MIX_ROW_TILE, d), lambda b, s: (row(b, s), 0)),
                  _resident(gain.shape)],
        out_specs=pl.BlockSpec((MIX_ROW_TILE, d), lambda b, s: (row(b, s), 0)),
        out_shape=jax.ShapeDtypeStruct((m, d), F32),
        scratch_shapes=[pltpu.VMEM((mem_len, mem_w), BF16),
                        pltpu.VMEM((mem_w // LANES, mem_len, 2 * LANES), BF16)],
        compiler_params=pltpu.CompilerParams(
            dimension_semantics=("parallel", "arbitrary"),
            vmem_limit_bytes=V7X_VMEM_LIMIT_BYTES),
        name="mix_out",
    )(a2d, qm_src, mem, w_mkv, w_out, h2d, gain)


def _moba_kernel(slopes_ref, q_ref, k_ref, v_ref, o_ref, tmpl_ref, km_ref):
    seq = k_ref.shape[0]
    bs = MOBA_BLOCK
    nb = seq // bs
    n_groups = k_ref.shape[1] // LANES
    heads_per_group = LANES // MOBA_HEAD_DIM
    first_gated = MOBA_TOPK + 1
    p = pl.program_id(0)
    lane = lax.broadcasted_iota(jnp.int32, (1, LANES), 1)
    bias_lane0 = [nb + 3 * hh for hh in range(heads_per_group)]
    in_head = [(lane >= hh * MOBA_HEAD_DIM) & (lane < (hh + 1) * MOBA_HEAD_DIM)
               for hh in range(heads_per_group)]

    @pl.when(pl.program_id(1) == 0)
    def _():
        posi = lax.broadcasted_iota(jnp.int32, (seq, 1), 0)
        pos = posi.astype(F32)
        for gi in range(n_groups):
            t = jnp.where(lane == posi // bs, 1.0, 0.0)
            for hh in range(heads_per_group):
                head = heads_per_group * (n_groups * p + gi) + hh
                x = (slopes_ref[head] * LOG2_E) * pos
                hi = x.astype(BF16).astype(F32)
                mid = (x - hi).astype(BF16).astype(F32)
                lo = (x - hi - mid).astype(BF16).astype(F32)
                t = jnp.where(lane == bias_lane0[hh], hi, t)
                t = jnp.where(lane == bias_lane0[hh] + 1, mid, t)
                t = jnp.where(lane == bias_lane0[hh] + 2, lo, t)
            tmpl_ref[gi] = t.astype(BF16)

    rowi = lax.broadcasted_iota(jnp.int32, (bs, bs), 0)
    coli = lax.broadcasted_iota(jnp.int32, (bs, bs), 1)
    causal = coli <= rowi
    blk_id = lax.broadcasted_iota(jnp.int32, (nb, bs), 0)
    gated_rows = slice(first_gated * bs, seq)

    def prepare(gi):
        cols = slice(gi * LANES, (gi + 1) * LANES)
        k_all = k_ref[:, cols]
        for n in range(nb):
            kb = k_all[n * bs:(n + 1) * bs, :].astype(F32)
            km_ref[gi, n:n + 1, :] = jnp.sum(kb, axis=0, keepdims=True) * (1.0 / bs)
        km = km_ref[gi]
        km_hi = km.astype(BF16)
        km_lo = (km - km_hi.astype(F32)).astype(BF16)
        q_all = q_ref[:, cols]
        v_all = v_ref[:, cols]
        k_ext = jnp.concatenate([k_all, tmpl_ref[gi]], axis=-1)
        v_ext = jnp.concatenate([v_all, jnp.ones_like(v_all)], axis=-1)
        q_ext = []
        for hh in range(heads_per_group):
            qh = jnp.where(in_head[hh], q_all, jnp.zeros_like(q_all))
            gate_t = _dot_nt(km_hi, qh[gated_rows]) + _dot_nt(km_lo, qh[gated_rows])
            bias_t = []
            for j in range(first_gated, nb):
                g = gate_t[:, (j - first_gated) * bs:(j - first_gated + 1) * bs]
                beaten_by = jnp.zeros((nb, bs), F32)
                for mth in range(j):
                    gm = g[mth:mth + 1, :]
                    wins = (gm > g) | ((gm == g) & (blk_id > mth))
                    beaten_by = beaten_by + jnp.where(wins, 1.0, 0.0)
                keep = (beaten_by < float(MOBA_TOPK)) | (blk_id >= j)
                bias_t.append(jnp.where(keep, 0.0, NEG_INF))
            bias_t = jnp.concatenate(bias_t, axis=1)
            block_bias = jnp.concatenate(
                [bias_t, jnp.zeros((LANES - nb, bias_t.shape[1]), F32)], axis=0).T
            ones_lanes = jnp.where(
                (lane >= bias_lane0[hh]) & (lane < bias_lane0[hh] + 3), 1.0, 0.0)
            gated_aug = (block_bias + ones_lanes).astype(BF16)
            blocks = []
            for j in range(nb):
                aug = (gated_aug[(j - first_gated) * bs:(j - first_gated + 1) * bs]
                       if j >= first_gated
                       else jnp.broadcast_to(ones_lanes.astype(BF16), (bs, LANES)))
                blocks.append(jnp.concatenate([qh[j * bs:(j + 1) * bs], aug], axis=-1))
            q_ext.append(blocks)
        return k_ext, v_ext, q_ext

    def scores(group, j, hh):
        k_ext, _, q_ext = group
        rows = slice(j * bs, (j + 1) * bs)
        qa = q_ext[hh][j]
        s_own = jnp.where(causal, _dot_nt(qa, k_ext[rows]), NEG_INF)
        m_row = jnp.max(s_own, axis=-1, keepdims=True)
        s_past = None
        if j:
            s_past = _dot_nt(qa, k_ext[:j * bs])
            m_row = jnp.maximum(m_row, jnp.max(s_past, axis=-1, keepdims=True))
        return s_own, s_past, m_row

    def attend(group, j, s_own, s_past, m_row):
        v_ext = group[1]
        rows = slice(j * bs, (j + 1) * bs)
        acc = _dot(jnp.exp2(s_own - m_row).astype(BF16), v_ext[rows])
        if j:
            acc = acc + _dot(jnp.exp2(s_past - m_row).astype(BF16), v_ext[:j * bs])
        return acc[:, :LANES] / acc[:, LANES:]

    order = [j for pair in zip(range(nb - 1, -1, -1), range(nb)) for j in pair][:nb]
    work = [(gi, j, hh) for gi in range(n_groups) for j in order
            for hh in range(heads_per_group)]
    groups = {0: prepare(0)}
    head_out = {}
    pending = scores(groups[0], *work[0][1:])
    for i, (gi, j, hh) in enumerate(work):
        current = pending
        if i + 1 < len(work):
            nxt = work[i + 1]
            if nxt[0] not in groups:
                groups[nxt[0]] = prepare(nxt[0])
            pending = scores(groups[nxt[0]], *nxt[1:])
        head_out[hh] = attend(groups[gi], j, *current)
        if i == 0 and n_groups > 1:
            groups[1] = prepare(1)
        if hh == heads_per_group - 1:
            o_ref[j * bs:(j + 1) * bs, gi * LANES:(gi + 1) * LANES] = jnp.where(
                in_head[0], head_out[0], head_out[1]).astype(o_ref.dtype)


def _alibi_slopes(n):
    def pow2(m):
        return [2.0 ** (-8.0 * (i + 1) / m) for i in range(m)]
    p = 2 ** int(math.floor(math.log2(n)))
    s = pow2(p)
    if p < n:
        s = s + pow2(2 * p)[0::2][: n - p]
    return np.asarray(s, dtype=np.float32)


def _moba(qproj3, kv3):
    batch, seq, _ = qproj3.shape
    n_pairs = MOBA_HEADS * MOBA_HEAD_DIM // LANES
    nb = seq // MOBA_BLOCK
    slopes = jnp.asarray(_alibi_slopes(MOBA_HEADS))
    gps = MOBA_GROUPS_PER_STEP
    width = gps * LANES
    steps = n_pairs // gps
    grid_spec = pltpu.PrefetchScalarGridSpec(
        num_scalar_prefetch=1,
        grid=(steps, batch),
        in_specs=[pl.BlockSpec((None, seq, width), lambda p, b, *_: (b, 0, p)),
                  pl.BlockSpec((None, seq, width), lambda p, b, *_: (b, 0, p)),
                  pl.BlockSpec((None, seq, width), lambda p, b, *_: (b, 0, steps + p))],
        out_specs=pl.BlockSpec((None, seq, width), lambda p, b, *_: (b, 0, p)),
        scratch_shapes=[pltpu.VMEM((gps, seq, LANES), BF16), pltpu.VMEM((gps, nb, LANES), F32)],
    )
    return pl.pallas_call(
        _moba_kernel,
        grid_spec=grid_spec,
        out_shape=jax.ShapeDtypeStruct((batch, seq, n_pairs * LANES), BF16),
        compiler_params=pltpu.CompilerParams(
            dimension_semantics=("parallel", "arbitrary"),
            vmem_limit_bytes=V7X_VMEM_LIMIT_BYTES),
        name="moba",
    )(slopes, qproj3, kv3, kv3)


def kernel(x, mem, w_in_a, ret_norm_gain, w_out_a, kv_norm_gain, w_kv_shared, w_in_b, w_out_b,
           w_mem_kv, norm_pre_mix, norm_post_mix, norm_pre_mlp, norm_post_mlp, w_up, w_down):
    batch, seq, d = x.shape
    m = batch * seq
    depth = norm_pre_mix.shape[0]
    n_a = depth // 2
    qk_w = RET_HEADS * RET_QK_DIM
    v_w = RET_HEADS * RET_V_DIM
    mem_w = MEM_HEADS * MEM_HEAD_DIM
    moba_w = MOBA_HEADS * MOBA_HEAD_DIM
    row_vec = lambda g: g.reshape(1, -1)
    mem_q_scale = jnp.full((1, mem_w), MEM_HEAD_DIM ** -0.5 * LOG2_E, F32)
    moba_q_scale = jnp.full((1, moba_w), MOBA_HEAD_DIM ** -0.5 * LOG2_E, F32)
    scale_a = jnp.concatenate(
        [jnp.ones((1, 2 * qk_w + 2 * v_w), F32), mem_q_scale], axis=1)
    scale_b = jnp.concatenate([moba_q_scale, mem_q_scale], axis=1)
    scale_kv = jnp.ones((1, 2 * moba_w), F32)

    h = x.reshape(m, d)
    kv3 = None
    for l in range(depth):
        if l < n_a:
            y, q_m = _proj_retention(h, norm_pre_mix[l:l + 1], w_in_a[l].astype(BF16), scale_a,
                                     row_vec(ret_norm_gain[l]), batch, seq)
            h = _mix_out(y, q_m, 0, mem, w_mem_kv[l].astype(BF16), w_out_a[l].astype(BF16), h,
                         row_vec(norm_post_mix[l]), batch, seq)
        else:
            jb = l - n_a
            if l == n_a:
                qproj, kv = _norm_proj(
                    h, jnp.stack([norm_pre_mix[l], kv_norm_gain]),
                    [w_in_b[jb].astype(BF16), w_kv_shared.astype(BF16)], [scale_b, scale_kv])
                kv3 = kv.reshape(batch, seq, 2 * moba_w)
            else:
                (qproj,) = _norm_proj(h, norm_pre_mix[l:l + 1], [w_in_b[jb].astype(BF16)],
                                      [scale_b])
            a = _moba(qproj.reshape(batch, seq, moba_w + mem_w), kv3)
            h = _mix_out(a.reshape(m, moba_w), qproj, moba_w // mem_w, mem,
                         w_mem_kv[l].astype(BF16), w_out_b[jb].astype(BF16), h,
                         row_vec(norm_post_mix[l]), batch, seq)
        h = _mlp(h, row_vec(norm_pre_mlp[l]), row_vec(norm_post_mlp[l]),
                 w_up[l].astype(BF16), w_down[l].astype(BF16))
    return h.reshape(batch, seq, d)
```

```python
import functools
import math

import numpy as np
import jax
import jax.numpy as jnp
from jax import lax
from jax.experimental import pallas as pl
from jax.experimental.pallas import tpu as pltpu

F32 = jnp.float32
BF16 = jnp.bfloat16

EPS = 1e-6
NEG_INF = -1e30
LOG2_E = math.log2(math.e)

LANES = 128
V7X_VMEM_LIMIT_BYTES = 56 * 1024 * 1024

RET_HEADS = 4
RET_QK_DIM = 128
RET_V_DIM = 192
RET_CHUNK = 128
MOBA_HEADS = 12
MOBA_HEAD_DIM = 64
MOBA_BLOCK = 256
MOBA_TOPK = 3
MOBA_GROUPS_PER_STEP = 2
MEM_HEADS = 4
MEM_HEAD_DIM = 64

ROW_TILE = 1024
DENSE_SUB_ROWS = 512
RET_ROW_TILE = 1024
RET_SUB_ROWS = 256
PROJ_PIECE_COLS = 256
MIX_ROW_TILE = 1024
MIX_SUB_ROWS = 256
FF_CHUNK = 1024


def _rms_scale(x):
    return lax.rsqrt(jnp.mean(x * x, axis=-1, keepdims=True) + EPS)


def _dot(a, b):
    return jnp.dot(a, b, preferred_element_type=F32)


def _dot_nt(a, b):
    return lax.dot_general(a, b, (((1,), (1,)), ((), ())), preferred_element_type=F32)


def _dot_tn(a, b):
    return lax.dot_general(a, b, (((0,), (0,)), ((), ())), preferred_element_type=F32)


def _resident(shape):
    zeros = (0,) * len(shape)
    return pl.BlockSpec(shape, lambda *_: zeros, pipeline_mode=pl.Buffered(1))


def _cast_slab_specs(weights, n_steps, linear_step):
    return [pl.BlockSpec((w.shape[0] // n_steps, w.shape[1]),
                         lambda *idx: (linear_step(*idx), 0)) for w in weights]


def _cast_slabs(src_refs, dst_refs):
    for src, dst in zip(src_refs, dst_refs):
        dst[...] = src[...].astype(dst.dtype)


def _norm_proj_kernel(x_ref, g_ref, *refs, n_out):
    w_refs, s_refs, o_refs = refs[:n_out], refs[n_out:2 * n_out], refs[2 * n_out:]
    sub = DENSE_SUB_ROWS
    n_sub = x_ref.shape[0] // sub

    def normed(r):
        x = x_ref[r * sub:(r + 1) * sub, :]
        return x * _rms_scale(x)

    xhat, xhat_next = normed(0), None
    for r in range(n_sub):
        rows = slice(r * sub, (r + 1) * sub)
        for i in range(n_out):
            hn = (xhat * g_ref[i:i + 1, :]).astype(BF16)
            out = _dot(hn, w_refs[i][...])
            if i == 0 and r + 1 < n_sub:
                xhat_next = normed(r + 1)
            o_refs[i][rows, :] = (out * s_refs[i][...]).astype(o_refs[i].dtype)
        xhat = xhat_next


def _norm_proj(x2d, gains, weights, col_scales):
    m, d = x2d.shape
    n_out = len(weights)
    return pl.pallas_call(
        functools.partial(_norm_proj_kernel, n_out=n_out),
        grid=(m // ROW_TILE,),
        in_specs=[pl.BlockSpec((ROW_TILE, d), lambda i: (i, 0)),
                  _resident(gains.shape)]
                 + [_resident(w.shape) for w in weights]
                 + [_resident(s.shape) for s in col_scales],
        out_specs=[pl.BlockSpec((ROW_TILE, w.shape[1]), lambda i: (i, 0)) for w in weights],
        out_shape=[jax.ShapeDtypeStruct((m, w.shape[1]), BF16) for w in weights],
        compiler_params=pltpu.CompilerParams(
            dimension_semantics=("parallel",), vmem_limit_bytes=V7X_VMEM_LIMIT_BYTES),
        name="norm_proj",
    )(x2d, gains, *weights, *col_scales)


def _mlp_kernel(h_ref, g_pre_ref, g_post_ref, wu_ref, wd_ref, *refs, n_cast):
    o_ref = refs[n_cast]
    _cast_slabs(refs[:n_cast], refs[n_cast + 1:])
    sub = DENSE_SUB_ROWS
    n_sub = h_ref.shape[0] // sub
    d_ff = wu_ref.shape[1]

    def normed(r):
        h = h_ref[r * sub:(r + 1) * sub, :]
        return (h * _rms_scale(h) * g_pre_ref[...]).astype(BF16)

    def epilogue(r, y):
        rows = slice(r * sub, (r + 1) * sub)
        o_ref[rows, :] = h_ref[rows, :] + y * _rms_scale(y) * g_post_ref[...]

    hn, hn_next, finished = normed(0), None, None
    for r in range(n_sub):
        y = None
        for c in range(d_ff // FF_CHUNK):
            cols = slice(c * FF_CHUNK, (c + 1) * FF_CHUNK)
            u = _dot(hn, wu_ref[:, cols])
            if c == 0:
                if r + 1 < n_sub:
                    hn_next = normed(r + 1)
                if finished is not None:
                    epilogue(*finished)
            a = jnp.square(jnp.maximum(u, 0.0)).astype(BF16)
            d = _dot(a, wd_ref[cols, :])
            y = d if y is None else y + d
        finished, hn = (r, y), hn_next
    epilogue(*finished)


def _mlp(h2d, g_pre, g_post, w_up, w_down, casts):
    m, d = h2d.shape
    n_steps = m // ROW_TILE
    row_spec = pl.BlockSpec((ROW_TILE, d), lambda i: (i, 0))
    cast_specs = _cast_slab_specs(casts, n_steps, lambda i: i)
    outs = pl.pallas_call(
        functools.partial(_mlp_kernel, n_cast=len(casts)),
        grid=(n_steps,),
        in_specs=[row_spec, _resident(g_pre.shape), _resident(g_post.shape),
                  _resident(w_up.shape), _resident(w_down.shape)] + cast_specs,
        out_specs=[row_spec] + cast_specs,
        out_shape=[jax.ShapeDtypeStruct((m, d), F32)]
                  + [jax.ShapeDtypeStruct(w.shape, BF16) for w in casts],
        compiler_params=pltpu.CompilerParams(
            dimension_semantics=("parallel",), vmem_limit_bytes=V7X_VMEM_LIMIT_BYTES),
        name="mlp",
    )(h2d, g_pre, g_post, w_up, w_down, *casts)
    return outs[0], outs[1:]


def _proj_retention_kernel(x_ref, g_pre_ref, w_ref, scale_ref, gain_ref, *refs, n_cast):
    y_ref, qm_ref = refs[n_cast:n_cast + 2]
    state_ref = refs[-1]
    _cast_slabs(refs[:n_cast], refs[n_cast + 2:-1])
    c = RET_CHUNK
    sub = RET_SUB_ROWS
    qk_w = RET_HEADS * RET_QK_DIM
    v_w = RET_HEADS * RET_V_DIM

    @pl.when(pl.program_id(1) == 0)
    def _():
        state_ref[...] = jnp.zeros(state_ref.shape, F32)

    row = lax.broadcasted_iota(jnp.int32, (c, c), 0)
    col = lax.broadcasted_iota(jnp.int32, (c, c), 1)
    rel = (row - col).astype(F32)
    pos = lax.broadcasted_iota(jnp.int32, (c, 1), 0).astype(F32)
    qk_scale = RET_QK_DIM ** -0.5

    tables = []
    for h in range(RET_HEADS):
        log_g = float(np.log1p(-np.exp2(np.float32(-5.0 - h))))
        d_intra = jnp.where(rel >= 0, jnp.exp(log_g * jnp.maximum(rel, 0.0)), 0.0) * qk_scale
        zeta = jnp.exp(log_g * (c - 1.0 - pos)) * qk_scale
        xi = jnp.exp(log_g * (pos + 1.0))
        chunk_decay = float(np.exp(np.float32(log_g * c)))
        tables.append((d_intra, zeta, xi, chunk_decay))

    n_pieces = w_ref.shape[1] // PROJ_PIECE_COLS

    def project_pieces(r):
        x = x_ref[r * sub:(r + 1) * sub, :]
        hn = (x * _rms_scale(x) * g_pre_ref[...]).astype(BF16)

        def piece(n):
            cols = slice(n * PROJ_PIECE_COLS, (n + 1) * PROJ_PIECE_COLS)
            return (_dot(hn, w_ref[:, cols]) * scale_ref[:, cols]).astype(BF16)
        return [functools.partial(piece, n) for n in range(n_pieces)]

    def retain(r, proj, next_pieces):
        qm_ref[r * sub:(r + 1) * sub, :] = proj[:, 2 * qk_w + 2 * v_w:]
        bodies = [(ci, h) for ci in range(sub // c) for h in range(RET_HEADS)]
        done = []

        def qk_scores(ci, h):
            rows = slice(ci * c, (ci + 1) * c)
            q = proj[rows, h * RET_QK_DIM:(h + 1) * RET_QK_DIM]
            k = proj[rows, qk_w + h * RET_QK_DIM:qk_w + (h + 1) * RET_QK_DIM]
            return q, k, _dot_nt(q, k)

        pending = qk_scores(*bodies[0])
        for i, (ci, h) in enumerate(bodies):
            while len(done) < (i + 1) * len(next_pieces) // len(bodies):
                done.append(next_pieces[len(done)]())
            q, k, qk = pending
            if i + 1 < len(bodies):
                pending = qk_scores(*bodies[i + 1])
            rows = slice(ci * c, (ci + 1) * c)
            out_rows = slice(r * sub + ci * c, r * sub + (ci + 1) * c)
            d_intra, zeta, xi, chunk_decay = tables[h]
            v_cols = slice(h * RET_V_DIM, (h + 1) * RET_V_DIM)
            v = proj[rows, 2 * qk_w + h * RET_V_DIM:2 * qk_w + (h + 1) * RET_V_DIM]
            g = proj[rows, 2 * qk_w + v_w + h * RET_V_DIM:
                     2 * qk_w + v_w + (h + 1) * RET_V_DIM].astype(F32)
            state = state_ref[h]
            s = qk * d_intra
            y = _dot(s.astype(BF16), v) + _dot(q, state.astype(BF16)) * xi
            kz = (k.astype(F32) * zeta).astype(BF16)
            state_ref[h] = state * chunk_decay + _dot_tn(kz, v)
            ms = jnp.mean(y * y, axis=-1, keepdims=True)
            yn = y * lax.rsqrt(ms + EPS) * gain_ref[:, v_cols]
            y_ref[out_rows, v_cols] = (g / (1.0 + jnp.exp2(g * -LOG2_E)) * yn).astype(
                y_ref.dtype)
        return done

    n_groups = x_ref.shape[0] // sub
    proj = jnp.concatenate([piece() for piece in project_pieces(0)], axis=1)
    for r in range(n_groups):
        next_pieces = project_pieces(r + 1) if r + 1 < n_groups else []
        done = retain(r, proj, next_pieces)
        if done:
            proj = jnp.concatenate(done, axis=1)


def _proj_retention(x2d, g_pre, w_in, col_scale, gn_gain, batch, seq, casts):
    m, d = x2d.shape
    steps = seq // RET_ROW_TILE
    v_w = RET_HEADS * RET_V_DIM
    mem_w = MEM_HEADS * MEM_HEAD_DIM
    linear_step = lambda b, s: b * steps + s
    row_block = lambda width: pl.BlockSpec((RET_ROW_TILE, width),
                                           lambda b, s: (linear_step(b, s), 0))
    cast_specs = _cast_slab_specs(casts, batch * steps, linear_step)
    outs = pl.pallas_call(
        functools.partial(_proj_retention_kernel, n_cast=len(casts)),
        grid=(batch, steps),
        in_specs=[row_block(d), _resident(g_pre.shape), _resident(w_in.shape),
                  _resident(col_scale.shape), _resident(gn_gain.shape)] + cast_specs,
        out_specs=[row_block(v_w), row_block(mem_w)] + cast_specs,
        out_shape=[jax.ShapeDtypeStruct((m, v_w), BF16), jax.ShapeDtypeStruct((m, mem_w), BF16)]
                  + [jax.ShapeDtypeStruct(w.shape, BF16) for w in casts],
        scratch_shapes=[pltpu.VMEM((RET_HEADS, RET_QK_DIM, RET_V_DIM), F32)],
        compiler_params=pltpu.CompilerParams(
            dimension_semantics=("parallel", "arbitrary"),
            vmem_limit_bytes=V7X_VMEM_LIMIT_BYTES),
        name="proj_retention",
    )(x2d, g_pre, w_in, col_scale, gn_gain, *casts)
    return outs[0], outs[1], outs[2:]


def _mix_out_kernel(a_ref, qm_ref, mem_ref, wmkv_ref, wout_ref, h_ref, gain_ref, o_ref,
                    mk_ref, mv_ref):
    mem_w = MEM_HEADS * MEM_HEAD_DIM
    groups = mem_w // LANES
    heads_per_group = LANES // MEM_HEAD_DIM
    lane = lax.broadcasted_iota(jnp.int32, (1, LANES), 1)
    head_lanes = [(lane >= hh * MEM_HEAD_DIM) & (lane < (hh + 1) * MEM_HEAD_DIM)
                  for hh in range(heads_per_group)]

    @pl.when(pl.program_id(1) == 0)
    def _():
        mkv = _dot(mem_ref[...].astype(BF16), wmkv_ref[...])
        mk_ref[...] = mkv[:, :mem_w].astype(BF16)
        for p in range(groups):
            mvp = mkv[:, mem_w + p * LANES:mem_w + (p + 1) * LANES].astype(BF16)
            mv_ref[p] = jnp.concatenate([mvp, jnp.ones_like(mvp)], axis=-1)

    a_w = a_ref.shape[1]
    sub = MIX_SUB_ROWS

    n_sub = a_ref.shape[0] // sub

    def scores(p, hh):
        cols = slice(p * LANES, (p + 1) * LANES)
        qp = qm_ref[:, cols]
        qh = jnp.where(head_lanes[hh], qp, jnp.zeros_like(qp))
        s = _dot_nt(qh, mk_ref[:, cols])
        return s, jnp.max(s, axis=-1, keepdims=True)

    def attend(p, hh, s, m_row):
        acc = _dot(jnp.exp2(s - m_row).astype(BF16), mv_ref[p])
        return acc[:, :LANES] / acc[:, LANES:]

    def project_a(r):
        return _dot(a_ref[r * sub:(r + 1) * sub, :], wout_ref[:a_w, :])

    def epilogue(r, mix):
        rows = slice(r * sub, (r + 1) * sub)
        o_ref[rows, :] = h_ref[rows, :] + mix * _rms_scale(mix) * gain_ref[...]

    work = [(p, hh) for p in range(groups) for hh in range(heads_per_group)]
    outs, mix_a = {}, {}
    pending = scores(*work[0])
    for i, (p, hh) in enumerate(work):
        current = pending
        if i + 1 < len(work):
            pending = scores(*work[i + 1])
        if i < n_sub // 2:
            mix_a[i] = project_a(i)
        outs[p, hh] = attend(p, hh, *current)
    mem_out = jnp.concatenate(
        [jnp.where(head_lanes[0], outs[p, 0], outs[p, 1]).astype(BF16) for p in range(groups)],
        axis=-1)
    projected = []
    for r in range(n_sub):
        if r not in mix_a:
            mix_a[r] = project_a(r)
            while projected:
                epilogue(*projected.pop(0))
        mix = mix_a.pop(r) + _dot(mem_out[r * sub:(r + 1) * sub], wout_ref[a_w:, :])
        projected.append((r, mix))
    while projected:
        epilogue(*projected.pop(0))


def _mix_out(a2d, qm_src, qm_col_block, mem, w_mkv, w_out, h2d, gain, batch, seq):
    m, d = h2d.shape
    a_w = a2d.shape[1]
    mem_w = MEM_HEADS * MEM_HEAD_DIM
    mem_len = mem.shape[1]
    steps = seq // MIX_ROW_TILE
    row = lambda b, s: b * steps + s
    return pl.pallas_call(
        _mix_out_kernel,
        grid=(batch, steps),
        in_specs=[pl.BlockSpec((MIX_ROW_TILE, a_w), lambda b, s: (row(b, s), 0)),
                  pl.BlockSpec((MIX_ROW_TILE, mem_w), lambda b, s: (row(b, s), qm_col_block)),
                  pl.BlockSpec((None, mem_len, d), lambda b, s: (b, 0, 0)),
                  _resident(w_mkv.shape), _resident(w_out.shape),
                  pl.BlockSpec((MIX_ROW_TILE, d), lambda b, s: (row(b, s), 0)),
                  _resident(gain.shape)],
        out_specs=pl.BlockSpec((MIX_ROW_TILE, d), lambda b, s: (row(b, s), 0)),
        out_shape=jax.ShapeDtypeStruct((m, d), F32),
        scratch_shapes=[pltpu.VMEM((mem_len, mem_w), BF16),
                        pltpu.VMEM((mem_w // LANES, mem_len, 2 * LANES), BF16)],
        compiler_params=pltpu.CompilerParams(
            dimension_semantics=("parallel", "arbitrary"),
            vmem_limit_bytes=V7X_VMEM_LIMIT_BYTES),
        name="mix_out",
    )(a2d, qm_src, mem, w_mkv, w_out, h2d, gain)


def _moba_kernel(slopes_ref, q_ref, k_ref, v_ref, o_ref, tmpl_ref, km_ref):
    seq = k_ref.shape[0]
    bs = MOBA_BLOCK
    nb = seq // bs
    n_groups = k_ref.shape[1] // LANES
    heads_per_group = LANES // MOBA_HEAD_DIM
    first_gated = MOBA_TOPK + 1
    p = pl.program_id(0)
    lane = lax.broadcasted_iota(jnp.int32, (1, LANES), 1)
    bias_lane0 = [nb + 3 * hh for hh in range(heads_per_group)]
    in_head = [(lane >= hh * MOBA_HEAD_DIM) & (lane < (hh + 1) * MOBA_HEAD_DIM)
               for hh in range(heads_per_group)]

    @pl.when(pl.program_id(1) == 0)
    def _():
        posi = lax.broadcasted_iota(jnp.int32, (seq, 1), 0)
        pos = posi.astype(F32)
        for gi in range(n_groups):
            t = jnp.where(lane == posi // bs, 1.0, 0.0)
            for hh in range(heads_per_group):
                head = heads_per_group * (n_groups * p + gi) + hh
                x = (slopes_ref[head] * LOG2_E) * pos
                hi = x.astype(BF16).astype(F32)
                mid = (x - hi).astype(BF16).astype(F32)
                lo = (x - hi - mid).astype(BF16).astype(F32)
                t = jnp.where(lane == bias_lane0[hh], hi, t)
                t = jnp.where(lane == bias_lane0[hh] + 1, mid, t)
                t = jnp.where(lane == bias_lane0[hh] + 2, lo, t)
            tmpl_ref[gi] = t.astype(BF16)

    rowi = lax.broadcasted_iota(jnp.int32, (bs, bs), 0)
    coli = lax.broadcasted_iota(jnp.int32, (bs, bs), 1)
    causal = coli <= rowi
    blk_id = lax.broadcasted_iota(jnp.int32, (nb, bs), 0)
    gated_rows = slice(first_gated * bs, seq)

    def prepare(gi):
        cols = slice(gi * LANES, (gi + 1) * LANES)
        k_all = k_ref[:, cols]
        for n in range(nb):
            kb = k_all[n * bs:(n + 1) * bs, :].astype(F32)
            km_ref[gi, n:n + 1, :] = jnp.sum(kb, axis=0, keepdims=True) * (1.0 / bs)
        km = km_ref[gi]
        km_hi = km.astype(BF16)
        km_lo = (km - km_hi.astype(F32)).astype(BF16)
        q_all = q_ref[:, cols]
        v_all = v_ref[:, cols]
        k_ext = jnp.concatenate([k_all, tmpl_ref[gi]], axis=-1)
        v_ext = jnp.concatenate([v_all, jnp.ones_like(v_all)], axis=-1)
        q_ext = []
        for hh in range(heads_per_group):
            qh = jnp.where(in_head[hh], q_all, jnp.zeros_like(q_all))
            gate_t = _dot_nt(km_hi, qh[gated_rows]) + _dot_nt(km_lo, qh[gated_rows])
            bias_t = []
            for j in range(first_gated, nb):
                g = gate_t[:, (j - first_gated) * bs:(j - first_gated + 1) * bs]
                beaten_by = jnp.zeros((nb, bs), F32)
                for mth in range(j):
                    gm = g[mth:mth + 1, :]
                    wins = (gm > g) | ((gm == g) & (blk_id > mth))
                    beaten_by = beaten_by + jnp.where(wins, 1.0, 0.0)
                keep = (beaten_by < float(MOBA_TOPK)) | (blk_id >= j)
                bias_t.append(jnp.where(keep, 0.0, NEG_INF))
            bias_t = jnp.concatenate(bias_t, axis=1)
            block_bias = jnp.concatenate(
                [bias_t, jnp.zeros((LANES - nb, bias_t.shape[1]), F32)], axis=0).T
            ones_lanes = jnp.where(
                (lane >= bias_lane0[hh]) & (lane < bias_lane0[hh] + 3), 1.0, 0.0)
            gated_aug = (block_bias + ones_lanes).astype(BF16)
            blocks = []
            for j in range(nb):
                aug = (gated_aug[(j - first_gated) * bs:(j - first_gated + 1) * bs]
                       if j >= first_gated
                       else jnp.broadcast_to(ones_lanes.astype(BF16), (bs, LANES)))
                blocks.append(jnp.concatenate([qh[j * bs:(j + 1) * bs], aug], axis=-1))
            q_ext.append(blocks)
        return k_ext, v_ext, q_ext

    def scores(group, j, hh):
        k_ext, _, q_ext = group
        rows = slice(j * bs, (j + 1) * bs)
        qa = q_ext[hh][j]
        s_own = jnp.where(causal, _dot_nt(qa, k_ext[rows]), NEG_INF)
        m_row = jnp.max(s_own, axis=-1, keepdims=True)
        s_past = None
        if j:
            s_past = _dot_nt(qa, k_ext[:j * bs])
            m_row = jnp.maximum(m_row, jnp.max(s_past, axis=-1, keepdims=True))
        return s_own, s_past, m_row

    def attend(group, j, s_own, s_past, m_row):
        v_ext = group[1]
        rows = slice(j * bs, (j + 1) * bs)
        acc = _dot(jnp.exp2(s_own - m_row).astype(BF16), v_ext[rows])
        if j:
            acc = acc + _dot(jnp.exp2(s_past - m_row).astype(BF16), v_ext[:j * bs])
        return acc[:, :LANES] / acc[:, LANES:]

    order = [j for pair in zip(range(nb - 1, -1, -1), range(nb)) for j in pair][:nb]
    work = [(gi, j, hh) for gi in range(n_groups) for j in order
            for hh in range(heads_per_group)]
    groups = {0: prepare(0)}
    head_out = {}
    pending = scores(groups[0], *work[0][1:])
    for i, (gi, j, hh) in enumerate(work):
        current = pending
        if i + 1 < len(work):
            nxt = work[i + 1]
            if nxt[0] not in groups:
                groups[nxt[0]] = prepare(nxt[0])
            pending = scores(groups[nxt[0]], *nxt[1:])
        head_out[hh] = attend(groups[gi], j, *current)
        if i == 0 and n_groups > 1:
            groups[1] = prepare(1)
        if hh == heads_per_group - 1:
            o_ref[j * bs:(j + 1) * bs, gi * LANES:(gi + 1) * LANES] = jnp.where(
                in_head[0], head_out[0], head_out[1]).astype(o_ref.dtype)


def _alibi_slopes(n):
    def pow2(m):
        return [2.0 ** (-8.0 * (i + 1) / m) for i in range(m)]
    p = 2 ** int(math.floor(math.log2(n)))
    s = pow2(p)
    if p < n:
        s = s + pow2(2 * p)[0::2][: n - p]
    return np.asarray(s, dtype=np.float32)


def _moba(qproj3, kv3):
    batch, seq, _ = qproj3.shape
    n_pairs = MOBA_HEADS * MOBA_HEAD_DIM // LANES
    nb = seq // MOBA_BLOCK
    slopes = jnp.asarray(_alibi_slopes(MOBA_HEADS))
    gps = MOBA_GROUPS_PER_STEP
    width = gps * LANES
    steps = n_pairs // gps
    grid_spec = pltpu.PrefetchScalarGridSpec(
        num_scalar_prefetch=1,
        grid=(steps, batch),
        in_specs=[pl.BlockSpec((None, seq, width), lambda p, b, *_: (b, 0, p)),
                  pl.BlockSpec((None, seq, width), lambda p, b, *_: (b, 0, p)),
                  pl.BlockSpec((None, seq, width), lambda p, b, *_: (b, 0, steps + p))],
        out_specs=pl.BlockSpec((None, seq, width), lambda p, b, *_: (b, 0, p)),
        scratch_shapes=[pltpu.VMEM((gps, seq, LANES), BF16), pltpu.VMEM((gps, nb, LANES), F32)],
    )
    return pl.pallas_call(
        _moba_kernel,
        grid_spec=grid_spec,
        out_shape=jax.ShapeDtypeStruct((batch, seq, n_pairs * LANES), BF16),
        compiler_params=pltpu.CompilerParams(
            dimension_semantics=("parallel", "arbitrary"),
            vmem_limit_bytes=V7X_VMEM_LIMIT_BYTES),
        name="moba",
    )(slopes, qproj3, kv3, kv3)


def kernel(x, mem, w_in_a, ret_norm_gain, w_out_a, kv_norm_gain, w_kv_shared, w_in_b, w_out_b,
           w_mem_kv, norm_pre_mix, norm_post_mix, norm_pre_mlp, norm_post_mlp, w_up, w_down):
    batch, seq, d = x.shape
    m = batch * seq
    depth = norm_pre_mix.shape[0]
    n_a = depth // 2
    qk_w = RET_HEADS * RET_QK_DIM
    v_w = RET_HEADS * RET_V_DIM
    mem_w = MEM_HEADS * MEM_HEAD_DIM
    moba_w = MOBA_HEADS * MOBA_HEAD_DIM
    row_vec = lambda g: g.reshape(1, -1)
    mem_q_scale = jnp.full((1, mem_w), MEM_HEAD_DIM ** -0.5 * LOG2_E, F32)
    moba_q_scale = jnp.full((1, moba_w), MOBA_HEAD_DIM ** -0.5 * LOG2_E, F32)
    scale_a = jnp.concatenate(
        [jnp.ones((1, 2 * qk_w + 2 * v_w), F32), mem_q_scale], axis=1)
    scale_b = jnp.concatenate([moba_q_scale, mem_q_scale], axis=1)
    scale_kv = jnp.ones((1, 2 * moba_w), F32)

    def layer_weights(l):
        w = {"mem_kv": w_mem_kv[l], "up": w_up[l], "down": w_down[l]}
        if l < n_a:
            w.update({"in": w_in_a[l], "out": w_out_a[l]})
        else:
            w.update({"in": w_in_b[l - n_a], "out": w_out_b[l - n_a]})
            if l == n_a:
                w["kv"] = w_kv_shared
        return w

    def hosted_casts(names, weights):
        slab_ok = lambda w: w.shape[0] % (m // ROW_TILE * 16) == 0
        return [n for n in names if slab_ok(weights[n])]

    h = x.reshape(m, d)
    kv3 = None
    ready = {}
    for l in range(depth):
        f32w = layer_weights(l)
        bf16w = dict(ready)
        if l < n_a:
            bf16w["in"] = f32w["in"].astype(BF16)
            names = hosted_casts([n for n in f32w if n not in bf16w], f32w)
            y, q_m, cast = _proj_retention(h, norm_pre_mix[l:l + 1], bf16w["in"], scale_a,
                                           row_vec(ret_norm_gain[l]), batch, seq,
                                           [f32w[n] for n in names])
            bf16w.update(zip(names, cast))
        for n in f32w:
            if n not in bf16w:
                bf16w[n] = f32w[n].astype(BF16)
        if l < n_a:
            h = _mix_out(y, q_m, 0, mem, bf16w["mem_kv"], bf16w["out"], h,
                         row_vec(norm_post_mix[l]), batch, seq)
        else:
            if l == n_a:
                qproj, kv = _norm_proj(
                    h, jnp.stack([norm_pre_mix[l], kv_norm_gain]),
                    [bf16w["in"], bf16w["kv"]], [scale_b, scale_kv])
                kv3 = kv.reshape(batch, seq, 2 * moba_w)
            else:
                (qproj,) = _norm_proj(h, norm_pre_mix[l:l + 1], [bf16w["in"]], [scale_b])
            a = _moba(qproj.reshape(batch, seq, moba_w + mem_w), kv3)
            h = _mix_out(a.reshape(m, moba_w), qproj, moba_w // mem_w, mem,
                         bf16w["mem_kv"], bf16w["out"], h,
                         row_vec(norm_post_mix[l]), batch, seq)
        nxt = layer_weights(l + 1) if l + 1 < depth else {}
        names = hosted_casts(list(nxt), nxt)
        h, cast = _mlp(h, row_vec(norm_pre_mlp[l]), row_vec(norm_post_mlp[l]),
                       bf16w["up"], bf16w["down"], [nxt[n] for n in names])
        ready = dict(zip(names, cast))
    return h.reshape(batch, seq, d)
```

```python
import functools
import math

import numpy as np
import jax
import jax.numpy as jnp
from jax import lax
from jax.experimental import pallas as pl
from jax.experimental.pallas import tpu as pltpu

F32 = jnp.float32
BF16 = jnp.bfloat16

EPS = 1e-6
NEG_INF = -1e30
LOG2_E = math.log2(math.e)

LANES = 128
V7X_VMEM_LIMIT_BYTES = 56 * 1024 * 1024

RET_HEADS = 4
RET_QK_DIM = 128
RET_V_DIM = 192
RET_CHUNK = 128
MOBA_HEADS = 12
MOBA_HEAD_DIM = 64
MOBA_BLOCK = 256
MOBA_TOPK = 3
MOBA_GROUPS_PER_STEP = 2
MEM_HEADS = 4
MEM_HEAD_DIM = 64

ROW_TILE = 1024
DENSE_SUB_ROWS = 512
RET_ROW_TILE = 1024
RET_SUB_ROWS = 256
PROJ_PIECE_COLS = 256
MIX_ROW_TILE = 1024
MIX_SUB_ROWS = 256
FF_CHUNK = 1024


def _rms_scale(x):
    return lax.rsqrt(jnp.mean(x * x, axis=-1, keepdims=True) + EPS)


def _dot(a, b):
    return jnp.dot(a, b, preferred_element_type=F32)


def _dot_nt(a, b):
    return lax.dot_general(a, b, (((1,), (1,)), ((), ())), preferred_element_type=F32)


def _dot_tn(a, b):
    return lax.dot_general(a, b, (((0,), (0,)), ((), ())), preferred_element_type=F32)


def _resident(shape):
    zeros = (0,) * len(shape)
    return pl.BlockSpec(shape, lambda *_: zeros, pipeline_mode=pl.Buffered(1))


def _cast_slab_specs(weights, n_steps, linear_step):
    in_specs, out_specs = [], []
    for arr, layer in weights:
        rows, cols = arr.shape[-2] // n_steps, arr.shape[-1]
        out_specs.append(pl.BlockSpec((rows, cols), lambda *idx: (linear_step(*idx), 0)))
        if layer is None:
            in_specs.append(out_specs[-1])
        else:
            in_specs.append(pl.BlockSpec(
                (None, rows, cols), lambda *idx, _l=layer: (_l, linear_step(*idx), 0)))
    return in_specs, out_specs


def _cast_slabs(src_refs, dst_refs):
    for src, dst in zip(src_refs, dst_refs):
        dst[...] = src[...].astype(dst.dtype)


def _norm_proj_kernel(x_ref, g_ref, *refs, n_out, n_cast):
    w_refs, s_refs = refs[:n_out], refs[n_out:2 * n_out]
    o_refs = refs[2 * n_out + n_cast:3 * n_out + n_cast]
    _cast_slabs(refs[2 * n_out:2 * n_out + n_cast], refs[3 * n_out + n_cast:])
    sub = DENSE_SUB_ROWS
    n_sub = x_ref.shape[0] // sub

    def normed(r):
        x = x_ref[r * sub:(r + 1) * sub, :]
        return x * _rms_scale(x)

    xhat, xhat_next = normed(0), None
    for r in range(n_sub):
        rows = slice(r * sub, (r + 1) * sub)
        for i in range(n_out):
            hn = (xhat * g_ref[i:i + 1, :]).astype(BF16)
            out = _dot(hn, w_refs[i][...])
            if i == 0 and r + 1 < n_sub:
                xhat_next = normed(r + 1)
            o_refs[i][rows, :] = (out * s_refs[i][...]).astype(o_refs[i].dtype)
        xhat = xhat_next


def _norm_proj(x2d, gains, weights, col_scales, casts):
    m, d = x2d.shape
    n_out = len(weights)
    n_steps = m // ROW_TILE
    cast_in, cast_out = _cast_slab_specs(casts, n_steps, lambda i: i)
    outs = pl.pallas_call(
        functools.partial(_norm_proj_kernel, n_out=n_out, n_cast=len(casts)),
        grid=(n_steps,),
        in_specs=[pl.BlockSpec((ROW_TILE, d), lambda i: (i, 0)),
                  _resident(gains.shape)]
                 + [_resident(w.shape) for w in weights]
                 + [_resident(s.shape) for s in col_scales] + cast_in,
        out_specs=[pl.BlockSpec((ROW_TILE, w.shape[1]), lambda i: (i, 0)) for w in weights]
                  + cast_out,
        out_shape=[jax.ShapeDtypeStruct((m, w.shape[1]), BF16) for w in weights]
                  + [jax.ShapeDtypeStruct(w.shape[-2:], BF16) for w, _ in casts],
        compiler_params=pltpu.CompilerParams(
            dimension_semantics=("parallel",), vmem_limit_bytes=V7X_VMEM_LIMIT_BYTES),
        name="norm_proj",
    )(x2d, gains, *weights, *col_scales, *[w for w, _ in casts])
    return outs[:n_out], outs[n_out:]


def _mlp_kernel(h_ref, g_pre_ref, g_post_ref, wu_ref, wd_ref, o_ref):
    sub = DENSE_SUB_ROWS
    n_sub = h_ref.shape[0] // sub
    d_ff = wu_ref.shape[1]

    def normed(r):
        h = h_ref[r * sub:(r + 1) * sub, :]
        return (h * _rms_scale(h) * g_pre_ref[...]).astype(BF16)

    def epilogue(r, y):
        rows = slice(r * sub, (r + 1) * sub)
        o_ref[rows, :] = h_ref[rows, :] + y * _rms_scale(y) * g_post_ref[...]

    hn, hn_next, finished = normed(0), None, None
    for r in range(n_sub):
        y = None
        for c in range(d_ff // FF_CHUNK):
            cols = slice(c * FF_CHUNK, (c + 1) * FF_CHUNK)
            u = _dot(hn, wu_ref[:, cols])
            if c == 0:
                if r + 1 < n_sub:
                    hn_next = normed(r + 1)
                if finished is not None:
                    epilogue(*finished)
            a = jnp.square(jnp.maximum(u, 0.0)).astype(BF16)
            d = _dot(a, wd_ref[cols, :])
            y = d if y is None else y + d
        finished, hn = (r, y), hn_next
    epilogue(*finished)


def _mlp(h2d, g_pre, g_post, w_up, w_down):
    m, d = h2d.shape
    row_spec = pl.BlockSpec((ROW_TILE, d), lambda i: (i, 0))
    return pl.pallas_call(
        _mlp_kernel,
        grid=(m // ROW_TILE,),
        in_specs=[row_spec, _resident(g_pre.shape), _resident(g_post.shape),
                  _resident(w_up.shape), _resident(w_down.shape)],
        out_specs=row_spec,
        out_shape=jax.ShapeDtypeStruct((m, d), F32),
        compiler_params=pltpu.CompilerParams(
            dimension_semantics=("parallel",), vmem_limit_bytes=V7X_VMEM_LIMIT_BYTES),
        name="mlp",
    )(h2d, g_pre, g_post, w_up, w_down)


def _proj_retention_kernel(x_ref, g_pre_ref, w_ref, scale_ref, gain_ref, *refs, n_cast):
    y_ref, qm_ref = refs[n_cast:n_cast + 2]
    state_ref = refs[-1]
    _cast_slabs(refs[:n_cast], refs[n_cast + 2:-1])
    c = RET_CHUNK
    sub = RET_SUB_ROWS
    qk_w = RET_HEADS * RET_QK_DIM
    v_w = RET_HEADS * RET_V_DIM

    @pl.when(pl.program_id(1) == 0)
    def _():
        state_ref[...] = jnp.zeros(state_ref.shape, F32)

    row = lax.broadcasted_iota(jnp.int32, (c, c), 0)
    col = lax.broadcasted_iota(jnp.int32, (c, c), 1)
    rel = (row - col).astype(F32)
    pos = lax.broadcasted_iota(jnp.int32, (c, 1), 0).astype(F32)
    qk_scale = RET_QK_DIM ** -0.5

    tables = []
    for h in range(RET_HEADS):
        log_g = float(np.log1p(-np.exp2(np.float32(-5.0 - h))))
        d_intra = jnp.where(rel >= 0, jnp.exp(log_g * jnp.maximum(rel, 0.0)), 0.0) * qk_scale
        zeta = jnp.exp(log_g * (c - 1.0 - pos)) * qk_scale
        xi = jnp.exp(log_g * (pos + 1.0))
        chunk_decay = float(np.exp(np.float32(log_g * c)))
        tables.append((d_intra, zeta, xi, chunk_decay))

    n_pieces = w_ref.shape[1] // PROJ_PIECE_COLS

    def project_pieces(r):
        x = x_ref[r * sub:(r + 1) * sub, :]
        hn = (x * _rms_scale(x) * g_pre_ref[...]).astype(BF16)

        def piece(n):
            cols = slice(n * PROJ_PIECE_COLS, (n + 1) * PROJ_PIECE_COLS)
            return (_dot(hn, w_ref[:, cols]) * scale_ref[:, cols]).astype(BF16)
        return [functools.partial(piece, n) for n in range(n_pieces)]

    def retain(r, proj, next_pieces):
        qm_ref[r * sub:(r + 1) * sub, :] = proj[:, 2 * qk_w + 2 * v_w:]
        bodies = [(ci, h) for ci in range(sub // c) for h in range(RET_HEADS)]
        done = []

        def qk_scores(ci, h):
            rows = slice(ci * c, (ci + 1) * c)
            q = proj[rows, h * RET_QK_DIM:(h + 1) * RET_QK_DIM]
            k = proj[rows, qk_w + h * RET_QK_DIM:qk_w + (h + 1) * RET_QK_DIM]
            return q, k, _dot_nt(q, k)

        pending = qk_scores(*bodies[0])
        for i, (ci, h) in enumerate(bodies):
            while len(done) < (i + 1) * len(next_pieces) // len(bodies):
                done.append(next_pieces[len(done)]())
            q, k, qk = pending
            if i + 1 < len(bodies):
                pending = qk_scores(*bodies[i + 1])
            rows = slice(ci * c, (ci + 1) * c)
            out_rows = slice(r * sub + ci * c, r * sub + (ci + 1) * c)
            d_intra, zeta, xi, chunk_decay = tables[h]
            v_cols = slice(h * RET_V_DIM, (h + 1) * RET_V_DIM)
            v = proj[rows, 2 * qk_w + h * RET_V_DIM:2 * qk_w + (h + 1) * RET_V_DIM]
            g = proj[rows, 2 * qk_w + v_w + h * RET_V_DIM:
                     2 * qk_w + v_w + (h + 1) * RET_V_DIM].astype(F32)
            state = state_ref[h]
            s = qk * d_intra
            y = _dot(s.astype(BF16), v) + _dot(q, state.astype(BF16)) * xi
            kz = (k.astype(F32) * zeta).astype(BF16)
            state_ref[h] = state * chunk_decay + _dot_tn(kz, v)
            ms = jnp.mean(y * y, axis=-1, keepdims=True)
            yn = y * lax.rsqrt(ms + EPS) * gain_ref[:, v_cols]
            y_ref[out_rows, v_cols] = (g / (1.0 + jnp.exp2(g * -LOG2_E)) * yn).astype(
                y_ref.dtype)
        return done

    n_groups = x_ref.shape[0] // sub
    proj = jnp.concatenate([piece() for piece in project_pieces(0)], axis=1)
    for r in range(n_groups):
        next_pieces = project_pieces(r + 1) if r + 1 < n_groups else []
        done = retain(r, proj, next_pieces)
        if done:
            proj = jnp.concatenate(done, axis=1)


def _proj_retention(x2d, g_pre, w_in, col_scale, gn_gain, batch, seq, casts):
    m, d = x2d.shape
    steps = seq // RET_ROW_TILE
    v_w = RET_HEADS * RET_V_DIM
    mem_w = MEM_HEADS * MEM_HEAD_DIM
    linear_step = lambda b, s: b * steps + s
    row_block = lambda width: pl.BlockSpec((RET_ROW_TILE, width),
                                           lambda b, s: (linear_step(b, s), 0))
    cast_in, cast_out = _cast_slab_specs(casts, batch * steps, linear_step)
    outs = pl.pallas_call(
        functools.partial(_proj_retention_kernel, n_cast=len(casts)),
        grid=(batch, steps),
        in_specs=[row_block(d), _resident(g_pre.shape), _resident(w_in.shape),
                  _resident(col_scale.shape), _resident(gn_gain.shape)] + cast_in,
        out_specs=[row_block(v_w), row_block(mem_w)] + cast_out,
        out_shape=[jax.ShapeDtypeStruct((m, v_w), BF16), jax.ShapeDtypeStruct((m, mem_w), BF16)]
                  + [jax.ShapeDtypeStruct(w.shape[-2:], BF16) for w, _ in casts],
        scratch_shapes=[pltpu.VMEM((RET_HEADS, RET_QK_DIM, RET_V_DIM), F32)],
        compiler_params=pltpu.CompilerParams(
            dimension_semantics=("parallel", "arbitrary"),
            vmem_limit_bytes=V7X_VMEM_LIMIT_BYTES),
        name="proj_retention",
    )(x2d, g_pre, w_in, col_scale, gn_gain, *[w for w, _ in casts])
    return outs[0], outs[1], outs[2:]


def _mix_out_kernel(a_ref, qm_ref, mem_ref, wmkv_ref, wout_ref, h_ref, gain_ref, o_ref,
                    mk_ref, mv_ref):
    mem_w = MEM_HEADS * MEM_HEAD_DIM
    groups = mem_w // LANES
    heads_per_group = LANES // MEM_HEAD_DIM
    lane = lax.broadcasted_iota(jnp.int32, (1, LANES), 1)
    head_lanes = [(lane >= hh * MEM_HEAD_DIM) & (lane < (hh + 1) * MEM_HEAD_DIM)
                  for hh in range(heads_per_group)]

    @pl.when(pl.program_id(1) == 0)
    def _():
        mkv = _dot(mem_ref[...].astype(BF16), wmkv_ref[...])
        mk_ref[...] = mkv[:, :mem_w].astype(BF16)
        for p in range(groups):
            mvp = mkv[:, mem_w + p * LANES:mem_w + (p + 1) * LANES].astype(BF16)
            mv_ref[p] = jnp.concatenate([mvp, jnp.ones_like(mvp)], axis=-1)

    a_w = a_ref.shape[1]
    sub = MIX_SUB_ROWS

    n_sub = a_ref.shape[0] // sub

    def scores(p, hh):
        cols = slice(p * LANES, (p + 1) * LANES)
        qp = qm_ref[:, cols]
        qh = jnp.where(head_lanes[hh], qp, jnp.zeros_like(qp))
        s = _dot_nt(qh, mk_ref[:, cols])
        return s, jnp.max(s, axis=-1, keepdims=True)

    def attend(p, hh, s, m_row):
        acc = _dot(jnp.exp2(s - m_row).astype(BF16), mv_ref[p])
        return acc[:, :LANES] / acc[:, LANES:]

    def project_a(r):
        return _dot(a_ref[r * sub:(r + 1) * sub, :], wout_ref[:a_w, :])

    def epilogue(r, mix):
        rows = slice(r * sub, (r + 1) * sub)
        o_ref[rows, :] = h_ref[rows, :] + mix * _rms_scale(mix) * gain_ref[...]

    work = [(p, hh) for p in range(groups) for hh in range(heads_per_group)]
    outs, mix_a = {}, {}
    pending = scores(*work[0])
    for i, (p, hh) in enumerate(work):
        current = pending
        if i + 1 < len(work):
            pending = scores(*work[i + 1])
        if i < n_sub // 2:
            mix_a[i] = project_a(i)
        outs[p, hh] = attend(p, hh, *current)
    mem_out = jnp.concatenate(
        [jnp.where(head_lanes[0], outs[p, 0], outs[p, 1]).astype(BF16) for p in range(groups)],
        axis=-1)
    projected = []
    for r in range(n_sub):
        if r not in mix_a:
            mix_a[r] = project_a(r)
            while projected:
                epilogue(*projected.pop(0))
        mix = mix_a.pop(r) + _dot(mem_out[r * sub:(r + 1) * sub], wout_ref[a_w:, :])
        projected.append((r, mix))
    while projected:
        epilogue(*projected.pop(0))


def _mix_out(a2d, qm_src, qm_col_block, mem, w_mkv, w_out, h2d, gain, batch, seq):
    m, d = h2d.shape
    a_w = a2d.shape[1]
    mem_w = MEM_HEADS * MEM_HEAD_DIM
    mem_len = mem.shape[1]
    steps = seq // MIX_ROW_TILE
    row = lambda b, s: b * steps + s
    return pl.pallas_call(
        _mix_out_kernel,
        grid=(batch, steps),
        in_specs=[pl.BlockSpec((MIX_ROW_TILE, a_w), lambda b, s: (row(b, s), 0)),
                  pl.BlockSpec((MIX_ROW_TILE, mem_w), lambda b, s: (row(b, s), qm_col_block)),
                  pl.BlockSpec((None, mem_len, d), lambda b, s: (b, 0, 0)),
                  _resident(w_mkv.shape), _resident(w_out.shape),
                  pl.BlockSpec((MIX_ROW_TILE, d), lambda b, s: (row(b, s), 0)),
                  _resident(gain.shape)],
        out_specs=pl.BlockSpec((MIX_ROW_TILE, d), lambda b, s: (row(b, s), 0)),
        out_shape=jax.ShapeDtypeStruct((m, d), F32),
        scratch_shapes=[pltpu.VMEM((mem_len, mem_w), BF16),
                        pltpu.VMEM((mem_w // LANES, mem_len, 2 * LANES), BF16)],
        compiler_params=pltpu.CompilerParams(
            dimension_semantics=("parallel", "arbitrary"),
            vmem_limit_bytes=V7X_VMEM_LIMIT_BYTES),
        name="mix_out",
    )(a2d, qm_src, mem, w_mkv, w_out, h2d, gain)


def _moba_kernel(slopes_ref, q_ref, k_ref, v_ref, o_ref, tmpl_ref, km_ref):
    seq = k_ref.shape[0]
    bs = MOBA_BLOCK
    nb = seq // bs
    n_groups = k_ref.shape[1] // LANES
    heads_per_group = LANES // MOBA_HEAD_DIM
    first_gated = MOBA_TOPK + 1
    p = pl.program_id(0)
    lane = lax.broadcasted_iota(jnp.int32, (1, LANES), 1)
    bias_lane0 = [nb + 3 * hh for hh in range(heads_per_group)]
    in_head = [(lane >= hh * MOBA_HEAD_DIM) & (lane < (hh + 1) * MOBA_HEAD_DIM)
               for hh in range(heads_per_group)]

    @pl.when(pl.program_id(1) == 0)
    def _():
        posi = lax.broadcasted_iota(jnp.int32, (seq, 1), 0)
        pos = posi.astype(F32)
        for gi in range(n_groups):
            t = jnp.where(lane == posi // bs, 1.0, 0.0)
            for hh in range(heads_per_group):
                head = heads_per_group * (n_groups * p + gi) + hh
                x = (slopes_ref[head] * LOG2_E) * pos
                hi = x.astype(BF16).astype(F32)
                mid = (x - hi).astype(BF16).astype(F32)
                lo = (x - hi - mid).astype(BF16).astype(F32)
                t = jnp.where(lane == bias_lane0[hh], hi, t)
                t = jnp.where(lane == bias_lane0[hh] + 1, mid, t)
                t = jnp.where(lane == bias_lane0[hh] + 2, lo, t)
            tmpl_ref[gi] = t.astype(BF16)

    rowi = lax.broadcasted_iota(jnp.int32, (bs, bs), 0)
    coli = lax.broadcasted_iota(jnp.int32, (bs, bs), 1)
    causal = coli <= rowi
    blk_id = lax.broadcasted_iota(jnp.int32, (nb, bs), 0)
    gated_rows = slice(first_gated * bs, seq)

    def prepare(gi):
        cols = slice(gi * LANES, (gi + 1) * LANES)
        k_all = k_ref[:, cols]
        for n in range(nb):
            kb = k_all[n * bs:(n + 1) * bs, :].astype(F32)
            km_ref[gi, n:n + 1, :] = jnp.sum(kb, axis=0, keepdims=True) * (1.0 / bs)
        km = km_ref[gi]
        km_hi = km.astype(BF16)
        km_lo = (km - km_hi.astype(F32)).astype(BF16)
        q_all = q_ref[:, cols]
        v_all = v_ref[:, cols]
        k_ext = jnp.concatenate([k_all, tmpl_ref[gi]], axis=-1)
        v_ext = jnp.concatenate([v_all, jnp.ones_like(v_all)], axis=-1)
        q_ext = []
        for hh in range(heads_per_group):
            qh = jnp.where(in_head[hh], q_all, jnp.zeros_like(q_all))
            gate_t = _dot_nt(km_hi, qh[gated_rows]) + _dot_nt(km_lo, qh[gated_rows])
            bias_t = []
            for j in range(first_gated, nb):
                g = gate_t[:, (j - first_gated) * bs:(j - first_gated + 1) * bs]
                beaten_by = jnp.zeros((nb, bs), F32)
                for mth in range(j):
                    gm = g[mth:mth + 1, :]
                    wins = (gm > g) | ((gm == g) & (blk_id > mth))
                    beaten_by = beaten_by + jnp.where(wins, 1.0, 0.0)
                keep = (beaten_by < float(MOBA_TOPK)) | (blk_id >= j)
                bias_t.append(jnp.where(keep, 0.0, NEG_INF))
            bias_t = jnp.concatenate(bias_t, axis=1)
            block_bias = jnp.concatenate(
                [bias_t, jnp.zeros((LANES - nb, bias_t.shape[1]), F32)], axis=0).T
            ones_lanes = jnp.where(
                (lane >= bias_lane0[hh]) & (lane < bias_lane0[hh] + 3), 1.0, 0.0)
            gated_aug = (block_bias + ones_lanes).astype(BF16)
            blocks = []
            for j in range(nb):
                aug = (gated_aug[(j - first_gated) * bs:(j - first_gated + 1) * bs]
                       if j >= first_gated
                       else jnp.broadcast_to(ones_lanes.astype(BF16), (bs, LANES)))
                blocks.append(jnp.concatenate([qh[j * bs:(j + 1) * bs], aug], axis=-1))
            q_ext.append(blocks)
        return k_ext, v_ext, q_ext

    def scores(group, j, hh):
        k_ext, _, q_ext = group
        rows = slice(j * bs, (j + 1) * bs)
        qa = q_ext[hh][j]
        s_own = jnp.where(causal, _dot_nt(qa, k_ext[rows]), NEG_INF)
        m_row = jnp.max(s_own, axis=-1, keepdims=True)
        s_past = None
        if j:
            s_past = _dot_nt(qa, k_ext[:j * bs])
            m_row = jnp.maximum(m_row, jnp.max(s_past, axis=-1, keepdims=True))
        return s_own, s_past, m_row

    def attend(group, j, s_own, s_past, m_row):
        v_ext = group[1]
        rows = slice(j * bs, (j + 1) * bs)
        acc = _dot(jnp.exp2(s_own - m_row).astype(BF16), v_ext[rows])
        if j:
            acc = acc + _dot(jnp.exp2(s_past - m_row).astype(BF16), v_ext[:j * bs])
        return acc[:, :LANES] / acc[:, LANES:]

    order = [j for pair in zip(range(nb - 1, -1, -1), range(nb)) for j in pair][:nb]
    work = [(gi, j, hh) for gi in range(n_groups) for j in order
            for hh in range(heads_per_group)]
    groups = {0: prepare(0)}
    head_out = {}
    pending = scores(groups[0], *work[0][1:])
    for i, (gi, j, hh) in enumerate(work):
        current = pending
        if i + 1 < len(work):
            nxt = work[i + 1]
            if nxt[0] not in groups:
                groups[nxt[0]] = prepare(nxt[0])
            pending = scores(groups[nxt[0]], *nxt[1:])
        head_out[hh] = attend(groups[gi], j, *current)
        if i == 0 and n_groups > 1:
            groups[1] = prepare(1)
        if hh == heads_per_group - 1:
            o_ref[j * bs:(j + 1) * bs, gi * LANES:(gi + 1) * LANES] = jnp.where(
                in_head[0], head_out[0], head_out[1]).astype(o_ref.dtype)


def _alibi_slopes(n):
    def pow2(m):
        return [2.0 ** (-8.0 * (i + 1) / m) for i in range(m)]
    p = 2 ** int(math.floor(math.log2(n)))
    s = pow2(p)
    if p < n:
        s = s + pow2(2 * p)[0::2][: n - p]
    return np.asarray(s, dtype=np.float32)


def _moba(qproj3, kv3):
    batch, seq, _ = qproj3.shape
    n_pairs = MOBA_HEADS * MOBA_HEAD_DIM // LANES
    nb = seq // MOBA_BLOCK
    slopes = jnp.asarray(_alibi_slopes(MOBA_HEADS))
    gps = MOBA_GROUPS_PER_STEP
    width = gps * LANES
    steps = n_pairs // gps
    grid_spec = pltpu.PrefetchScalarGridSpec(
        num_scalar_prefetch=1,
        grid=(steps, batch),
        in_specs=[pl.BlockSpec((None, seq, width), lambda p, b, *_: (b, 0, p)),
                  pl.BlockSpec((None, seq, width), lambda p, b, *_: (b, 0, p)),
                  pl.BlockSpec((None, seq, width), lambda p, b, *_: (b, 0, steps + p))],
        out_specs=pl.BlockSpec((None, seq, width), lambda p, b, *_: (b, 0, p)),
        scratch_shapes=[pltpu.VMEM((gps, seq, LANES), BF16), pltpu.VMEM((gps, nb, LANES), F32)],
    )
    return pl.pallas_call(
        _moba_kernel,
        grid_spec=grid_spec,
        out_shape=jax.ShapeDtypeStruct((batch, seq, n_pairs * LANES), BF16),
        compiler_params=pltpu.CompilerParams(
            dimension_semantics=("parallel", "arbitrary"),
            vmem_limit_bytes=V7X_VMEM_LIMIT_BYTES),
        name="moba",
    )(slopes, qproj3, kv3, kv3)


def kernel(x, mem, w_in_a, ret_norm_gain, w_out_a, kv_norm_gain, w_kv_shared, w_in_b, w_out_b,
           w_mem_kv, norm_pre_mix, norm_post_mix, norm_pre_mlp, norm_post_mlp, w_up, w_down):
    batch, seq, d = x.shape
    m = batch * seq
    depth = norm_pre_mix.shape[0]
    n_a = depth // 2
    qk_w = RET_HEADS * RET_QK_DIM
    v_w = RET_HEADS * RET_V_DIM
    mem_w = MEM_HEADS * MEM_HEAD_DIM
    moba_w = MOBA_HEADS * MOBA_HEAD_DIM
    row_vec = lambda g: g.reshape(1, -1)
    mem_q_scale = jnp.full((1, mem_w), MEM_HEAD_DIM ** -0.5 * LOG2_E, F32)
    moba_q_scale = jnp.full((1, moba_w), MOBA_HEAD_DIM ** -0.5 * LOG2_E, F32)
    scale_a = jnp.concatenate(
        [jnp.ones((1, 2 * qk_w + 2 * v_w), F32), mem_q_scale], axis=1)
    scale_b = jnp.concatenate([moba_q_scale, mem_q_scale], axis=1)
    scale_kv = jnp.ones((1, 2 * moba_w), F32)

    def layer_weights(l):
        w = {"mem_kv": (w_mem_kv, l), "up": (w_up, l), "down": (w_down, l)}
        if l < n_a:
            w.update({"in": (w_in_a, l), "out": (w_out_a, l)})
        else:
            w.update({"in": (w_in_b, l - n_a), "out": (w_out_b, l - n_a)})
            if l == n_a:
                w["kv"] = (w_kv_shared, None)
        return w

    def cast_outside(arr, index):
        return (arr if index is None else arr[index]).astype(BF16)

    def hosted_casts(names, weights):
        slab_ok = lambda arr: arr.shape[-2] % (m // ROW_TILE * 16) == 0
        return [n for n in names if slab_ok(weights[n][0])]

    h = x.reshape(m, d)
    kv3 = None
    for l in range(depth):
        f32w = layer_weights(l)
        bf16w = {n: cast_outside(*f32w[n]) for n in ("in", "kv") if n in f32w}
        names = hosted_casts([n for n in f32w if n not in bf16w], f32w)
        later = [f32w[n] for n in names]
        if l < n_a:
            y, q_m, cast = _proj_retention(h, norm_pre_mix[l:l + 1], bf16w["in"], scale_a,
                                           row_vec(ret_norm_gain[l]), batch, seq, later)
        elif l == n_a:
            (qproj, kv), cast = _norm_proj(
                h, jnp.stack([norm_pre_mix[l], kv_norm_gain]),
                [bf16w["in"], bf16w["kv"]], [scale_b, scale_kv], later)
            kv3 = kv.reshape(batch, seq, 2 * moba_w)
        else:
            (qproj,), cast = _norm_proj(h, norm_pre_mix[l:l + 1], [bf16w["in"]], [scale_b],
                                        later)
        bf16w.update(zip(names, cast))
        for n in f32w:
            if n not in bf16w:
                bf16w[n] = cast_outside(*f32w[n])
        if l < n_a:
            h = _mix_out(y, q_m, 0, mem, bf16w["mem_kv"], bf16w["out"], h,
                         row_vec(norm_post_mix[l]), batch, seq)
        else:
            a = _moba(qproj.reshape(batch, seq, moba_w + mem_w), kv3)
            h = _mix_out(a.reshape(m, moba_w), qproj, moba_w // mem_w, mem,
                         bf16w["mem_kv"], bf16w["out"], h,
                         row_vec(norm_post_mix[l]), batch, seq)
        h = _mlp(h, row_vec(norm_pre_mlp[l]), row_vec(norm_post_mlp[l]),
                 bf16w["up"], bf16w["down"])
    return h.reshape(batch, seq, d)
```

```python
import functools
import math

import numpy as np
import jax
import jax.numpy as jnp
from jax import lax
from jax.experimental import pallas as pl
from jax.experimental.pallas import tpu as pltpu

F32 = jnp.float32
BF16 = jnp.bfloat16

EPS = 1e-6
NEG_INF = -1e30
LOG2_E = math.log2(math.e)

LANES = 128
V7X_VMEM_LIMIT_BYTES = 56 * 1024 * 1024

RET_HEADS = 4
RET_QK_DIM = 128
RET_V_DIM = 192
RET_CHUNK = 128
MOBA_HEADS = 12
MOBA_HEAD_DIM = 64
MOBA_BLOCK = 256
MOBA_TOPK = 3
MOBA_GROUPS_PER_STEP = 2
MEM_HEADS = 4
MEM_HEAD_DIM = 64

ROW_TILE = 1024
DENSE_SUB_ROWS = 512
RET_ROW_TILE = 1024
RET_SUB_ROWS = 256
PROJ_PIECE_COLS = 256
MIX_ROW_TILE = 1024
MIX_SUB_ROWS = 256
FF_CHUNK = 1024


def _rms_scale(x):
    return lax.rsqrt(jnp.mean(x * x, axis=-1, keepdims=True) + EPS)


def _dot(a, b):
    return jnp.dot(a, b, preferred_element_type=F32)


def _dot_nt(a, b):
    return lax.dot_general(a, b, (((1,), (1,)), ((), ())), preferred_element_type=F32)


def _dot_tn(a, b):
    return lax.dot_general(a, b, (((0,), (0,)), ((), ())), preferred_element_type=F32)


def _resident(shape):
    zeros = (0,) * len(shape)
    return pl.BlockSpec(shape, lambda *_: zeros, pipeline_mode=pl.Buffered(1))


def _cast_slab_specs(weights, n_steps, linear_step):
    in_specs, out_specs = [], []
    for arr, layer in weights:
        rows, cols = arr.shape[-2] // n_steps, arr.shape[-1]
        out_specs.append(pl.BlockSpec((rows, cols), lambda *idx: (linear_step(*idx), 0)))
        if layer is None:
            in_specs.append(out_specs[-1])
        else:
            in_specs.append(pl.BlockSpec(
                (None, rows, cols), lambda *idx, _l=layer: (_l, linear_step(*idx), 0)))
    return in_specs, out_specs


def _cast_slabs(src_refs, dst_refs):
    for src, dst in zip(src_refs, dst_refs):
        dst[...] = src[...].astype(dst.dtype)


def _norm_proj_kernel(x_ref, g_ref, *refs, n_out):
    w_refs, s_refs, o_refs = refs[:n_out], refs[n_out:2 * n_out], refs[2 * n_out:]
    sub = DENSE_SUB_ROWS
    n_sub = x_ref.shape[0] // sub

    def normed(r):
        x = x_ref[r * sub:(r + 1) * sub, :]
        return x * _rms_scale(x)

    xhat, xhat_next = normed(0), None
    for r in range(n_sub):
        rows = slice(r * sub, (r + 1) * sub)
        for i in range(n_out):
            hn = (xhat * g_ref[i:i + 1, :]).astype(BF16)
            out = _dot(hn, w_refs[i][...])
            if i == 0 and r + 1 < n_sub:
                xhat_next = normed(r + 1)
            o_refs[i][rows, :] = (out * s_refs[i][...]).astype(o_refs[i].dtype)
        xhat = xhat_next


def _norm_proj(x2d, gains, weights, col_scales):
    m, d = x2d.shape
    n_out = len(weights)
    return pl.pallas_call(
        functools.partial(_norm_proj_kernel, n_out=n_out),
        grid=(m // ROW_TILE,),
        in_specs=[pl.BlockSpec((ROW_TILE, d), lambda i: (i, 0)),
                  _resident(gains.shape)]
                 + [_resident(w.shape) for w in weights]
                 + [_resident(s.shape) for s in col_scales],
        out_specs=[pl.BlockSpec((ROW_TILE, w.shape[1]), lambda i: (i, 0)) for w in weights],
        out_shape=[jax.ShapeDtypeStruct((m, w.shape[1]), BF16) for w in weights],
        compiler_params=pltpu.CompilerParams(
            dimension_semantics=("parallel",), vmem_limit_bytes=V7X_VMEM_LIMIT_BYTES),
        name="norm_proj",
    )(x2d, gains, *weights, *col_scales)


def _mlp_kernel(h_ref, g_pre_ref, g_post_ref, wu_ref, wd_ref, o_ref):
    sub = DENSE_SUB_ROWS
    n_sub = h_ref.shape[0] // sub
    d_ff = wu_ref.shape[1]

    def normed(r):
        h = h_ref[r * sub:(r + 1) * sub, :]
        return (h * _rms_scale(h) * g_pre_ref[...]).astype(BF16)

    def epilogue(r, y):
        rows = slice(r * sub, (r + 1) * sub)
        o_ref[rows, :] = h_ref[rows, :] + y * _rms_scale(y) * g_post_ref[...]

    hn, hn_next, finished = normed(0), None, None
    for r in range(n_sub):
        y = None
        for c in range(d_ff // FF_CHUNK):
            cols = slice(c * FF_CHUNK, (c + 1) * FF_CHUNK)
            u = _dot(hn, wu_ref[:, cols])
            if c == 0:
                if r + 1 < n_sub:
                    hn_next = normed(r + 1)
                if finished is not None:
                    epilogue(*finished)
            a = jnp.square(jnp.maximum(u, 0.0)).astype(BF16)
            d = _dot(a, wd_ref[cols, :])
            y = d if y is None else y + d
        finished, hn = (r, y), hn_next
    epilogue(*finished)


def _mlp(h2d, g_pre, g_post, w_up, w_down):
    m, d = h2d.shape
    row_spec = pl.BlockSpec((ROW_TILE, d), lambda i: (i, 0))
    return pl.pallas_call(
        _mlp_kernel,
        grid=(m // ROW_TILE,),
        in_specs=[row_spec, _resident(g_pre.shape), _resident(g_post.shape),
                  _resident(w_up.shape), _resident(w_down.shape)],
        out_specs=row_spec,
        out_shape=jax.ShapeDtypeStruct((m, d), F32),
        compiler_params=pltpu.CompilerParams(
            dimension_semantics=("parallel",), vmem_limit_bytes=V7X_VMEM_LIMIT_BYTES),
        name="mlp",
    )(h2d, g_pre, g_post, w_up, w_down)


def _proj_retention_kernel(x_ref, g_pre_ref, w_ref, scale_ref, gain_ref, *refs, n_cast):
    y_ref, qm_ref = refs[n_cast:n_cast + 2]
    state_ref = refs[-1]
    _cast_slabs(refs[:n_cast], refs[n_cast + 2:-1])
    c = RET_CHUNK
    sub = RET_SUB_ROWS
    qk_w = RET_HEADS * RET_QK_DIM
    v_w = RET_HEADS * RET_V_DIM

    @pl.when(pl.program_id(1) == 0)
    def _():
        state_ref[...] = jnp.zeros(state_ref.shape, F32)

    row = lax.broadcasted_iota(jnp.int32, (c, c), 0)
    col = lax.broadcasted_iota(jnp.int32, (c, c), 1)
    rel = (row - col).astype(F32)
    pos = lax.broadcasted_iota(jnp.int32, (c, 1), 0).astype(F32)
    qk_scale = RET_QK_DIM ** -0.5

    tables = []
    for h in range(RET_HEADS):
        log_g = float(np.log1p(-np.exp2(np.float32(-5.0 - h))))
        d_intra = jnp.where(rel >= 0, jnp.exp(log_g * jnp.maximum(rel, 0.0)), 0.0) * qk_scale
        zeta = jnp.exp(log_g * (c - 1.0 - pos)) * qk_scale
        xi = jnp.exp(log_g * (pos + 1.0))
        chunk_decay = float(np.exp(np.float32(log_g * c)))
        tables.append((d_intra, zeta, xi, chunk_decay))

    n_pieces = w_ref.shape[1] // PROJ_PIECE_COLS

    def project_pieces(r):
        x = x_ref[r * sub:(r + 1) * sub, :]
        hn = (x * _rms_scale(x) * g_pre_ref[...]).astype(BF16)

        def piece(n):
            cols = slice(n * PROJ_PIECE_COLS, (n + 1) * PROJ_PIECE_COLS)
            return (_dot(hn, w_ref[:, cols]) * scale_ref[:, cols]).astype(BF16)
        return [functools.partial(piece, n) for n in range(n_pieces)]

    def retain(r, proj, next_pieces):
        qm_ref[r * sub:(r + 1) * sub, :] = proj[:, 2 * qk_w + 2 * v_w:]
        bodies = [(ci, h) for ci in range(sub // c) for h in range(RET_HEADS)]
        done = []

        def qk_scores(ci, h):
            rows = slice(ci * c, (ci + 1) * c)
            q = proj[rows, h * RET_QK_DIM:(h + 1) * RET_QK_DIM]
            k = proj[rows, qk_w + h * RET_QK_DIM:qk_w + (h + 1) * RET_QK_DIM]
            return q, k, _dot_nt(q, k)

        pending = qk_scores(*bodies[0])
        for i, (ci, h) in enumerate(bodies):
            while len(done) < (i + 1) * len(next_pieces) // len(bodies):
                done.append(next_pieces[len(done)]())
            q, k, qk = pending
            if i + 1 < len(bodies):
                pending = qk_scores(*bodies[i + 1])
            rows = slice(ci * c, (ci + 1) * c)
            out_rows = slice(r * sub + ci * c, r * sub + (ci + 1) * c)
            d_intra, zeta, xi, chunk_decay = tables[h]
            v_cols = slice(h * RET_V_DIM, (h + 1) * RET_V_DIM)
            v = proj[rows, 2 * qk_w + h * RET_V_DIM:2 * qk_w + (h + 1) * RET_V_DIM]
            g = proj[rows, 2 * qk_w + v_w + h * RET_V_DIM:
                     2 * qk_w + v_w + (h + 1) * RET_V_DIM].astype(F32)
            state = state_ref[h]
            s = qk * d_intra
            y = _dot(s.astype(BF16), v) + _dot(q, state.astype(BF16)) * xi
            kz = (k.astype(F32) * zeta).astype(BF16)
            state_ref[h] = state * chunk_decay + _dot_tn(kz, v)
            ms = jnp.mean(y * y, axis=-1, keepdims=True)
            yn = y * lax.rsqrt(ms + EPS) * gain_ref[:, v_cols]
            y_ref[out_rows, v_cols] = (g / (1.0 + jnp.exp2(g * -LOG2_E)) * yn).astype(
                y_ref.dtype)
        return done

    n_groups = x_ref.shape[0] // sub
    proj = jnp.concatenate([piece() for piece in project_pieces(0)], axis=1)
    for r in range(n_groups):
        next_pieces = project_pieces(r + 1) if r + 1 < n_groups else []
        done = retain(r, proj, next_pieces)
        if done:
            proj = jnp.concatenate(done, axis=1)


def _proj_retention(x2d, g_pre, w_in, col_scale, gn_gain, batch, seq, casts):
    m, d = x2d.shape
    steps = seq // RET_ROW_TILE
    v_w = RET_HEADS * RET_V_DIM
    mem_w = MEM_HEADS * MEM_HEAD_DIM
    linear_step = lambda b, s: b * steps + s
    row_block = lambda width: pl.BlockSpec((RET_ROW_TILE, width),
                                           lambda b, s: (linear_step(b, s), 0))
    cast_in, cast_out = _cast_slab_specs(casts, batch * steps, linear_step)
    outs = pl.pallas_call(
        functools.partial(_proj_retention_kernel, n_cast=len(casts)),
        grid=(batch, steps),
        in_specs=[row_block(d), _resident(g_pre.shape), _resident(w_in.shape),
                  _resident(col_scale.shape), _resident(gn_gain.shape)] + cast_in,
        out_specs=[row_block(v_w), row_block(mem_w)] + cast_out,
        out_shape=[jax.ShapeDtypeStruct((m, v_w), BF16), jax.ShapeDtypeStruct((m, mem_w), BF16)]
                  + [jax.ShapeDtypeStruct(w.shape[-2:], BF16) for w, _ in casts],
        scratch_shapes=[pltpu.VMEM((RET_HEADS, RET_QK_DIM, RET_V_DIM), F32)],
        compiler_params=pltpu.CompilerParams(
            dimension_semantics=("parallel", "arbitrary"),
            vmem_limit_bytes=V7X_VMEM_LIMIT_BYTES),
        name="proj_retention",
    )(x2d, g_pre, w_in, col_scale, gn_gain, *[w for w, _ in casts])
    return outs[0], outs[1], outs[2:]


def _mix_out_kernel(a_ref, qm_ref, mem_ref, wmkv_ref, wout_ref, h_ref, gain_ref, o_ref,
                    mk_ref, mv_ref):
    mem_w = MEM_HEADS * MEM_HEAD_DIM
    groups = mem_w // LANES
    heads_per_group = LANES // MEM_HEAD_DIM
    lane = lax.broadcasted_iota(jnp.int32, (1, LANES), 1)
    head_lanes = [(lane >= hh * MEM_HEAD_DIM) & (lane < (hh + 1) * MEM_HEAD_DIM)
                  for hh in range(heads_per_group)]

    @pl.when(pl.program_id(1) == 0)
    def _():
        mkv = _dot(mem_ref[...].astype(BF16), wmkv_ref[...])
        mk_ref[...] = mkv[:, :mem_w].astype(BF16)
        for p in range(groups):
            mvp = mkv[:, mem_w + p * LANES:mem_w + (p + 1) * LANES].astype(BF16)
            mv_ref[p] = jnp.concatenate([mvp, jnp.ones_like(mvp)], axis=-1)

    a_w = a_ref.shape[1]
    sub = MIX_SUB_ROWS

    n_sub = a_ref.shape[0] // sub

    def scores(p, hh):
        cols = slice(p * LANES, (p + 1) * LANES)
        qp = qm_ref[:, cols]
        qh = jnp.where(head_lanes[hh], qp, jnp.zeros_like(qp))
        s = _dot_nt(qh, mk_ref[:, cols])
        return s, jnp.max(s, axis=-1, keepdims=True)

    def attend(p, hh, s, m_row):
        acc = _dot(jnp.exp2(s - m_row).astype(BF16), mv_ref[p])
        return acc[:, :LANES] / acc[:, LANES:]

    def project_a(r):
        return _dot(a_ref[r * sub:(r + 1) * sub, :], wout_ref[:a_w, :])

    def epilogue(r, mix):
        rows = slice(r * sub, (r + 1) * sub)
        o_ref[rows, :] = h_ref[rows, :] + mix * _rms_scale(mix) * gain_ref[...]

    work = [(p, hh) for p in range(groups) for hh in range(heads_per_group)]
    outs, mix_a = {}, {}
    pending = scores(*work[0])
    for i, (p, hh) in enumerate(work):
        current = pending
        if i + 1 < len(work):
            pending = scores(*work[i + 1])
        if i < n_sub // 2:
            mix_a[i] = project_a(i)
        outs[p, hh] = attend(p, hh, *current)
    mem_out = jnp.concatenate(
        [jnp.where(head_lanes[0], outs[p, 0], outs[p, 1]).astype(BF16) for p in range(groups)],
        axis=-1)
    projected = []
    for r in range(n_sub):
        if r not in mix_a:
            mix_a[r] = project_a(r)
            while projected:
                epilogue(*projected.pop(0))
        mix = mix_a.pop(r) + _dot(mem_out[r * sub:(r + 1) * sub], wout_ref[a_w:, :])
        projected.append((r, mix))
    while projected:
        epilogue(*projected.pop(0))


def _mix_out(a2d, qm_src, qm_col_block, mem, w_mkv, w_out, h2d, gain, batch, seq):
    m, d = h2d.shape
    a_w = a2d.shape[1]
    mem_w = MEM_HEADS * MEM_HEAD_DIM
    mem_len = mem.shape[1]
    steps = seq // MIX_ROW_TILE
    row = lambda b, s: b * steps + s
    return pl.pallas_call(
        _mix_out_kernel,
        grid=(batch, steps),
        in_specs=[pl.BlockSpec((MIX_ROW_TILE, a_w), lambda b, s: (row(b, s), 0)),
                  pl.BlockSpec((MIX_ROW_TILE, mem_w), lambda b, s: (row(b, s), qm_col_block)),
                  pl.BlockSpec((None, mem_len, d), lambda b, s: (b, 0, 0)),
                  _resident(w_mkv.shape), _resident(w_out.shape),
                  pl.BlockSpec((MIX_ROW_TILE, d), lambda b, s: (row(b, s), 0)),
                  _resident(gain.shape)],
        out_specs=pl.BlockSpec((MIX_ROW_TILE, d), lambda b, s: (row(b, s), 0)),
        out_shape=jax.ShapeDtypeStruct((m, d), F32),
        scratch_shapes=[pltpu.VMEM((mem_len, mem_w), BF16),
                        pltpu.VMEM((mem_w // LANES, mem_len, 2 * LANES), BF16)],
        compiler_params=pltpu.CompilerParams(
            dimension_semantics=("parallel", "arbitrary"),
            vmem_limit_bytes=V7X_VMEM_LIMIT_BYTES),
        name="mix_out",
    )(a2d, qm_src, mem, w_mkv, w_out, h2d, gain)


def _moba_kernel(slopes_ref, q_ref, k_ref, v_ref, o_ref, tmpl_ref, km_ref):
    seq = k_ref.shape[0]
    bs = MOBA_BLOCK
    nb = seq // bs
    n_groups = k_ref.shape[1] // LANES
    heads_per_group = LANES // MOBA_HEAD_DIM
    first_gated = MOBA_TOPK + 1
    p = pl.program_id(0)
    lane = lax.broadcasted_iota(jnp.int32, (1, LANES), 1)
    bias_lane0 = [nb + 3 * hh for hh in range(heads_per_group)]
    in_head = [(lane >= hh * MOBA_HEAD_DIM) & (lane < (hh + 1) * MOBA_HEAD_DIM)
               for hh in range(heads_per_group)]

    @pl.when(pl.program_id(1) == 0)
    def _():
        posi = lax.broadcasted_iota(jnp.int32, (seq, 1), 0)
        pos = posi.astype(F32)
        for gi in range(n_groups):
            t = jnp.where(lane == posi // bs, 1.0, 0.0)
            for hh in range(heads_per_group):
                head = heads_per_group * (n_groups * p + gi) + hh
                x = (slopes_ref[head] * LOG2_E) * pos
                hi = x.astype(BF16).astype(F32)
                mid = (x - hi).astype(BF16).astype(F32)
                lo = (x - hi - mid).astype(BF16).astype(F32)
                t = jnp.where(lane == bias_lane0[hh], hi, t)
                t = jnp.where(lane == bias_lane0[hh] + 1, mid, t)
                t = jnp.where(lane == bias_lane0[hh] + 2, lo, t)
            tmpl_ref[gi] = t.astype(BF16)

    rowi = lax.broadcasted_iota(jnp.int32, (bs, bs), 0)
    coli = lax.broadcasted_iota(jnp.int32, (bs, bs), 1)
    causal = coli <= rowi
    blk_id = lax.broadcasted_iota(jnp.int32, (nb, bs), 0)
    gated_rows = slice(first_gated * bs, seq)

    def prepare(gi):
        cols = slice(gi * LANES, (gi + 1) * LANES)
        k_all = k_ref[:, cols]
        for n in range(nb):
            kb = k_all[n * bs:(n + 1) * bs, :].astype(F32)
            km_ref[gi, n:n + 1, :] = jnp.sum(kb, axis=0, keepdims=True) * (1.0 / bs)
        km = km_ref[gi]
        km_hi = km.astype(BF16)
        km_lo = (km - km_hi.astype(F32)).astype(BF16)
        q_all = q_ref[:, cols]
        v_all = v_ref[:, cols]
        k_ext = jnp.concatenate([k_all, tmpl_ref[gi]], axis=-1)
        v_ext = jnp.concatenate([v_all, jnp.ones_like(v_all)], axis=-1)
        q_ext = []
        for hh in range(heads_per_group):
            qh = jnp.where(in_head[hh], q_all, jnp.zeros_like(q_all))
            gate_t = _dot_nt(km_hi, qh[gated_rows]) + _dot_nt(km_lo, qh[gated_rows])
            bias_t = []
            for j in range(first_gated, nb):
                g = gate_t[:, (j - first_gated) * bs:(j - first_gated + 1) * bs]
                beaten_by = jnp.zeros((nb, bs), F32)
                for mth in range(j):
                    gm = g[mth:mth + 1, :]
                    wins = (gm > g) | ((gm == g) & (blk_id > mth))
                    beaten_by = beaten_by + jnp.where(wins, 1.0, 0.0)
                keep = (beaten_by < float(MOBA_TOPK)) | (blk_id >= j)
                bias_t.append(jnp.where(keep, 0.0, NEG_INF))
            bias_t = jnp.concatenate(bias_t, axis=1)
            block_bias = jnp.concatenate(
                [bias_t, jnp.zeros((LANES - nb, bias_t.shape[1]), F32)], axis=0).T
            ones_lanes = jnp.where(
                (lane >= bias_lane0[hh]) & (lane < bias_lane0[hh] + 3), 1.0, 0.0)
            gated_aug = (block_bias + ones_lanes).astype(BF16)
            blocks = []
            for j in range(nb):
                aug = (gated_aug[(j - first_gated) * bs:(j - first_gated + 1) * bs]
                       if j >= first_gated
                       else jnp.broadcast_to(ones_lanes.astype(BF16), (bs, LANES)))
                blocks.append(jnp.concatenate([qh[j * bs:(j + 1) * bs], aug], axis=-1))
            q_ext.append(blocks)
        return k_ext, v_ext, q_ext

    def scores(group, j, hh):
        k_ext, _, q_ext = group
        rows = slice(j * bs, (j + 1) * bs)
        qa = q_ext[hh][j]
        s_own = jnp.where(causal, _dot_nt(qa, k_ext[rows]), NEG_INF)
        m_row = jnp.max(s_own, axis=-1, keepdims=True)
        s_past = None
        if j:
            s_past = _dot_nt(qa, k_ext[:j * bs])
            m_row = jnp.maximum(m_row, jnp.max(s_past, axis=-1, keepdims=True))
        return s_own, s_past, m_row

    def attend(group, j, s_own, s_past, m_row):
        v_ext = group[1]
        rows = slice(j * bs, (j + 1) * bs)
        acc = _dot(jnp.exp2(s_own - m_row).astype(BF16), v_ext[rows])
        if j:
            acc = acc + _dot(jnp.exp2(s_past - m_row).astype(BF16), v_ext[:j * bs])
        return acc[:, :LANES] / acc[:, LANES:]

    order = [j for pair in zip(range(nb - 1, -1, -1), range(nb)) for j in pair][:nb]
    work = [(gi, j, hh) for gi in range(n_groups) for j in order
            for hh in range(heads_per_group)]
    groups = {0: prepare(0)}
    head_out = {}
    pending = scores(groups[0], *work[0][1:])
    for i, (gi, j, hh) in enumerate(work):
        current = pending
        if i + 1 < len(work):
            nxt = work[i + 1]
            if nxt[0] not in groups:
                groups[nxt[0]] = prepare(nxt[0])
            pending = scores(groups[nxt[0]], *nxt[1:])
        head_out[hh] = attend(groups[gi], j, *current)
        if i == 0 and n_groups > 1:
            groups[1] = prepare(1)
        if hh == heads_per_group - 1:
            o_ref[j * bs:(j + 1) * bs, gi * LANES:(gi + 1) * LANES] = jnp.where(
                in_head[0], head_out[0], head_out[1]).astype(o_ref.dtype)


def _alibi_slopes(n):
    def pow2(m):
        return [2.0 ** (-8.0 * (i + 1) / m) for i in range(m)]
    p = 2 ** int(math.floor(math.log2(n)))
    s = pow2(p)
    if p < n:
        s = s + pow2(2 * p)[0::2][: n - p]
    return np.asarray(s, dtype=np.float32)


def _moba(qproj3, kv3):
    batch, seq, _ = qproj3.shape
    n_pairs = MOBA_HEADS * MOBA_HEAD_DIM // LANES
    nb = seq // MOBA_BLOCK
    slopes = jnp.asarray(_alibi_slopes(MOBA_HEADS))
    gps = MOBA_GROUPS_PER_STEP
    width = gps * LANES
    steps = n_pairs // gps
    grid_spec = pltpu.PrefetchScalarGridSpec(
        num_scalar_prefetch=1,
        grid=(steps, batch),
        in_specs=[pl.BlockSpec((None, seq, width), lambda p, b, *_: (b, 0, p)),
                  pl.BlockSpec((None, seq, width), lambda p, b, *_: (b, 0, p)),
                  pl.BlockSpec((None, seq, width), lambda p, b, *_: (b, 0, steps + p))],
        out_specs=pl.BlockSpec((None, seq, width), lambda p, b, *_: (b, 0, p)),
        scratch_shapes=[pltpu.VMEM((gps, seq, LANES), BF16), pltpu.VMEM((gps, nb, LANES), F32)],
    )
    return pl.pallas_call(
        _moba_kernel,
        grid_spec=grid_spec,
        out_shape=jax.ShapeDtypeStruct((batch, seq, n_pairs * LANES), BF16),
        compiler_params=pltpu.CompilerParams(
            dimension_semantics=("parallel", "arbitrary"),
            vmem_limit_bytes=V7X_VMEM_LIMIT_BYTES),
        name="moba",
    )(slopes, qproj3, kv3, kv3)


def kernel(x, mem, w_in_a, ret_norm_gain, w_out_a, kv_norm_gain, w_kv_shared, w_in_b, w_out_b,
           w_mem_kv, norm_pre_mix, norm_post_mix, norm_pre_mlp, norm_post_mlp, w_up, w_down):
    batch, seq, d = x.shape
    m = batch * seq
    depth = norm_pre_mix.shape[0]
    n_a = depth // 2
    qk_w = RET_HEADS * RET_QK_DIM
    v_w = RET_HEADS * RET_V_DIM
    mem_w = MEM_HEADS * MEM_HEAD_DIM
    moba_w = MOBA_HEADS * MOBA_HEAD_DIM
    row_vec = lambda g: g.reshape(1, -1)
    mem_q_scale = jnp.full((1, mem_w), MEM_HEAD_DIM ** -0.5 * LOG2_E, F32)
    moba_q_scale = jnp.full((1, moba_w), MOBA_HEAD_DIM ** -0.5 * LOG2_E, F32)
    scale_a = jnp.concatenate(
        [jnp.ones((1, 2 * qk_w + 2 * v_w), F32), mem_q_scale], axis=1)
    scale_b = jnp.concatenate([moba_q_scale, mem_q_scale], axis=1)
    scale_kv = jnp.ones((1, 2 * moba_w), F32)

    def layer_weights(l):
        w = {"mem_kv": (w_mem_kv, l), "up": (w_up, l), "down": (w_down, l)}
        if l < n_a:
            w.update({"in": (w_in_a, l), "out": (w_out_a, l)})
        else:
            w.update({"in": (w_in_b, l - n_a), "out": (w_out_b, l - n_a)})
            if l == n_a:
                w["kv"] = (w_kv_shared, None)
        return w

    def cast_outside(arr, index):
        return (arr if index is None else arr[index]).astype(BF16)

    def hosted_casts(names, weights):
        slab_ok = lambda arr: arr.shape[-2] % (m // RET_ROW_TILE * 16) == 0
        return [n for n in names if slab_ok(weights[n][0])]

    h = x.reshape(m, d)
    kv3 = None
    ready = {}
    for l in range(depth):
        f32w = layer_weights(l)
        bf16w = dict(ready)
        ready = {}
        if l < n_a:
            bf16w["in"] = cast_outside(*f32w["in"])
            todo = {("own", n): f32w[n] for n in f32w if n not in bf16w}
            if l + 1 == n_a and l + 1 < depth:
                todo.update({("next", n): w for n, w in layer_weights(l + 1).items()})
            keys = hosted_casts(list(todo), todo)
            y, q_m, cast = _proj_retention(h, norm_pre_mix[l:l + 1], bf16w["in"], scale_a,
                                           row_vec(ret_norm_gain[l]), batch, seq,
                                           [todo[k] for k in keys])
            for (where, n), w in zip(keys, cast):
                (bf16w if where == "own" else ready)[n] = w
        for n in f32w:
            if n not in bf16w:
                bf16w[n] = cast_outside(*f32w[n])
        if l < n_a:
            h = _mix_out(y, q_m, 0, mem, bf16w["mem_kv"], bf16w["out"], h,
                         row_vec(norm_post_mix[l]), batch, seq)
        else:
            if l == n_a:
                qproj, kv = _norm_proj(
                    h, jnp.stack([norm_pre_mix[l], kv_norm_gain]),
                    [bf16w["in"], bf16w["kv"]], [scale_b, scale_kv])
                kv3 = kv.reshape(batch, seq, 2 * moba_w)
            else:
                (qproj,) = _norm_proj(h, norm_pre_mix[l:l + 1], [bf16w["in"]], [scale_b])
            a = _moba(qproj.reshape(batch, seq, moba_w + mem_w), kv3)
            h = _mix_out(a.reshape(m, moba_w), qproj, moba_w // mem_w, mem,
                         bf16w["mem_kv"], bf16w["out"], h,
                         row_vec(norm_post_mix[l]), batch, seq)
        h = _mlp(h, row_vec(norm_pre_mlp[l]), row_vec(norm_post_mlp[l]),
                 bf16w["up"], bf16w["down"])
    return h.reshape(batch, seq, d)
```

```python
import functools
import math

import numpy as np
import jax
import jax.numpy as jnp
from jax import lax
from jax.experimental import pallas as pl
from jax.experimental.pallas import tpu as pltpu

F32 = jnp.float32
BF16 = jnp.bfloat16

EPS = 1e-6
NEG_INF = -1e30
LOG2_E = math.log2(math.e)

LANES = 128
V7X_VMEM_LIMIT_BYTES = 56 * 1024 * 1024

RET_HEADS = 4
RET_QK_DIM = 128
RET_V_DIM = 192
RET_CHUNK = 128
MOBA_HEADS = 12
MOBA_HEAD_DIM = 64
MOBA_BLOCK = 256
MOBA_TOPK = 3
MOBA_GROUPS_PER_STEP = 2
MEM_HEADS = 4
MEM_HEAD_DIM = 64

ROW_TILE = 1024
DENSE_SUB_ROWS = 512
RET_ROW_TILE = 1024
RET_SUB_ROWS = 256
PROJ_PIECE_COLS = 256
MIX_ROW_TILE = 1024
MIX_SUB_ROWS = 256
FF_CHUNK = 1024


def _rms_scale(x):
    return lax.rsqrt(jnp.mean(x * x, axis=-1, keepdims=True) + EPS)


def _dot(a, b):
    return jnp.dot(a, b, preferred_element_type=F32)


def _dot_nt(a, b):
    return lax.dot_general(a, b, (((1,), (1,)), ((), ())), preferred_element_type=F32)


def _dot_tn(a, b):
    return lax.dot_general(a, b, (((0,), (0,)), ((), ())), preferred_element_type=F32)


def _resident(shape):
    zeros = (0,) * len(shape)
    return pl.BlockSpec(shape, lambda *_: zeros, pipeline_mode=pl.Buffered(1))


def _cast_slab_specs(weights, n_steps, linear_step):
    in_specs, out_specs = [], []
    for arr, layer in weights:
        rows, cols = arr.shape[-2] // n_steps, arr.shape[-1]
        out_specs.append(pl.BlockSpec((rows, cols), lambda *idx: (linear_step(*idx), 0)))
        if layer is None:
            in_specs.append(out_specs[-1])
        else:
            in_specs.append(pl.BlockSpec(
                (None, rows, cols), lambda *idx, _l=layer: (_l, linear_step(*idx), 0)))
    return in_specs, out_specs


def _cast_slabs(src_refs, dst_refs):
    for src, dst in zip(src_refs, dst_refs):
        dst[...] = src[...].astype(dst.dtype)


def _norm_proj_kernel(x_ref, g_ref, *refs, n_out):
    w_refs, s_refs, o_refs = refs[:n_out], refs[n_out:2 * n_out], refs[2 * n_out:]
    sub = DENSE_SUB_ROWS
    n_sub = x_ref.shape[0] // sub

    def normed(r):
        x = x_ref[r * sub:(r + 1) * sub, :]
        return x * _rms_scale(x)

    xhat, xhat_next = normed(0), None
    for r in range(n_sub):
        rows = slice(r * sub, (r + 1) * sub)
        for i in range(n_out):
            hn = (xhat * g_ref[i:i + 1, :]).astype(BF16)
            out = _dot(hn, w_refs[i][...])
            if i == 0 and r + 1 < n_sub:
                xhat_next = normed(r + 1)
            o_refs[i][rows, :] = (out * s_refs[i][...]).astype(o_refs[i].dtype)
        xhat = xhat_next


def _norm_proj(x2d, gains, weights, col_scales):
    m, d = x2d.shape
    n_out = len(weights)
    return pl.pallas_call(
        functools.partial(_norm_proj_kernel, n_out=n_out),
        grid=(m // ROW_TILE,),
        in_specs=[pl.BlockSpec((ROW_TILE, d), lambda i: (i, 0)),
                  _resident(gains.shape)]
                 + [_resident(w.shape) for w in weights]
                 + [_resident(s.shape) for s in col_scales],
        out_specs=[pl.BlockSpec((ROW_TILE, w.shape[1]), lambda i: (i, 0)) for w in weights],
        out_shape=[jax.ShapeDtypeStruct((m, w.shape[1]), BF16) for w in weights],
        compiler_params=pltpu.CompilerParams(
            dimension_semantics=("parallel",), vmem_limit_bytes=V7X_VMEM_LIMIT_BYTES),
        name="norm_proj",
    )(x2d, gains, *weights, *col_scales)


def _mlp_kernel(h_ref, g_pre_ref, g_post_ref, wu_ref, wd_ref, o_ref):
    sub = DENSE_SUB_ROWS
    n_sub = h_ref.shape[0] // sub
    d_ff = wu_ref.shape[1]

    def normed(r):
        h = h_ref[r * sub:(r + 1) * sub, :]
        return (h * _rms_scale(h) * g_pre_ref[...]).astype(BF16)

    def epilogue(r, y):
        rows = slice(r * sub, (r + 1) * sub)
        o_ref[rows, :] = h_ref[rows, :] + y * _rms_scale(y) * g_post_ref[...]

    hn, hn_next, finished = normed(0), None, None
    for r in range(n_sub):
        y = None
        for c in range(d_ff // FF_CHUNK):
            cols = slice(c * FF_CHUNK, (c + 1) * FF_CHUNK)
            u = _dot(hn, wu_ref[:, cols])
            if c == 0:
                if r + 1 < n_sub:
                    hn_next = normed(r + 1)
                if finished is not None:
                    epilogue(*finished)
            a = jnp.square(jnp.maximum(u, 0.0)).astype(BF16)
            d = _dot(a, wd_ref[cols, :])
            y = d if y is None else y + d
        finished, hn = (r, y), hn_next
    epilogue(*finished)


def _mlp(h2d, g_pre, g_post, w_up, w_down):
    m, d = h2d.shape
    row_spec = pl.BlockSpec((ROW_TILE, d), lambda i: (i, 0))
    return pl.pallas_call(
        _mlp_kernel,
        grid=(m // ROW_TILE,),
        in_specs=[row_spec, _resident(g_pre.shape), _resident(g_post.shape),
                  _resident(w_up.shape), _resident(w_down.shape)],
        out_specs=row_spec,
        out_shape=jax.ShapeDtypeStruct((m, d), F32),
        compiler_params=pltpu.CompilerParams(
            dimension_semantics=("parallel",), vmem_limit_bytes=V7X_VMEM_LIMIT_BYTES),
        name="mlp",
    )(h2d, g_pre, g_post, w_up, w_down)


def _proj_retention_kernel(x_ref, g_pre_ref, w_ref, scale_ref, gain_ref, *refs, n_cast):
    y_ref, qm_ref = refs[n_cast:n_cast + 2]
    state_ref = refs[-1]
    _cast_slabs(refs[:n_cast], refs[n_cast + 2:-1])
    c = RET_CHUNK
    sub = RET_SUB_ROWS
    qk_w = RET_HEADS * RET_QK_DIM
    v_w = RET_HEADS * RET_V_DIM

    @pl.when(pl.program_id(1) == 0)
    def _():
        state_ref[...] = jnp.zeros(state_ref.shape, F32)

    row = lax.broadcasted_iota(jnp.int32, (c, c), 0)
    col = lax.broadcasted_iota(jnp.int32, (c, c), 1)
    rel = (row - col).astype(F32)
    pos = lax.broadcasted_iota(jnp.int32, (c, 1), 0).astype(F32)
    qk_scale = RET_QK_DIM ** -0.5

    tables = []
    for h in range(RET_HEADS):
        log_g = float(np.log1p(-np.exp2(np.float32(-5.0 - h))))
        d_intra = jnp.where(rel >= 0, jnp.exp(log_g * jnp.maximum(rel, 0.0)), 0.0) * qk_scale
        zeta = jnp.exp(log_g * (c - 1.0 - pos)) * qk_scale
        xi = jnp.exp(log_g * (pos + 1.0))
        chunk_decay = float(np.exp(np.float32(log_g * c)))
        tables.append((d_intra, zeta, xi, chunk_decay))

    n_pieces = w_ref.shape[1] // PROJ_PIECE_COLS

    def project_pieces(r):
        x = x_ref[r * sub:(r + 1) * sub, :]
        hn = (x * _rms_scale(x) * g_pre_ref[...]).astype(BF16)

        def piece(n):
            cols = slice(n * PROJ_PIECE_COLS, (n + 1) * PROJ_PIECE_COLS)
            return (_dot(hn, w_ref[:, cols]) * scale_ref[:, cols]).astype(BF16)
        return [functools.partial(piece, n) for n in range(n_pieces)]

    def retain(r, proj, next_pieces):
        qm_ref[r * sub:(r + 1) * sub, :] = proj[:, 2 * qk_w + 2 * v_w:]
        bodies = [(ci, h) for ci in range(sub // c) for h in range(RET_HEADS)]
        done = []

        def qk_scores(ci, h):
            rows = slice(ci * c, (ci + 1) * c)
            q = proj[rows, h * RET_QK_DIM:(h + 1) * RET_QK_DIM]
            k = proj[rows, qk_w + h * RET_QK_DIM:qk_w + (h + 1) * RET_QK_DIM]
            return q, k, _dot_nt(q, k)

        pending = qk_scores(*bodies[0])
        for i, (ci, h) in enumerate(bodies):
            while len(done) < (i + 1) * len(next_pieces) // len(bodies):
                done.append(next_pieces[len(done)]())
            q, k, qk = pending
            if i + 1 < len(bodies):
                pending = qk_scores(*bodies[i + 1])
            rows = slice(ci * c, (ci + 1) * c)
            out_rows = slice(r * sub + ci * c, r * sub + (ci + 1) * c)
            d_intra, zeta, xi, chunk_decay = tables[h]
            v_cols = slice(h * RET_V_DIM, (h + 1) * RET_V_DIM)
            v = proj[rows, 2 * qk_w + h * RET_V_DIM:2 * qk_w + (h + 1) * RET_V_DIM]
            g = proj[rows, 2 * qk_w + v_w + h * RET_V_DIM:
                     2 * qk_w + v_w + (h + 1) * RET_V_DIM].astype(F32)
            state = state_ref[h]
            s = qk * d_intra
            y = _dot(s.astype(BF16), v) + _dot(q, state.astype(BF16)) * xi
            kz = (k.astype(F32) * zeta).astype(BF16)
            state_ref[h] = state * chunk_decay + _dot_tn(kz, v)
            ms = jnp.mean(y * y, axis=-1, keepdims=True)
            yn = y * lax.rsqrt(ms + EPS) * gain_ref[:, v_cols]
            y_ref[out_rows, v_cols] = (g / (1.0 + jnp.exp2(g * -LOG2_E)) * yn).astype(
                y_ref.dtype)
        return done

    n_groups = x_ref.shape[0] // sub
    proj = jnp.concatenate([piece() for piece in project_pieces(0)], axis=1)
    for r in range(n_groups):
        next_pieces = project_pieces(r + 1) if r + 1 < n_groups else []
        done = retain(r, proj, next_pieces)
        if done:
            proj = jnp.concatenate(done, axis=1)


def _proj_retention(x2d, g_pre, w_in, col_scale, gn_gain, batch, seq, casts):
    m, d = x2d.shape
    steps = seq // RET_ROW_TILE
    v_w = RET_HEADS * RET_V_DIM
    mem_w = MEM_HEADS * MEM_HEAD_DIM
    linear_step = lambda b, s: b * steps + s
    row_block = lambda width: pl.BlockSpec((RET_ROW_TILE, width),
                                           lambda b, s: (linear_step(b, s), 0))
    cast_in, cast_out = _cast_slab_specs(casts, batch * steps, linear_step)
    outs = pl.pallas_call(
        functools.partial(_proj_retention_kernel, n_cast=len(casts)),
        grid=(batch, steps),
        in_specs=[row_block(d), _resident(g_pre.shape), _resident(w_in.shape),
                  _resident(col_scale.shape), _resident(gn_gain.shape)] + cast_in,
        out_specs=[row_block(v_w), row_block(mem_w)] + cast_out,
        out_shape=[jax.ShapeDtypeStruct((m, v_w), BF16), jax.ShapeDtypeStruct((m, mem_w), BF16)]
                  + [jax.ShapeDtypeStruct(w.shape[-2:], BF16) for w, _ in casts],
        scratch_shapes=[pltpu.VMEM((RET_HEADS, RET_QK_DIM, RET_V_DIM), F32)],
        compiler_params=pltpu.CompilerParams(
            dimension_semantics=("parallel", "arbitrary"),
            vmem_limit_bytes=V7X_VMEM_LIMIT_BYTES),
        name="proj_retention",
    )(x2d, g_pre, w_in, col_scale, gn_gain, *[w for w, _ in casts])
    return outs[0], outs[1], outs[2:]


def _mix_out_kernel(a_ref, qm_ref, mem_ref, wmkv_ref, wout_ref, h_ref, gain_ref, o_ref,
                    mk_ref, mv_ref):
    mem_w = MEM_HEADS * MEM_HEAD_DIM
    groups = mem_w // LANES
    heads_per_group = LANES // MEM_HEAD_DIM
    lane = lax.broadcasted_iota(jnp.int32, (1, LANES), 1)
    head_lanes = [(lane >= hh * MEM_HEAD_DIM) & (lane < (hh + 1) * MEM_HEAD_DIM)
                  for hh in range(heads_per_group)]

    @pl.when(pl.program_id(1) == 0)
    def _():
        mkv = _dot(mem_ref[...].astype(BF16), wmkv_ref[...])
        mk_ref[...] = mkv[:, :mem_w].astype(BF16)
        for p in range(groups):
            mvp = mkv[:, mem_w + p * LANES:mem_w + (p + 1) * LANES].astype(BF16)
            mv_ref[p] = jnp.concatenate([mvp, jnp.ones_like(mvp)], axis=-1)

    a_w = a_ref.shape[1]
    sub = MIX_SUB_ROWS

    n_sub = a_ref.shape[0] // sub

    def scores(p, hh):
        cols = slice(p * LANES, (p + 1) * LANES)
        qp = qm_ref[:, cols]
        qh = jnp.where(head_lanes[hh], qp, jnp.zeros_like(qp))
        s = _dot_nt(qh, mk_ref[:, cols])
        return s, jnp.max(s, axis=-1, keepdims=True)

    def attend(p, hh, s, m_row):
        acc = _dot(jnp.exp2(s - m_row).astype(BF16), mv_ref[p])
        return acc[:, :LANES] / acc[:, LANES:]

    def project_a(r):
        return _dot(a_ref[r * sub:(r + 1) * sub, :], wout_ref[:a_w, :])

    def epilogue(r, mix):
        rows = slice(r * sub, (r + 1) * sub)
        o_ref[rows, :] = h_ref[rows, :] + mix * _rms_scale(mix) * gain_ref[...]

    work = [(p, hh) for p in range(groups) for hh in range(heads_per_group)]
    outs, mix_a = {}, {}
    pending = scores(*work[0])
    for i, (p, hh) in enumerate(work):
        current = pending
        if i + 1 < len(work):
            pending = scores(*work[i + 1])
        if i < n_sub // 2:
            mix_a[i] = project_a(i)
        outs[p, hh] = attend(p, hh, *current)
    mem_out = jnp.concatenate(
        [jnp.where(head_lanes[0], outs[p, 0], outs[p, 1]).astype(BF16) for p in range(groups)],
        axis=-1)
    projected = []
    for r in range(n_sub):
        if r not in mix_a:
            mix_a[r] = project_a(r)
            while projected:
                epilogue(*projected.pop(0))
        mix = mix_a.pop(r) + _dot(mem_out[r * sub:(r + 1) * sub], wout_ref[a_w:, :])
        projected.append((r, mix))
    while projected:
        epilogue(*projected.pop(0))


def _mix_out(a2d, qm_src, qm_col_block, mem, w_mkv, w_out, h2d, gain, batch, seq):
    m, d = h2d.shape
    a_w = a2d.shape[1]
    mem_w = MEM_HEADS * MEM_HEAD_DIM
    mem_len = mem.shape[1]
    steps = seq // MIX_ROW_TILE
    row = lambda b, s: b * steps + s
    return pl.pallas_call(
        _mix_out_kernel,
        grid=(batch, steps),
        in_specs=[pl.BlockSpec((MIX_ROW_TILE, a_w), lambda b, s: (row(b, s), 0)),
                  pl.BlockSpec((MIX_ROW_TILE, mem_w), lambda b, s: (row(b, s), qm_col_block)),
                  pl.BlockSpec((None, mem_len, d), lambda b, s: (b, 0, 0)),
                  _resident(w_mkv.shape), _resident(w_out.shape),
                  pl.BlockSpec((MIX_ROW_TILE, d), lambda b, s: (row(b, s), 0)),
                  _resident(gain.shape)],
        out_specs=pl.BlockSpec((MIX_ROW_TILE, d), lambda b, s: (row(b, s), 0)),
        out_shape=jax.ShapeDtypeStruct((m, d), F32),
        scratch_shapes=[pltpu.VMEM((mem_len, mem_w), BF16),
                        pltpu.VMEM((mem_w // LANES, mem_len, 2 * LANES), BF16)],
        compiler_params=pltpu.CompilerParams(
            dimension_semantics=("parallel", "arbitrary"),
            vmem_limit_bytes=V7X_VMEM_LIMIT_BYTES),
        name="mix_out",
    )(a2d, qm_src, mem, w_mkv, w_out, h2d, gain)


def _moba_kernel(slopes_ref, q_ref, k_ref, v_ref, o_ref, tmpl_ref, km_ref):
    seq = k_ref.shape[0]
    bs = MOBA_BLOCK
    nb = seq // bs
    n_groups = k_ref.shape[1] // LANES
    heads_per_group = LANES // MOBA_HEAD_DIM
    first_gated = MOBA_TOPK + 1
    p = pl.program_id(0)
    lane = lax.broadcasted_iota(jnp.int32, (1, LANES), 1)
    bias_lane0 = [nb + 3 * hh for hh in range(heads_per_group)]
    in_head = [(lane >= hh * MOBA_HEAD_DIM) & (lane < (hh + 1) * MOBA_HEAD_DIM)
               for hh in range(heads_per_group)]

    @pl.when(pl.program_id(1) == 0)
    def _():
        posi = lax.broadcasted_iota(jnp.int32, (seq, 1), 0)
        pos = posi.astype(F32)
        for gi in range(n_groups):
            t = jnp.where(lane == posi // bs, 1.0, 0.0)
            for hh in range(heads_per_group):
                head = heads_per_group * (n_groups * p + gi) + hh
                x = (slopes_ref[head] * LOG2_E) * pos
                hi = x.astype(BF16).astype(F32)
                mid = (x - hi).astype(BF16).astype(F32)
                lo = (x - hi - mid).astype(BF16).astype(F32)
                t = jnp.where(lane == bias_lane0[hh], hi, t)
                t = jnp.where(lane == bias_lane0[hh] + 1, mid, t)
                t = jnp.where(lane == bias_lane0[hh] + 2, lo, t)
            tmpl_ref[gi] = t.astype(BF16)

    rowi = lax.broadcasted_iota(jnp.int32, (bs, bs), 0)
    coli = lax.broadcasted_iota(jnp.int32, (bs, bs), 1)
    causal = coli <= rowi
    blk_id = lax.broadcasted_iota(jnp.int32, (nb, bs), 0)
    gated_rows = slice(first_gated * bs, seq)

    def prepare(gi):
        cols = slice(gi * LANES, (gi + 1) * LANES)
        k_all = k_ref[:, cols]
        for n in range(nb):
            kb = k_all[n * bs:(n + 1) * bs, :].astype(F32)
            km_ref[gi, n:n + 1, :] = jnp.sum(kb, axis=0, keepdims=True) * (1.0 / bs)
        km = km_ref[gi]
        km_hi = km.astype(BF16)
        km_lo = (km - km_hi.astype(F32)).astype(BF16)
        q_all = q_ref[:, cols]
        v_all = v_ref[:, cols]
        k_ext = jnp.concatenate([k_all, tmpl_ref[gi]], axis=-1)
        v_ext = jnp.concatenate([v_all, jnp.ones_like(v_all)], axis=-1)
        q_ext = []
        for hh in range(heads_per_group):
            qh = jnp.where(in_head[hh], q_all, jnp.zeros_like(q_all))
            gate_t = _dot_nt(km_hi, qh[gated_rows]) + _dot_nt(km_lo, qh[gated_rows])
            bias_t = []
            for j in range(first_gated, nb):
                g = gate_t[:, (j - first_gated) * bs:(j - first_gated + 1) * bs]
                beaten_by = jnp.zeros((nb, bs), F32)
                for mth in range(j):
                    gm = g[mth:mth + 1, :]
                    wins = (gm > g) | ((gm == g) & (blk_id > mth))
                    beaten_by = beaten_by + jnp.where(wins, 1.0, 0.0)
                keep = (beaten_by < float(MOBA_TOPK)) | (blk_id >= j)
                bias_t.append(jnp.where(keep, 0.0, NEG_INF))
            bias_t = jnp.concatenate(bias_t, axis=1)
            block_bias = jnp.concatenate(
                [bias_t, jnp.zeros((LANES - nb, bias_t.shape[1]), F32)], axis=0).T
            ones_lanes = jnp.where(
                (lane >= bias_lane0[hh]) & (lane < bias_lane0[hh] + 3), 1.0, 0.0)
            gated_aug = (block_bias + ones_lanes).astype(BF16)
            blocks = []
            for j in range(nb):
                aug = (gated_aug[(j - first_gated) * bs:(j - first_gated + 1) * bs]
                       if j >= first_gated
                       else jnp.broadcast_to(ones_lanes.astype(BF16), (bs, LANES)))
                blocks.append(jnp.concatenate([qh[j * bs:(j + 1) * bs], aug], axis=-1))
            q_ext.append(blocks)
        return k_ext, v_ext, q_ext

    def scores(group, j, hh):
        k_ext, _, q_ext = group
        rows = slice(j * bs, (j + 1) * bs)
        qa = q_ext[hh][j]
        s_own = jnp.where(causal, _dot_nt(qa, k_ext[rows]), NEG_INF)
        m_row = jnp.max(s_own, axis=-1, keepdims=True)
        s_past = None
        if j:
            s_past = _dot_nt(qa, k_ext[:j * bs])
            m_row = jnp.maximum(m_row, jnp.max(s_past, axis=-1, keepdims=True))
        return s_own, s_past, m_row

    def attend(group, j, s_own, s_past, m_row):
        v_ext = group[1]
        rows = slice(j * bs, (j + 1) * bs)
        acc = _dot(jnp.exp2(s_own - m_row).astype(BF16), v_ext[rows])
        if j:
            acc = acc + _dot(jnp.exp2(s_past - m_row).astype(BF16), v_ext[:j * bs])
        return acc[:, :LANES] / acc[:, LANES:]

    order = [j for pair in zip(range(nb - 1, -1, -1), range(nb)) for j in pair][:nb]
    work = [(gi, j, hh) for gi in range(n_groups) for j in order
            for hh in range(heads_per_group)]
    groups = {0: prepare(0)}
    head_out = {}
    pending = scores(groups[0], *work[0][1:])
    for i, (gi, j, hh) in enumerate(work):
        current = pending
        if i + 1 < len(work):
            nxt = work[i + 1]
            if nxt[0] not in groups:
                groups[nxt[0]] = prepare(nxt[0])
            pending = scores(groups[nxt[0]], *nxt[1:])
        head_out[hh] = attend(groups[gi], j, *current)
        if i == 0 and n_groups > 1:
            groups[1] = prepare(1)
        if hh == heads_per_group - 1:
            o_ref[j * bs:(j + 1) * bs, gi * LANES:(gi + 1) * LANES] = jnp.where(
                in_head[0], head_out[0], head_out[1]).astype(o_ref.dtype)


def _alibi_slopes(n):
    def pow2(m):
        return [2.0 ** (-8.0 * (i + 1) / m) for i in range(m)]
    p = 2 ** int(math.floor(math.log2(n)))
    s = pow2(p)
    if p < n:
        s = s + pow2(2 * p)[0::2][: n - p]
    return np.asarray(s, dtype=np.float32)


def _moba(qproj3, kv3):
    batch, seq, _ = qproj3.shape
    n_pairs = MOBA_HEADS * MOBA_HEAD_DIM // LANES
    nb = seq // MOBA_BLOCK
    slopes = jnp.asarray(_alibi_slopes(MOBA_HEADS))
    gps = MOBA_GROUPS_PER_STEP
    width = gps * LANES
    steps = n_pairs // gps
    grid_spec = pltpu.PrefetchScalarGridSpec(
        num_scalar_prefetch=1,
        grid=(steps, batch),
        in_specs=[pl.BlockSpec((None, seq, width), lambda p, b, *_: (b, 0, p)),
                  pl.BlockSpec((None, seq, width), lambda p, b, *_: (b, 0, p)),
                  pl.BlockSpec((None, seq, width), lambda p, b, *_: (b, 0, steps + p))],
        out_specs=pl.BlockSpec((None, seq, width), lambda p, b, *_: (b, 0, p)),
        scratch_shapes=[pltpu.VMEM((gps, seq, LANES), BF16), pltpu.VMEM((gps, nb, LANES), F32)],
    )
    return pl.pallas_call(
        _moba_kernel,
        grid_spec=grid_spec,
        out_shape=jax.ShapeDtypeStruct((batch, seq, n_pairs * LANES), BF16),
        compiler_params=pltpu.CompilerParams(
            dimension_semantics=("parallel", "arbitrary"),
            vmem_limit_bytes=V7X_VMEM_LIMIT_BYTES),
        name="moba",
    )(slopes, qproj3, kv3, kv3)


def kernel(x, mem, w_in_a, ret_norm_gain, w_out_a, kv_norm_gain, w_kv_shared, w_in_b, w_out_b,
           w_mem_kv, norm_pre_mix, norm_post_mix, norm_pre_mlp, norm_post_mlp, w_up, w_down):
    batch, seq, d = x.shape
    m = batch * seq
    depth = norm_pre_mix.shape[0]
    n_a = depth // 2
    qk_w = RET_HEADS * RET_QK_DIM
    v_w = RET_HEADS * RET_V_DIM
    mem_w = MEM_HEADS * MEM_HEAD_DIM
    moba_w = MOBA_HEADS * MOBA_HEAD_DIM
    row_vec = lambda g: g.reshape(1, -1)
    mem_q_scale = jnp.full((1, mem_w), MEM_HEAD_DIM ** -0.5 * LOG2_E, F32)
    moba_q_scale = jnp.full((1, moba_w), MOBA_HEAD_DIM ** -0.5 * LOG2_E, F32)
    scale_a = jnp.concatenate(
        [jnp.ones((1, 2 * qk_w + 2 * v_w), F32), mem_q_scale], axis=1)
    scale_b = jnp.concatenate([moba_q_scale, mem_q_scale], axis=1)
    scale_kv = jnp.ones((1, 2 * moba_w), F32)

    def layer_weights(l):
        w = {"mem_kv": (w_mem_kv, l), "up": (w_up, l), "down": (w_down, l)}
        if l < n_a:
            w.update({"in": (w_in_a, l), "out": (w_out_a, l)})
        else:
            w.update({"in": (w_in_b, l - n_a), "out": (w_out_b, l - n_a)})
            if l == n_a:
                w["kv"] = (w_kv_shared, None)
        return w

    def cast_outside(arr, index):
        return (arr if index is None else arr[index]).astype(BF16)

    def hosted_casts(names, weights):
        slab_ok = lambda arr: arr.shape[-2] % (m // RET_ROW_TILE * 16) == 0
        return [n for n in names if slab_ok(weights[n][0])]

    h = x.reshape(m, d)
    kv3 = None
    ready = {}
    for l in range(depth):
        f32w = layer_weights(l)
        bf16w = dict(ready)
        ready = {}
        if l < n_a:
            bf16w["in"] = cast_outside(*f32w["in"])
            todo = {("own", n): f32w[n] for n in f32w if n not in bf16w}
            if l + 1 == n_a and l + 1 < depth:
                todo.update({("next", n): w for n, w in layer_weights(l + 1).items()
                             if n not in ("in", "kv")})
            keys = hosted_casts(list(todo), todo)
            y, q_m, cast = _proj_retention(h, norm_pre_mix[l:l + 1], bf16w["in"], scale_a,
                                           row_vec(ret_norm_gain[l]), batch, seq,
                                           [todo[k] for k in keys])
            for (where, n), w in zip(keys, cast):
                (bf16w if where == "own" else ready)[n] = w
        for n in f32w:
            if n not in bf16w:
                bf16w[n] = cast_outside(*f32w[n])
        if l < n_a:
            h = _mix_out(y, q_m, 0, mem, bf16w["mem_kv"], bf16w["out"], h,
                         row_vec(norm_post_mix[l]), batch, seq)
        else:
            if l == n_a:
                qproj, kv = _norm_proj(
                    h, jnp.stack([norm_pre_mix[l], kv_norm_gain]),
                    [bf16w["in"], bf16w["kv"]], [scale_b, scale_kv])
                kv3 = kv.reshape(batch, seq, 2 * moba_w)
            else:
                (qproj,) = _norm_proj(h, norm_pre_mix[l:l + 1], [bf16w["in"]], [scale_b])
            a = _moba(qproj.reshape(batch, seq, moba_w + mem_w), kv3)
            h = _mix_out(a.reshape(m, moba_w), qproj, moba_w // mem_w, mem,
                         bf16w["mem_kv"], bf16w["out"], h,
                         row_vec(norm_post_mix[l]), batch, seq)
        h = _mlp(h, row_vec(norm_pre_mlp[l]), row_vec(norm_post_mlp[l]),
                 bf16w["up"], bf16w["down"])
    return h.reshape(batch, seq, d)
```

```python
import functools
import math

import numpy as np
import jax
import jax.numpy as jnp
from jax import lax
from jax.experimental import pallas as pl
from jax.experimental.pallas import tpu as pltpu

F32 = jnp.float32
BF16 = jnp.bfloat16

EPS = 1e-6
NEG_INF = -1e30
LOG2_E = math.log2(math.e)

LANES = 128
V7X_VMEM_LIMIT_BYTES = 56 * 1024 * 1024

RET_HEADS = 4
RET_QK_DIM = 128
RET_V_DIM = 192
RET_CHUNK = 128
MOBA_HEADS = 12
MOBA_HEAD_DIM = 64
MOBA_BLOCK = 256
MOBA_TOPK = 3
MOBA_GROUPS_PER_STEP = 3
MEM_HEADS = 4
MEM_HEAD_DIM = 64

ROW_TILE = 1024
DENSE_SUB_ROWS = 512
RET_ROW_TILE = 1024
RET_SUB_ROWS = 256
PROJ_PIECE_COLS = 256
MIX_ROW_TILE = 1024
MIX_SUB_ROWS = 256
FF_CHUNK = 1024


def _rms_scale(x):
    return lax.rsqrt(jnp.mean(x * x, axis=-1, keepdims=True) + EPS)


def _dot(a, b):
    return jnp.dot(a, b, preferred_element_type=F32)


def _dot_nt(a, b):
    return lax.dot_general(a, b, (((1,), (1,)), ((), ())), preferred_element_type=F32)


def _dot_tn(a, b):
    return lax.dot_general(a, b, (((0,), (0,)), ((), ())), preferred_element_type=F32)


def _resident(shape):
    zeros = (0,) * len(shape)
    return pl.BlockSpec(shape, lambda *_: zeros, pipeline_mode=pl.Buffered(1))


def _cast_slab_specs(weights, n_steps, linear_step):
    in_specs, out_specs = [], []
    for arr, layer in weights:
        rows, cols = arr.shape[-2] // n_steps, arr.shape[-1]
        out_specs.append(pl.BlockSpec((rows, cols), lambda *idx: (linear_step(*idx), 0)))
        if layer is None:
            in_specs.append(out_specs[-1])
        else:
            in_specs.append(pl.BlockSpec(
                (None, rows, cols), lambda *idx, _l=layer: (_l, linear_step(*idx), 0)))
    return in_specs, out_specs


def _cast_slabs(src_refs, dst_refs):
    for src, dst in zip(src_refs, dst_refs):
        dst[...] = src[...].astype(dst.dtype)


def _norm_proj_kernel(x_ref, g_ref, *refs, n_out):
    w_refs, s_refs, o_refs = refs[:n_out], refs[n_out:2 * n_out], refs[2 * n_out:]
    sub = DENSE_SUB_ROWS
    n_sub = x_ref.shape[0] // sub

    def normed(r):
        x = x_ref[r * sub:(r + 1) * sub, :]
        return x * _rms_scale(x)

    xhat, xhat_next = normed(0), None
    for r in range(n_sub):
        rows = slice(r * sub, (r + 1) * sub)
        for i in range(n_out):
            hn = (xhat * g_ref[i:i + 1, :]).astype(BF16)
            out = _dot(hn, w_refs[i][...])
            if i == 0 and r + 1 < n_sub:
                xhat_next = normed(r + 1)
            o_refs[i][rows, :] = (out * s_refs[i][...]).astype(o_refs[i].dtype)
        xhat = xhat_next


def _norm_proj(x2d, gains, weights, col_scales):
    m, d = x2d.shape
    n_out = len(weights)
    return pl.pallas_call(
        functools.partial(_norm_proj_kernel, n_out=n_out),
        grid=(m // ROW_TILE,),
        in_specs=[pl.BlockSpec((ROW_TILE, d), lambda i: (i, 0)),
                  _resident(gains.shape)]
                 + [_resident(w.shape) for w in weights]
                 + [_resident(s.shape) for s in col_scales],
        out_specs=[pl.BlockSpec((ROW_TILE, w.shape[1]), lambda i: (i, 0)) for w in weights],
        out_shape=[jax.ShapeDtypeStruct((m, w.shape[1]), BF16) for w in weights],
        compiler_params=pltpu.CompilerParams(
            dimension_semantics=("parallel",), vmem_limit_bytes=V7X_VMEM_LIMIT_BYTES),
        name="norm_proj",
    )(x2d, gains, *weights, *col_scales)


def _mlp_kernel(h_ref, g_pre_ref, g_post_ref, wu_ref, wd_ref, o_ref):
    sub = DENSE_SUB_ROWS
    n_sub = h_ref.shape[0] // sub
    d_ff = wu_ref.shape[1]

    def normed(r):
        h = h_ref[r * sub:(r + 1) * sub, :]
        return (h * _rms_scale(h) * g_pre_ref[...]).astype(BF16)

    def epilogue(r, y):
        rows = slice(r * sub, (r + 1) * sub)
        o_ref[rows, :] = h_ref[rows, :] + y * _rms_scale(y) * g_post_ref[...]

    hn, hn_next, finished = normed(0), None, None
    for r in range(n_sub):
        y = None
        for c in range(d_ff // FF_CHUNK):
            cols = slice(c * FF_CHUNK, (c + 1) * FF_CHUNK)
            u = _dot(hn, wu_ref[:, cols])
            if c == 0:
                if r + 1 < n_sub:
                    hn_next = normed(r + 1)
                if finished is not None:
                    epilogue(*finished)
            a = jnp.square(jnp.maximum(u, 0.0)).astype(BF16)
            d = _dot(a, wd_ref[cols, :])
            y = d if y is None else y + d
        finished, hn = (r, y), hn_next
    epilogue(*finished)


def _mlp(h2d, g_pre, g_post, w_up, w_down):
    m, d = h2d.shape
    row_spec = pl.BlockSpec((ROW_TILE, d), lambda i: (i, 0))
    return pl.pallas_call(
        _mlp_kernel,
        grid=(m // ROW_TILE,),
        in_specs=[row_spec, _resident(g_pre.shape), _resident(g_post.shape),
                  _resident(w_up.shape), _resident(w_down.shape)],
        out_specs=row_spec,
        out_shape=jax.ShapeDtypeStruct((m, d), F32),
        compiler_params=pltpu.CompilerParams(
            dimension_semantics=("parallel",), vmem_limit_bytes=V7X_VMEM_LIMIT_BYTES),
        name="mlp",
    )(h2d, g_pre, g_post, w_up, w_down)


def _proj_retention_kernel(x_ref, g_pre_ref, w_ref, scale_ref, gain_ref, *refs, n_cast):
    y_ref, qm_ref = refs[n_cast:n_cast + 2]
    state_ref = refs[-1]
    _cast_slabs(refs[:n_cast], refs[n_cast + 2:-1])
    c = RET_CHUNK
    sub = RET_SUB_ROWS
    qk_w = RET_HEADS * RET_QK_DIM
    v_w = RET_HEADS * RET_V_DIM

    @pl.when(pl.program_id(1) == 0)
    def _():
        state_ref[...] = jnp.zeros(state_ref.shape, F32)

    row = lax.broadcasted_iota(jnp.int32, (c, c), 0)
    col = lax.broadcasted_iota(jnp.int32, (c, c), 1)
    rel = (row - col).astype(F32)
    pos = lax.broadcasted_iota(jnp.int32, (c, 1), 0).astype(F32)
    qk_scale = RET_QK_DIM ** -0.5

    tables = []
    for h in range(RET_HEADS):
        log_g = float(np.log1p(-np.exp2(np.float32(-5.0 - h))))
        d_intra = jnp.where(rel >= 0, jnp.exp(log_g * jnp.maximum(rel, 0.0)), 0.0) * qk_scale
        zeta = jnp.exp(log_g * (c - 1.0 - pos)) * qk_scale
        xi = jnp.exp(log_g * (pos + 1.0))
        chunk_decay = float(np.exp(np.float32(log_g * c)))
        tables.append((d_intra, zeta, xi, chunk_decay))

    n_pieces = w_ref.shape[1] // PROJ_PIECE_COLS

    def project_pieces(r):
        x = x_ref[r * sub:(r + 1) * sub, :]
        hn = (x * _rms_scale(x) * g_pre_ref[...]).astype(BF16)

        def piece(n):
            cols = slice(n * PROJ_PIECE_COLS, (n + 1) * PROJ_PIECE_COLS)
            return (_dot(hn, w_ref[:, cols]) * scale_ref[:, cols]).astype(BF16)
        return [functools.partial(piece, n) for n in range(n_pieces)]

    def retain(r, proj, next_pieces):
        qm_ref[r * sub:(r + 1) * sub, :] = proj[:, 2 * qk_w + 2 * v_w:]
        bodies = [(ci, h) for ci in range(sub // c) for h in range(RET_HEADS)]
        done = []

        def qk_scores(ci, h):
            rows = slice(ci * c, (ci + 1) * c)
            q = proj[rows, h * RET_QK_DIM:(h + 1) * RET_QK_DIM]
            k = proj[rows, qk_w + h * RET_QK_DIM:qk_w + (h + 1) * RET_QK_DIM]
            return q, k, _dot_nt(q, k)

        pending = qk_scores(*bodies[0])
        for i, (ci, h) in enumerate(bodies):
            while len(done) < (i + 1) * len(next_pieces) // len(bodies):
                done.append(next_pieces[len(done)]())
            q, k, qk = pending
            if i + 1 < len(bodies):
                pending = qk_scores(*bodies[i + 1])
            rows = slice(ci * c, (ci + 1) * c)
            out_rows = slice(r * sub + ci * c, r * sub + (ci + 1) * c)
            d_intra, zeta, xi, chunk_decay = tables[h]
            v_cols = slice(h * RET_V_DIM, (h + 1) * RET_V_DIM)
            v = proj[rows, 2 * qk_w + h * RET_V_DIM:2 * qk_w + (h + 1) * RET_V_DIM]
            g = proj[rows, 2 * qk_w + v_w + h * RET_V_DIM:
                     2 * qk_w + v_w + (h + 1) * RET_V_DIM].astype(F32)
            state = state_ref[h]
            s = qk * d_intra
            y = _dot(s.astype(BF16), v) + _dot(q, state.astype(BF16)) * xi
            kz = (k.astype(F32) * zeta).astype(BF16)
            state_ref[h] = state * chunk_decay + _dot_tn(kz, v)
            ms = jnp.mean(y * y, axis=-1, keepdims=True)
            yn = y * lax.rsqrt(ms + EPS) * gain_ref[:, v_cols]
            y_ref[out_rows, v_cols] = (g / (1.0 + jnp.exp2(g * -LOG2_E)) * yn).astype(
                y_ref.dtype)
        return done

    n_groups = x_ref.shape[0] // sub
    proj = jnp.concatenate([piece() for piece in project_pieces(0)], axis=1)
    for r in range(n_groups):
        next_pieces = project_pieces(r + 1) if r + 1 < n_groups else []
        done = retain(r, proj, next_pieces)
        if done:
            proj = jnp.concatenate(done, axis=1)


def _proj_retention(x2d, g_pre, w_in, col_scale, gn_gain, batch, seq, casts):
    m, d = x2d.shape
    steps = seq // RET_ROW_TILE
    v_w = RET_HEADS * RET_V_DIM
    mem_w = MEM_HEADS * MEM_HEAD_DIM
    linear_step = lambda b, s: b * steps + s
    row_block = lambda width: pl.BlockSpec((RET_ROW_TILE, width),
                                           lambda b, s: (linear_step(b, s), 0))
    cast_in, cast_out = _cast_slab_specs(casts, batch * steps, linear_step)
    outs = pl.pallas_call(
        functools.partial(_proj_retention_kernel, n_cast=len(casts)),
        grid=(batch, steps),
        in_specs=[row_block(d), _resident(g_pre.shape), _resident(w_in.shape),
                  _resident(col_scale.shape), _resident(gn_gain.shape)] + cast_in,
        out_specs=[row_block(v_w), row_block(mem_w)] + cast_out,
        out_shape=[jax.ShapeDtypeStruct((m, v_w), BF16), jax.ShapeDtypeStruct((m, mem_w), BF16)]
                  + [jax.ShapeDtypeStruct(w.shape[-2:], BF16) for w, _ in casts],
        scratch_shapes=[pltpu.VMEM((RET_HEADS, RET_QK_DIM, RET_V_DIM), F32)],
        compiler_params=pltpu.CompilerParams(
            dimension_semantics=("parallel", "arbitrary"),
            vmem_limit_bytes=V7X_VMEM_LIMIT_BYTES),
        name="proj_retention",
    )(x2d, g_pre, w_in, col_scale, gn_gain, *[w for w, _ in casts])
    return outs[0], outs[1], outs[2:]


def _mix_out_kernel(a_ref, qm_ref, mem_ref, wmkv_ref, wout_ref, h_ref, gain_ref, o_ref,
                    mk_ref, mv_ref):
    mem_w = MEM_HEADS * MEM_HEAD_DIM
    groups = mem_w // LANES
    heads_per_group = LANES // MEM_HEAD_DIM
    lane = lax.broadcasted_iota(jnp.int32, (1, LANES), 1)
    head_lanes = [(lane >= hh * MEM_HEAD_DIM) & (lane < (hh + 1) * MEM_HEAD_DIM)
                  for hh in range(heads_per_group)]

    @pl.when(pl.program_id(1) == 0)
    def _():
        mkv = _dot(mem_ref[...].astype(BF16), wmkv_ref[...])
        mk_ref[...] = mkv[:, :mem_w].astype(BF16)
        for p in range(groups):
            mvp = mkv[:, mem_w + p * LANES:mem_w + (p + 1) * LANES].astype(BF16)
            mv_ref[p] = jnp.concatenate([mvp, jnp.ones_like(mvp)], axis=-1)

    a_w = a_ref.shape[1]
    sub = MIX_SUB_ROWS

    n_sub = a_ref.shape[0] // sub

    def scores(p, hh):
        cols = slice(p * LANES, (p + 1) * LANES)
        qp = qm_ref[:, cols]
        qh = jnp.where(head_lanes[hh], qp, jnp.zeros_like(qp))
        s = _dot_nt(qh, mk_ref[:, cols])
        return s, jnp.max(s, axis=-1, keepdims=True)

    def attend(p, hh, s, m_row):
        acc = _dot(jnp.exp2(s - m_row).astype(BF16), mv_ref[p])
        return acc[:, :LANES] / acc[:, LANES:]

    def project_a(r):
        return _dot(a_ref[r * sub:(r + 1) * sub, :], wout_ref[:a_w, :])

    def epilogue(r, mix):
        rows = slice(r * sub, (r + 1) * sub)
        o_ref[rows, :] = h_ref[rows, :] + mix * _rms_scale(mix) * gain_ref[...]

    work = [(p, hh) for p in range(groups) for hh in range(heads_per_group)]
    outs, mix_a = {}, {}
    pending = scores(*work[0])
    for i, (p, hh) in enumerate(work):
        current = pending
        if i + 1 < len(work):
            pending = scores(*work[i + 1])
        if i < n_sub // 2:
            mix_a[i] = project_a(i)
        outs[p, hh] = attend(p, hh, *current)
    mem_out = jnp.concatenate(
        [jnp.where(head_lanes[0], outs[p, 0], outs[p, 1]).astype(BF16) for p in range(groups)],
        axis=-1)
    projected = []
    for r in range(n_sub):
        if r not in mix_a:
            mix_a[r] = project_a(r)
            while projected:
                epilogue(*projected.pop(0))
        mix = mix_a.pop(r) + _dot(mem_out[r * sub:(r + 1) * sub], wout_ref[a_w:, :])
        projected.append((r, mix))
    while projected:
        epilogue(*projected.pop(0))


def _mix_out(a2d, qm_src, qm_col_block, mem, w_mkv, w_out, h2d, gain, batch, seq):
    m, d = h2d.shape
    a_w = a2d.shape[1]
    mem_w = MEM_HEADS * MEM_HEAD_DIM
    mem_len = mem.shape[1]
    steps = seq // MIX_ROW_TILE
    row = lambda b, s: b * steps + s
    return pl.pallas_call(
        _mix_out_kernel,
        grid=(batch, steps),
        in_specs=[pl.BlockSpec((MIX_ROW_TILE, a_w), lambda b, s: (row(b, s), 0)),
                  pl.BlockSpec((MIX_ROW_TILE, mem_w), lambda b, s: (row(b, s), qm_col_block)),
                  pl.BlockSpec((None, mem_len, d), lambda b, s: (b, 0, 0)),
                  _resident(w_mkv.shape), _resident(w_out.shape),
                  pl.BlockSpec((MIX_ROW_TILE, d), lambda b, s: (row(b, s), 0)),
                  _resident(gain.shape)],
        out_specs=pl.BlockSpec((MIX_ROW_TILE, d), lambda b, s: (row(b, s), 0)),
        out_shape=jax.ShapeDtypeStruct((m, d), F32),
        scratch_shapes=[pltpu.VMEM((mem_len, mem_w), BF16),
                        pltpu.VMEM((mem_w // LANES, mem_len, 2 * LANES), BF16)],
        compiler_params=pltpu.CompilerParams(
            dimension_semantics=("parallel", "arbitrary"),
            vmem_limit_bytes=V7X_VMEM_LIMIT_BYTES),
        name="mix_out",
    )(a2d, qm_src, mem, w_mkv, w_out, h2d, gain)


def _moba_kernel(slopes_ref, q_ref, k_ref, v_ref, o_ref, tmpl_ref, km_ref):
    seq = k_ref.shape[0]
    bs = MOBA_BLOCK
    nb = seq // bs
    n_groups = k_ref.shape[1] // LANES
    heads_per_group = LANES // MOBA_HEAD_DIM
    first_gated = MOBA_TOPK + 1
    p = pl.program_id(0)
    lane = lax.broadcasted_iota(jnp.int32, (1, LANES), 1)
    bias_lane0 = [nb + 3 * hh for hh in range(heads_per_group)]
    in_head = [(lane >= hh * MOBA_HEAD_DIM) & (lane < (hh + 1) * MOBA_HEAD_DIM)
               for hh in range(heads_per_group)]

    @pl.when(pl.program_id(1) == 0)
    def _():
        posi = lax.broadcasted_iota(jnp.int32, (seq, 1), 0)
        pos = posi.astype(F32)
        for gi in range(n_groups):
            t = jnp.where(lane == posi // bs, 1.0, 0.0)
            for hh in range(heads_per_group):
                head = heads_per_group * (n_groups * p + gi) + hh
                x = (slopes_ref[head] * LOG2_E) * pos
                hi = x.astype(BF16).astype(F32)
                mid = (x - hi).astype(BF16).astype(F32)
                lo = (x - hi - mid).astype(BF16).astype(F32)
                t = jnp.where(lane == bias_lane0[hh], hi, t)
                t = jnp.where(lane == bias_lane0[hh] + 1, mid, t)
                t = jnp.where(lane == bias_lane0[hh] + 2, lo, t)
            tmpl_ref[gi] = t.astype(BF16)

    rowi = lax.broadcasted_iota(jnp.int32, (bs, bs), 0)
    coli = lax.broadcasted_iota(jnp.int32, (bs, bs), 1)
    causal = coli <= rowi
    blk_id = lax.broadcasted_iota(jnp.int32, (nb, bs), 0)
    gated_rows = slice(first_gated * bs, seq)

    def prepare(gi):
        cols = slice(gi * LANES, (gi + 1) * LANES)
        k_all = k_ref[:, cols]
        for n in range(nb):
            kb = k_all[n * bs:(n + 1) * bs, :].astype(F32)
            km_ref[gi, n:n + 1, :] = jnp.sum(kb, axis=0, keepdims=True) * (1.0 / bs)
        km = km_ref[gi]
        km_hi = km.astype(BF16)
        km_lo = (km - km_hi.astype(F32)).astype(BF16)
        q_all = q_ref[:, cols]
        v_all = v_ref[:, cols]
        k_ext = jnp.concatenate([k_all, tmpl_ref[gi]], axis=-1)
        v_ext = jnp.concatenate([v_all, jnp.ones_like(v_all)], axis=-1)
        q_ext = []
        for hh in range(heads_per_group):
            qh = jnp.where(in_head[hh], q_all, jnp.zeros_like(q_all))
            gate_t = _dot_nt(km_hi, qh[gated_rows]) + _dot_nt(km_lo, qh[gated_rows])
            bias_t = []
            for j in range(first_gated, nb):
                g = gate_t[:, (j - first_gated) * bs:(j - first_gated + 1) * bs]
                beaten_by = jnp.zeros((nb, bs), F32)
                for mth in range(j):
                    gm = g[mth:mth + 1, :]
                    wins = (gm > g) | ((gm == g) & (blk_id > mth))
                    beaten_by = beaten_by + jnp.where(wins, 1.0, 0.0)
                keep = (beaten_by < float(MOBA_TOPK)) | (blk_id >= j)
                bias_t.append(jnp.where(keep, 0.0, NEG_INF))
            bias_t = jnp.concatenate(bias_t, axis=1)
            block_bias = jnp.concatenate(
                [bias_t, jnp.zeros((LANES - nb, bias_t.shape[1]), F32)], axis=0).T
            ones_lanes = jnp.where(
                (lane >= bias_lane0[hh]) & (lane < bias_lane0[hh] + 3), 1.0, 0.0)
            gated_aug = (block_bias + ones_lanes).astype(BF16)
            blocks = []
            for j in range(nb):
                aug = (gated_aug[(j - first_gated) * bs:(j - first_gated + 1) * bs]
                       if j >= first_gated
                       else jnp.broadcast_to(ones_lanes.astype(BF16), (bs, LANES)))
                blocks.append(jnp.concatenate([qh[j * bs:(j + 1) * bs], aug], axis=-1))
            q_ext.append(blocks)
        return k_ext, v_ext, q_ext

    def scores(group, j, hh):
        k_ext, _, q_ext = group
        rows = slice(j * bs, (j + 1) * bs)
        qa = q_ext[hh][j]
        s_own = jnp.where(causal, _dot_nt(qa, k_ext[rows]), NEG_INF)
        m_row = jnp.max(s_own, axis=-1, keepdims=True)
        s_past = None
        if j:
            s_past = _dot_nt(qa, k_ext[:j * bs])
            m_row = jnp.maximum(m_row, jnp.max(s_past, axis=-1, keepdims=True))
        return s_own, s_past, m_row

    def attend(group, j, s_own, s_past, m_row):
        v_ext = group[1]
        rows = slice(j * bs, (j + 1) * bs)
        acc = _dot(jnp.exp2(s_own - m_row).astype(BF16), v_ext[rows])
        if j:
            acc = acc + _dot(jnp.exp2(s_past - m_row).astype(BF16), v_ext[:j * bs])
        return acc[:, :LANES] / acc[:, LANES:]

    order = [j for pair in zip(range(nb - 1, -1, -1), range(nb)) for j in pair][:nb]
    work = [(gi, j, hh) for gi in range(n_groups) for j in order
            for hh in range(heads_per_group)]
    groups = {0: prepare(0)}
    head_out = {}
    pending = scores(groups[0], *work[0][1:])
    for i, (gi, j, hh) in enumerate(work):
        current = pending
        if i + 1 < len(work):
            nxt = work[i + 1]
            if nxt[0] not in groups:
                groups[nxt[0]] = prepare(nxt[0])
            pending = scores(groups[nxt[0]], *nxt[1:])
        head_out[hh] = attend(groups[gi], j, *current)
        if i == 0 and n_groups > 1:
            groups[1] = prepare(1)
        if hh == heads_per_group - 1:
            o_ref[j * bs:(j + 1) * bs, gi * LANES:(gi + 1) * LANES] = jnp.where(
                in_head[0], head_out[0], head_out[1]).astype(o_ref.dtype)


def _alibi_slopes(n):
    def pow2(m):
        return [2.0 ** (-8.0 * (i + 1) / m) for i in range(m)]
    p = 2 ** int(math.floor(math.log2(n)))
    s = pow2(p)
    if p < n:
        s = s + pow2(2 * p)[0::2][: n - p]
    return np.asarray(s, dtype=np.float32)


def _moba(qproj3, kv3):
    batch, seq, _ = qproj3.shape
    n_pairs = MOBA_HEADS * MOBA_HEAD_DIM // LANES
    nb = seq // MOBA_BLOCK
    slopes = jnp.asarray(_alibi_slopes(MOBA_HEADS))
    gps = MOBA_GROUPS_PER_STEP
    width = gps * LANES
    steps = n_pairs // gps
    grid_spec = pltpu.PrefetchScalarGridSpec(
        num_scalar_prefetch=1,
        grid=(steps, batch),
        in_specs=[pl.BlockSpec((None, seq, width), lambda p, b, *_: (b, 0, p)),
                  pl.BlockSpec((None, seq, width), lambda p, b, *_: (b, 0, p)),
                  pl.BlockSpec((None, seq, width), lambda p, b, *_: (b, 0, steps + p))],
        out_specs=pl.BlockSpec((None, seq, width), lambda p, b, *_: (b, 0, p)),
        scratch_shapes=[pltpu.VMEM((gps, seq, LANES), BF16), pltpu.VMEM((gps, nb, LANES), F32)],
    )
    return pl.pallas_call(
        _moba_kernel,
        grid_spec=grid_spec,
        out_shape=jax.ShapeDtypeStruct((batch, seq, n_pairs * LANES), BF16),
        compiler_params=pltpu.CompilerParams(
            dimension_semantics=("parallel", "arbitrary"),
            vmem_limit_bytes=V7X_VMEM_LIMIT_BYTES),
        name="moba",
    )(slopes, qproj3, kv3, kv3)


def kernel(x, mem, w_in_a, ret_norm_gain, w_out_a, kv_norm_gain, w_kv_shared, w_in_b, w_out_b,
           w_mem_kv, norm_pre_mix, norm_post_mix, norm_pre_mlp, norm_post_mlp, w_up, w_down):
    batch, seq, d = x.shape
    m = batch * seq
    depth = norm_pre_mix.shape[0]
    n_a = depth // 2
    qk_w = RET_HEADS * RET_QK_DIM
    v_w = RET_HEADS * RET_V_DIM
    mem_w = MEM_HEADS * MEM_HEAD_DIM
    moba_w = MOBA_HEADS * MOBA_HEAD_DIM
    row_vec = lambda g: g.reshape(1, -1)
    mem_q_scale = jnp.full((1, mem_w), MEM_HEAD_DIM ** -0.5 * LOG2_E, F32)
    moba_q_scale = jnp.full((1, moba_w), MOBA_HEAD_DIM ** -0.5 * LOG2_E, F32)
    scale_a = jnp.concatenate(
        [jnp.ones((1, 2 * qk_w + 2 * v_w), F32), mem_q_scale], axis=1)
    scale_b = jnp.concatenate([moba_q_scale, mem_q_scale], axis=1)
    scale_kv = jnp.ones((1, 2 * moba_w), F32)

    def layer_weights(l):
        w = {"mem_kv": (w_mem_kv, l), "up": (w_up, l), "down": (w_down, l)}
        if l < n_a:
            w.update({"in": (w_in_a, l), "out": (w_out_a, l)})
        else:
            w.update({"in": (w_in_b, l - n_a), "out": (w_out_b, l - n_a)})
            if l == n_a:
                w["kv"] = (w_kv_shared, None)
        return w

    def cast_outside(arr, index):
        return (arr if index is None else arr[index]).astype(BF16)

    def hosted_casts(names, weights):
        slab_ok = lambda arr: arr.shape[-2] % (m // RET_ROW_TILE * 16) == 0
        return [n for n in names if slab_ok(weights[n][0])]

    h = x.reshape(m, d)
    kv3 = None
    ready = {}
    for l in range(depth):
        f32w = layer_weights(l)
        bf16w = dict(ready)
        ready = {}
        if l < n_a:
            bf16w["in"] = cast_outside(*f32w["in"])
            todo = {("own", n): f32w[n] for n in f32w if n not in bf16w}
            if l + 1 == n_a and l + 1 < depth:
                todo.update({("next", n): w for n, w in layer_weights(l + 1).items()})
            keys = hosted_casts(list(todo), todo)
            y, q_m, cast = _proj_retention(h, norm_pre_mix[l:l + 1], bf16w["in"], scale_a,
                                           row_vec(ret_norm_gain[l]), batch, seq,
                                           [todo[k] for k in keys])
            for (where, n), w in zip(keys, cast):
                (bf16w if where == "own" else ready)[n] = w
        for n in f32w:
            if n not in bf16w:
                bf16w[n] = cast_outside(*f32w[n])
        if l < n_a:
            h = _mix_out(y, q_m, 0, mem, bf16w["mem_kv"], bf16w["out"], h,
                         row_vec(norm_post_mix[l]), batch, seq)
        else:
            if l == n_a:
                qproj, kv = _norm_proj(
                    h, jnp.stack([norm_pre_mix[l], kv_norm_gain]),
                    [bf16w["in"], bf16w["kv"]], [scale_b, scale_kv])
                kv3 = kv.reshape(batch, seq, 2 * moba_w)
            else:
                (qproj,) = _norm_proj(h, norm_pre_mix[l:l + 1], [bf16w["in"]], [scale_b])
            a = _moba(qproj.reshape(batch, seq, moba_w + mem_w), kv3)
            h = _mix_out(a.reshape(m, moba_w), qproj, moba_w // mem_w, mem,
                         bf16w["mem_kv"], bf16w["out"], h,
                         row_vec(norm_post_mix[l]), batch, seq)
        h = _mlp(h, row_vec(norm_pre_mlp[l]), row_vec(norm_post_mlp[l]),
                 bf16w["up"], bf16w["down"])
    return h.reshape(batch, seq, d)
```

```python
import functools
import math

import numpy as np
import jax
import jax.numpy as jnp
from jax import lax
from jax.experimental import pallas as pl
from jax.experimental.pallas import tpu as pltpu

F32 = jnp.float32
BF16 = jnp.bfloat16

EPS = 1e-6
NEG_INF = -1e30
LOG2_E = math.log2(math.e)

LANES = 128
BF16_SUBLANES = 16
V7X_VMEM_LIMIT_BYTES = 56 * 1024 * 1024

RET_HEADS = 4
RET_QK_DIM = 128
RET_V_DIM = 192
RET_CHUNK = 128
MOBA_HEADS = 12
MOBA_HEAD_DIM = 64
MOBA_BLOCK = 256
MOBA_TOPK = 3
MOBA_GROUPS_PER_STEP = 3
ALIBI_SPLIT = 3
MEM_HEADS = 4
MEM_HEAD_DIM = 64

ROW_TILE = 1024
DENSE_SUB_ROWS = 512
RET_ROW_TILE = 1024
RET_SUB_ROWS = 256
PROJ_PIECE_COLS = 256
MIX_ROW_TILE = 1024
MIX_SUB_ROWS = 256
FF_CHUNK = 1024


def _rms_scale(x):
    return lax.rsqrt(jnp.mean(x * x, axis=-1, keepdims=True) + EPS)


def _dot(a, b):
    return jnp.dot(a, b, preferred_element_type=F32)


def _dot_nt(a, b):
    return lax.dot_general(a, b, (((1,), (1,)), ((), ())), preferred_element_type=F32)


def _dot_tn(a, b):
    return lax.dot_general(a, b, (((0,), (0,)), ((), ())), preferred_element_type=F32)


def _resident(shape):
    zeros = (0,) * len(shape)
    return pl.BlockSpec(shape, lambda *_: zeros, pipeline_mode=pl.Buffered(1))


def _cast_slab_specs(weights, n_steps, linear_step):
    in_specs, out_specs = [], []
    for arr, layer in weights:
        rows, cols = arr.shape[-2] // n_steps, arr.shape[-1]
        out_specs.append(pl.BlockSpec((rows, cols), lambda *idx: (linear_step(*idx), 0)))
        if layer is None:
            in_specs.append(out_specs[-1])
        else:
            in_specs.append(pl.BlockSpec(
                (None, rows, cols), lambda *idx, _l=layer: (_l, linear_step(*idx), 0)))
    return in_specs, out_specs


def _cast_slabs(src_refs, dst_refs):
    for src, dst in zip(src_refs, dst_refs):
        dst[...] = src[...].astype(dst.dtype)


def _norm_proj_kernel(x_ref, g_ref, *refs, n_out):
    w_refs, s_refs, o_refs = refs[:n_out], refs[n_out:2 * n_out], refs[2 * n_out:]
    sub = DENSE_SUB_ROWS
    n_sub = x_ref.shape[0] // sub

    def normed(r):
        x = x_ref[r * sub:(r + 1) * sub, :]
        return x * _rms_scale(x)

    xhat, xhat_next = normed(0), None
    for r in range(n_sub):
        rows = slice(r * sub, (r + 1) * sub)
        for i in range(n_out):
            hn = (xhat * g_ref[i:i + 1, :]).astype(BF16)
            out = _dot(hn, w_refs[i][...])
            if i == 0 and r + 1 < n_sub:
                xhat_next = normed(r + 1)
            o_refs[i][rows, :] = (out * s_refs[i][...]).astype(o_refs[i].dtype)
        xhat = xhat_next


def _norm_proj(x2d, gains, weights, col_scales):
    m, d = x2d.shape
    n_out = len(weights)
    return pl.pallas_call(
        functools.partial(_norm_proj_kernel, n_out=n_out),
        grid=(m // ROW_TILE,),
        in_specs=[pl.BlockSpec((ROW_TILE, d), lambda i: (i, 0)),
                  _resident(gains.shape)]
                 + [_resident(w.shape) for w in weights]
                 + [_resident(s.shape) for s in col_scales],
        out_specs=[pl.BlockSpec((ROW_TILE, w.shape[1]), lambda i: (i, 0)) for w in weights],
        out_shape=[jax.ShapeDtypeStruct((m, w.shape[1]), BF16) for w in weights],
        compiler_params=pltpu.CompilerParams(
            dimension_semantics=("parallel",), vmem_limit_bytes=V7X_VMEM_LIMIT_BYTES),
        name="norm_proj",
    )(x2d, gains, *weights, *col_scales)


def _mlp_kernel(h_ref, g_pre_ref, g_post_ref, wu_ref, wd_ref, o_ref):
    sub = DENSE_SUB_ROWS
    n_sub = h_ref.shape[0] // sub
    d_ff = wu_ref.shape[1]

    def normed(r):
        h = h_ref[r * sub:(r + 1) * sub, :]
        return (h * _rms_scale(h) * g_pre_ref[...]).astype(BF16)

    def epilogue(r, y):
        rows = slice(r * sub, (r + 1) * sub)
        o_ref[rows, :] = h_ref[rows, :] + y * _rms_scale(y) * g_post_ref[...]

    hn, hn_next, finished = normed(0), None, None
    for r in range(n_sub):
        y = None
        for c in range(d_ff // FF_CHUNK):
            cols = slice(c * FF_CHUNK, (c + 1) * FF_CHUNK)
            u = _dot(hn, wu_ref[:, cols])
            if c == 0:
                if r + 1 < n_sub:
                    hn_next = normed(r + 1)
                if finished is not None:
                    epilogue(*finished)
            a = jnp.square(jnp.maximum(u, 0.0)).astype(BF16)
            d = _dot(a, wd_ref[cols, :])
            y = d if y is None else y + d
        finished, hn = (r, y), hn_next
    epilogue(*finished)


def _mlp(h2d, g_pre, g_post, w_up, w_down):
    m, d = h2d.shape
    row_spec = pl.BlockSpec((ROW_TILE, d), lambda i: (i, 0))
    return pl.pallas_call(
        _mlp_kernel,
        grid=(m // ROW_TILE,),
        in_specs=[row_spec, _resident(g_pre.shape), _resident(g_post.shape),
                  _resident(w_up.shape), _resident(w_down.shape)],
        out_specs=row_spec,
        out_shape=jax.ShapeDtypeStruct((m, d), F32),
        compiler_params=pltpu.CompilerParams(
            dimension_semantics=("parallel",), vmem_limit_bytes=V7X_VMEM_LIMIT_BYTES),
        name="mlp",
    )(h2d, g_pre, g_post, w_up, w_down)


def _proj_retention_kernel(x_ref, g_pre_ref, w_ref, scale_ref, gain_ref, *refs, n_cast):
    y_ref, qm_ref = refs[n_cast:n_cast + 2]
    state_ref = refs[-1]
    _cast_slabs(refs[:n_cast], refs[n_cast + 2:-1])
    c = RET_CHUNK
    sub = RET_SUB_ROWS
    qk_w = RET_HEADS * RET_QK_DIM
    v_w = RET_HEADS * RET_V_DIM

    @pl.when(pl.program_id(1) == 0)
    def _():
        state_ref[...] = jnp.zeros(state_ref.shape, F32)

    row = lax.broadcasted_iota(jnp.int32, (c, c), 0)
    col = lax.broadcasted_iota(jnp.int32, (c, c), 1)
    rel = (row - col).astype(F32)
    pos = lax.broadcasted_iota(jnp.int32, (c, 1), 0).astype(F32)
    qk_scale = RET_QK_DIM ** -0.5

    tables = []
    for h in range(RET_HEADS):
        log_g = float(np.log1p(-np.exp2(np.float32(-5.0 - h))))
        d_intra = jnp.where(rel >= 0, jnp.exp(log_g * jnp.maximum(rel, 0.0)), 0.0) * qk_scale
        zeta = jnp.exp(log_g * (c - 1.0 - pos)) * qk_scale
        xi = jnp.exp(log_g * (pos + 1.0))
        chunk_decay = float(np.exp(np.float32(log_g * c)))
        tables.append((d_intra, zeta, xi, chunk_decay))

    n_pieces = w_ref.shape[1] // PROJ_PIECE_COLS

    def project_pieces(r):
        x = x_ref[r * sub:(r + 1) * sub, :]
        hn = (x * _rms_scale(x) * g_pre_ref[...]).astype(BF16)

        def piece(n):
            cols = slice(n * PROJ_PIECE_COLS, (n + 1) * PROJ_PIECE_COLS)
            return (_dot(hn, w_ref[:, cols]) * scale_ref[:, cols]).astype(BF16)
        return [functools.partial(piece, n) for n in range(n_pieces)]

    def retain(r, proj, next_pieces):
        qm_ref[r * sub:(r + 1) * sub, :] = proj[:, 2 * qk_w + 2 * v_w:]
        bodies = [(ci, h) for ci in range(sub // c) for h in range(RET_HEADS)]
        done = []

        def qk_scores(ci, h):
            rows = slice(ci * c, (ci + 1) * c)
            q = proj[rows, h * RET_QK_DIM:(h + 1) * RET_QK_DIM]
            k = proj[rows, qk_w + h * RET_QK_DIM:qk_w + (h + 1) * RET_QK_DIM]
            return q, k, _dot_nt(q, k)

        pending = qk_scores(*bodies[0])
        for i, (ci, h) in enumerate(bodies):
            while len(done) < (i + 1) * len(next_pieces) // len(bodies):
                done.append(next_pieces[len(done)]())
            q, k, qk = pending
            if i + 1 < len(bodies):
                pending = qk_scores(*bodies[i + 1])
            rows = slice(ci * c, (ci + 1) * c)
            out_rows = slice(r * sub + ci * c, r * sub + (ci + 1) * c)
            d_intra, zeta, xi, chunk_decay = tables[h]
            v_cols = slice(h * RET_V_DIM, (h + 1) * RET_V_DIM)
            v = proj[rows, 2 * qk_w + h * RET_V_DIM:2 * qk_w + (h + 1) * RET_V_DIM]
            g = proj[rows, 2 * qk_w + v_w + h * RET_V_DIM:
                     2 * qk_w + v_w + (h + 1) * RET_V_DIM].astype(F32)
            state = state_ref[h]
            s = qk * d_intra
            y = _dot(s.astype(BF16), v) + _dot(q, state.astype(BF16)) * xi
            kz = (k.astype(F32) * zeta).astype(BF16)
            state_ref[h] = state * chunk_decay + _dot_tn(kz, v)
            ms = jnp.mean(y * y, axis=-1, keepdims=True)
            yn = y * lax.rsqrt(ms + EPS) * gain_ref[:, v_cols]
            y_ref[out_rows, v_cols] = (g / (1.0 + jnp.exp2(g * -LOG2_E)) * yn).astype(
                y_ref.dtype)
        return done

    n_groups = x_ref.shape[0] // sub
    proj = jnp.concatenate([piece() for piece in project_pieces(0)], axis=1)
    for r in range(n_groups):
        next_pieces = project_pieces(r + 1) if r + 1 < n_groups else []
        done = retain(r, proj, next_pieces)
        if done:
            proj = jnp.concatenate(done, axis=1)


def _proj_retention(x2d, g_pre, w_in, col_scale, gn_gain, batch, seq, casts):
    m, d = x2d.shape
    steps = seq // RET_ROW_TILE
    v_w = RET_HEADS * RET_V_DIM
    mem_w = MEM_HEADS * MEM_HEAD_DIM
    linear_step = lambda b, s: b * steps + s
    row_block = lambda width: pl.BlockSpec((RET_ROW_TILE, width),
                                           lambda b, s: (linear_step(b, s), 0))
    cast_in, cast_out = _cast_slab_specs(casts, batch * steps, linear_step)
    outs = pl.pallas_call(
        functools.partial(_proj_retention_kernel, n_cast=len(casts)),
        grid=(batch, steps),
        in_specs=[row_block(d), _resident(g_pre.shape), _resident(w_in.shape),
                  _resident(col_scale.shape), _resident(gn_gain.shape)] + cast_in,
        out_specs=[row_block(v_w), row_block(mem_w)] + cast_out,
        out_shape=[jax.ShapeDtypeStruct((m, v_w), BF16), jax.ShapeDtypeStruct((m, mem_w), BF16)]
                  + [jax.ShapeDtypeStruct(w.shape[-2:], BF16) for w, _ in casts],
        scratch_shapes=[pltpu.VMEM((RET_HEADS, RET_QK_DIM, RET_V_DIM), F32)],
        compiler_params=pltpu.CompilerParams(
            dimension_semantics=("parallel", "arbitrary"),
            vmem_limit_bytes=V7X_VMEM_LIMIT_BYTES),
        name="proj_retention",
    )(x2d, g_pre, w_in, col_scale, gn_gain, *[w for w, _ in casts])
    return outs[0], outs[1], outs[2:]


def _mix_out_kernel(a_ref, qm_ref, mem_ref, wmkv_ref, wout_ref, h_ref, gain_ref, o_ref,
                    mk_ref, mv_ref):
    mem_w = MEM_HEADS * MEM_HEAD_DIM
    groups = mem_w // LANES
    heads_per_group = LANES // MEM_HEAD_DIM
    lane = lax.broadcasted_iota(jnp.int32, (1, LANES), 1)
    head_lanes = [(lane >= hh * MEM_HEAD_DIM) & (lane < (hh + 1) * MEM_HEAD_DIM)
                  for hh in range(heads_per_group)]

    @pl.when(pl.program_id(1) == 0)
    def _():
        mkv = _dot(mem_ref[...].astype(BF16), wmkv_ref[...])
        mk_ref[...] = mkv[:, :mem_w].astype(BF16)
        for p in range(groups):
            mvp = mkv[:, mem_w + p * LANES:mem_w + (p + 1) * LANES].astype(BF16)
            mv_ref[p] = jnp.concatenate([mvp, jnp.ones_like(mvp)], axis=-1)

    a_w = a_ref.shape[1]
    sub = MIX_SUB_ROWS

    n_sub = a_ref.shape[0] // sub

    def scores(p, hh):
        cols = slice(p * LANES, (p + 1) * LANES)
        qp = qm_ref[:, cols]
        qh = jnp.where(head_lanes[hh], qp, jnp.zeros_like(qp))
        s = _dot_nt(qh, mk_ref[:, cols])
        return s, jnp.max(s, axis=-1, keepdims=True)

    def attend(p, hh, s, m_row):
        acc = _dot(jnp.exp2(s - m_row).astype(BF16), mv_ref[p])
        return acc[:, :LANES] / acc[:, LANES:]

    def project_a(r):
        return _dot(a_ref[r * sub:(r + 1) * sub, :], wout_ref[:a_w, :])

    def epilogue(r, mix):
        rows = slice(r * sub, (r + 1) * sub)
        o_ref[rows, :] = h_ref[rows, :] + mix * _rms_scale(mix) * gain_ref[...]

    work = [(p, hh) for p in range(groups) for hh in range(heads_per_group)]
    outs, mix_a = {}, {}
    pending = scores(*work[0])
    for i, (p, hh) in enumerate(work):
        current = pending
        if i + 1 < len(work):
            pending = scores(*work[i + 1])
        if i < n_sub // 2:
            mix_a[i] = project_a(i)
        outs[p, hh] = attend(p, hh, *current)
    mem_out = jnp.concatenate(
        [jnp.where(head_lanes[0], outs[p, 0], outs[p, 1]).astype(BF16) for p in range(groups)],
        axis=-1)
    projected = []
    for r in range(n_sub):
        if r not in mix_a:
            mix_a[r] = project_a(r)
            while projected:
                epilogue(*projected.pop(0))
        mix = mix_a.pop(r) + _dot(mem_out[r * sub:(r + 1) * sub], wout_ref[a_w:, :])
        projected.append((r, mix))
    while projected:
        epilogue(*projected.pop(0))


def _mix_out(a2d, qm_src, qm_col_block, mem, w_mkv, w_out, h2d, gain, batch, seq):
    m, d = h2d.shape
    a_w = a2d.shape[1]
    mem_w = MEM_HEADS * MEM_HEAD_DIM
    mem_len = mem.shape[1]
    steps = seq // MIX_ROW_TILE
    row = lambda b, s: b * steps + s
    return pl.pallas_call(
        _mix_out_kernel,
        grid=(batch, steps),
        in_specs=[pl.BlockSpec((MIX_ROW_TILE, a_w), lambda b, s: (row(b, s), 0)),
                  pl.BlockSpec((MIX_ROW_TILE, mem_w), lambda b, s: (row(b, s), qm_col_block)),
                  pl.BlockSpec((None, mem_len, d), lambda b, s: (b, 0, 0)),
                  _resident(w_mkv.shape), _resident(w_out.shape),
                  pl.BlockSpec((MIX_ROW_TILE, d), lambda b, s: (row(b, s), 0)),
                  _resident(gain.shape)],
        out_specs=pl.BlockSpec((MIX_ROW_TILE, d), lambda b, s: (row(b, s), 0)),
        out_shape=jax.ShapeDtypeStruct((m, d), F32),
        scratch_shapes=[pltpu.VMEM((mem_len, mem_w), BF16),
                        pltpu.VMEM((mem_w // LANES, mem_len, 2 * LANES), BF16)],
        compiler_params=pltpu.CompilerParams(
            dimension_semantics=("parallel", "arbitrary"),
            vmem_limit_bytes=V7X_VMEM_LIMIT_BYTES),
        name="mix_out",
    )(a2d, qm_src, mem, w_mkv, w_out, h2d, gain)


def _moba_kernel(slopes_ref, q_ref, k_ref, v_ref, o_ref, tmpl_ref, km_ref):
    seq = k_ref.shape[0]
    bs = MOBA_BLOCK
    nb = seq // bs
    n_groups = k_ref.shape[1] // LANES
    heads_per_group = LANES // MOBA_HEAD_DIM
    first_gated = MOBA_TOPK + 1
    p = pl.program_id(0)
    lane = lax.broadcasted_iota(jnp.int32, (1, LANES), 1)
    bias_lane0 = [nb + ALIBI_SPLIT * hh for hh in range(heads_per_group)]
    in_head = [(lane >= hh * MOBA_HEAD_DIM) & (lane < (hh + 1) * MOBA_HEAD_DIM)
               for hh in range(heads_per_group)]

    @pl.when(pl.program_id(1) == 0)
    def _():
        posi = lax.broadcasted_iota(jnp.int32, (seq, 1), 0)
        pos = posi.astype(F32)
        for gi in range(n_groups):
            t = jnp.where(lane == posi // bs, 1.0, 0.0)
            for hh in range(heads_per_group):
                head = heads_per_group * (n_groups * p + gi) + hh
                x = (slopes_ref[head] * LOG2_E) * pos
                hi = x.astype(BF16).astype(F32)
                mid = (x - hi).astype(BF16).astype(F32)
                lo = (x - hi - mid).astype(BF16).astype(F32)
                t = jnp.where(lane == bias_lane0[hh], hi, t)
                t = jnp.where(lane == bias_lane0[hh] + 1, mid, t)
                t = jnp.where(lane == bias_lane0[hh] + 2, lo, t)
            tmpl_ref[gi] = t.astype(BF16)

    rowi = lax.broadcasted_iota(jnp.int32, (bs, bs), 0)
    coli = lax.broadcasted_iota(jnp.int32, (bs, bs), 1)
    causal = coli <= rowi
    blk_id = lax.broadcasted_iota(jnp.int32, (nb, bs), 0)
    gated_rows = slice(first_gated * bs, seq)

    def prepare(gi):
        cols = slice(gi * LANES, (gi + 1) * LANES)
        k_all = k_ref[:, cols]
        for n in range(nb):
            kb = k_all[n * bs:(n + 1) * bs, :].astype(F32)
            km_ref[gi, n:n + 1, :] = jnp.sum(kb, axis=0, keepdims=True) * (1.0 / bs)
        km = km_ref[gi]
        km_hi = km.astype(BF16)
        km_lo = (km - km_hi.astype(F32)).astype(BF16)
        q_all = q_ref[:, cols]
        v_all = v_ref[:, cols]
        k_ext = jnp.concatenate([k_all, tmpl_ref[gi]], axis=-1)
        v_ext = jnp.concatenate([v_all, jnp.ones_like(v_all)], axis=-1)
        q_ext = []
        for hh in range(heads_per_group):
            qh = jnp.where(in_head[hh], q_all, jnp.zeros_like(q_all))
            gate_t = _dot_nt(km_hi, qh[gated_rows]) + _dot_nt(km_lo, qh[gated_rows])
            bias_t = []
            for j in range(first_gated, nb):
                g = gate_t[:, (j - first_gated) * bs:(j - first_gated + 1) * bs]
                beaten_by = jnp.zeros((nb, bs), F32)
                for mth in range(j):
                    gm = g[mth:mth + 1, :]
                    wins = (gm > g) | ((gm == g) & (blk_id > mth))
                    beaten_by = beaten_by + jnp.where(wins, 1.0, 0.0)
                keep = (beaten_by < float(MOBA_TOPK)) | (blk_id >= j)
                bias_t.append(jnp.where(keep, 0.0, NEG_INF))
            bias_t = jnp.concatenate(bias_t, axis=1)
            block_bias = jnp.concatenate(
                [bias_t, jnp.zeros((LANES - nb, bias_t.shape[1]), F32)], axis=0).T
            ones_lanes = jnp.where(
                (lane >= bias_lane0[hh]) & (lane < bias_lane0[hh] + ALIBI_SPLIT), 1.0, 0.0)
            gated_aug = (block_bias + ones_lanes).astype(BF16)
            blocks = []
            for j in range(nb):
                aug = (gated_aug[(j - first_gated) * bs:(j - first_gated + 1) * bs]
                       if j >= first_gated
                       else jnp.broadcast_to(ones_lanes.astype(BF16), (bs, LANES)))
                blocks.append(jnp.concatenate([qh[j * bs:(j + 1) * bs], aug], axis=-1))
            q_ext.append(blocks)
        return k_ext, v_ext, q_ext

    def scores(group, j, hh):
        k_ext, _, q_ext = group
        rows = slice(j * bs, (j + 1) * bs)
        qa = q_ext[hh][j]
        s_own = jnp.where(causal, _dot_nt(qa, k_ext[rows]), NEG_INF)
        m_row = jnp.max(s_own, axis=-1, keepdims=True)
        s_past = None
        if j:
            s_past = _dot_nt(qa, k_ext[:j * bs])
            m_row = jnp.maximum(m_row, jnp.max(s_past, axis=-1, keepdims=True))
        return s_own, s_past, m_row

    def attend(group, j, s_own, s_past, m_row):
        v_ext = group[1]
        rows = slice(j * bs, (j + 1) * bs)
        acc = _dot(jnp.exp2(s_own - m_row).astype(BF16), v_ext[rows])
        if j:
            acc = acc + _dot(jnp.exp2(s_past - m_row).astype(BF16), v_ext[:j * bs])
        return acc[:, :LANES] / acc[:, LANES:]

    order = [j for pair in zip(range(nb - 1, -1, -1), range(nb)) for j in pair][:nb]
    work = [(gi, j, hh) for gi in range(n_groups) for j in order
            for hh in range(heads_per_group)]
    groups = {0: prepare(0)}
    head_out = {}
    pending = scores(groups[0], *work[0][1:])
    for i, (gi, j, hh) in enumerate(work):
        current = pending
        if i + 1 < len(work):
            nxt = work[i + 1]
            if nxt[0] not in groups:
                groups[nxt[0]] = prepare(nxt[0])
            pending = scores(groups[nxt[0]], *nxt[1:])
        head_out[hh] = attend(groups[gi], j, *current)
        if i == 0 and n_groups > 1:
            groups[1] = prepare(1)
        if hh == heads_per_group - 1:
            o_ref[j * bs:(j + 1) * bs, gi * LANES:(gi + 1) * LANES] = jnp.where(
                in_head[0], head_out[0], head_out[1]).astype(o_ref.dtype)


def _alibi_slopes(n):
    def pow2(m):
        return [2.0 ** (-8.0 * (i + 1) / m) for i in range(m)]
    p = 2 ** int(math.floor(math.log2(n)))
    s = pow2(p)
    if p < n:
        s = s + pow2(2 * p)[0::2][: n - p]
    return np.asarray(s, dtype=np.float32)


def _moba(qproj3, kv3):
    batch, seq, _ = qproj3.shape
    n_pairs = MOBA_HEADS * MOBA_HEAD_DIM // LANES
    nb = seq // MOBA_BLOCK
    slopes = jnp.asarray(_alibi_slopes(MOBA_HEADS))
    gps = MOBA_GROUPS_PER_STEP
    width = gps * LANES
    steps = n_pairs // gps
    grid_spec = pltpu.PrefetchScalarGridSpec(
        num_scalar_prefetch=1,
        grid=(steps, batch),
        in_specs=[pl.BlockSpec((None, seq, width), lambda p, b, *_: (b, 0, p)),
                  pl.BlockSpec((None, seq, width), lambda p, b, *_: (b, 0, p)),
                  pl.BlockSpec((None, seq, width), lambda p, b, *_: (b, 0, steps + p))],
        out_specs=pl.BlockSpec((None, seq, width), lambda p, b, *_: (b, 0, p)),
        scratch_shapes=[pltpu.VMEM((gps, seq, LANES), BF16), pltpu.VMEM((gps, nb, LANES), F32)],
    )
    return pl.pallas_call(
        _moba_kernel,
        grid_spec=grid_spec,
        out_shape=jax.ShapeDtypeStruct((batch, seq, n_pairs * LANES), BF16),
        compiler_params=pltpu.CompilerParams(
            dimension_semantics=("parallel", "arbitrary"),
            vmem_limit_bytes=V7X_VMEM_LIMIT_BYTES),
        name="moba",
    )(slopes, qproj3, kv3, kv3)


def kernel(x, mem, w_in_a, ret_norm_gain, w_out_a, kv_norm_gain, w_kv_shared, w_in_b, w_out_b,
           w_mem_kv, norm_pre_mix, norm_post_mix, norm_pre_mlp, norm_post_mlp, w_up, w_down):
    batch, seq, d = x.shape
    m = batch * seq
    depth = norm_pre_mix.shape[0]
    n_a = depth // 2
    qk_w = RET_HEADS * RET_QK_DIM
    v_w = RET_HEADS * RET_V_DIM
    mem_w = MEM_HEADS * MEM_HEAD_DIM
    moba_w = MOBA_HEADS * MOBA_HEAD_DIM
    row_vec = lambda g: g.reshape(1, -1)
    mem_q_scale = jnp.full((1, mem_w), MEM_HEAD_DIM ** -0.5 * LOG2_E, F32)
    moba_q_scale = jnp.full((1, moba_w), MOBA_HEAD_DIM ** -0.5 * LOG2_E, F32)
    scale_a = jnp.concatenate(
        [jnp.ones((1, 2 * qk_w + 2 * v_w), F32), mem_q_scale], axis=1)
    scale_b = jnp.concatenate([moba_q_scale, mem_q_scale], axis=1)
    scale_kv = jnp.ones((1, 2 * moba_w), F32)

    def layer_weights(l):
        w = {"mem_kv": (w_mem_kv, l), "up": (w_up, l), "down": (w_down, l)}
        if l < n_a:
            w.update({"in": (w_in_a, l), "out": (w_out_a, l)})
        else:
            w.update({"in": (w_in_b, l - n_a), "out": (w_out_b, l - n_a)})
            if l == n_a:
                w["kv"] = (w_kv_shared, None)
        return w

    def cast_outside(arr, index):
        return (arr if index is None else arr[index]).astype(BF16)

    def hosted_casts(names, weights):
        slab_ok = lambda arr: arr.shape[-2] % (m // RET_ROW_TILE * BF16_SUBLANES) == 0
        return [n for n in names if slab_ok(weights[n][0])]

    h = x.reshape(m, d)
    kv3 = None
    ready = {}
    for l in range(depth):
        f32w = layer_weights(l)
        bf16w = dict(ready)
        ready = {}
        if l < n_a:
            bf16w["in"] = cast_outside(*f32w["in"])
            todo = {("own", n): f32w[n] for n in f32w if n not in bf16w}
            if l + 1 == n_a and l + 1 < depth:
                todo.update({("next", n): w for n, w in layer_weights(l + 1).items()})
            keys = hosted_casts(list(todo), todo)
            y, q_m, cast = _proj_retention(h, norm_pre_mix[l:l + 1], bf16w["in"], scale_a,
                                           row_vec(ret_norm_gain[l]), batch, seq,
                                           [todo[k] for k in keys])
            for (where, n), w in zip(keys, cast):
                (bf16w if where == "own" else ready)[n] = w
        for n in f32w:
            if n not in bf16w:
                bf16w[n] = cast_outside(*f32w[n])
        if l < n_a:
            h = _mix_out(y, q_m, 0, mem, bf16w["mem_kv"], bf16w["out"], h,
                         row_vec(norm_post_mix[l]), batch, seq)
        else:
            if l == n_a:
                qproj, kv = _norm_proj(
                    h, jnp.stack([norm_pre_mix[l], kv_norm_gain]),
                    [bf16w["in"], bf16w["kv"]], [scale_b, scale_kv])
                kv3 = kv.reshape(batch, seq, 2 * moba_w)
            else:
                (qproj,) = _norm_proj(h, norm_pre_mix[l:l + 1], [bf16w["in"]], [scale_b])
            a = _moba(qproj.reshape(batch, seq, moba_w + mem_w), kv3)
            h = _mix_out(a.reshape(m, moba_w), qproj, moba_w // mem_w, mem,
                         bf16w["mem_kv"], bf16w["out"], h,
                         row_vec(norm_post_mix[l]), batch, seq)
        h = _mlp(h, row_vec(norm_pre_mlp[l]), row_vec(norm_post_mlp[l]),
                 bf16w["up"], bf16w["down"])
    return h.reshape(batch, seq, d)
```

```python
import functools
import math

import numpy as np
import jax
import jax.numpy as jnp
from jax import lax
from jax.experimental import pallas as pl
from jax.experimental.pallas import tpu as pltpu

F32 = jnp.float32
BF16 = jnp.bfloat16

EPS = 1e-6
NEG_INF = -1e30
LOG2_E = math.log2(math.e)

LANES = 128
BF16_SUBLANES = 16
V7X_VMEM_LIMIT_BYTES = 56 * 1024 * 1024

RET_HEADS = 4
RET_QK_DIM = 128
RET_V_DIM = 192
RET_CHUNK = 128
MOBA_HEADS = 12
MOBA_HEAD_DIM = 64
MOBA_BLOCK = 256
MOBA_TOPK = 3
MOBA_GROUPS_PER_STEP = 3
ALIBI_SPLIT = 3
MEM_HEADS = 4
MEM_HEAD_DIM = 64

ROW_TILE = 1024
DENSE_SUB_ROWS = 512
RET_ROW_TILE = 1024
RET_SUB_ROWS = 256
PROJ_PIECE_COLS = 256
MIX_ROW_TILE = 1024
MIX_SUB_ROWS = 256
FF_CHUNK = 1024


def _rms_scale(x):
    return lax.rsqrt(jnp.mean(x * x, axis=-1, keepdims=True) + EPS)


def _dot(a, b):
    return jnp.dot(a, b, preferred_element_type=F32)


def _dot_nt(a, b):
    return lax.dot_general(a, b, (((1,), (1,)), ((), ())), preferred_element_type=F32)


def _dot_tn(a, b):
    return lax.dot_general(a, b, (((0,), (0,)), ((), ())), preferred_element_type=F32)


def _resident(shape):
    zeros = (0,) * len(shape)
    return pl.BlockSpec(shape, lambda *_: zeros, pipeline_mode=pl.Buffered(1))


def _cast_slab_specs(weights, n_steps, linear_step):
    in_specs, out_specs = [], []
    for arr, layer in weights:
        rows, cols = arr.shape[-2] // n_steps, arr.shape[-1]
        out_specs.append(pl.BlockSpec((rows, cols), lambda *idx: (linear_step(*idx), 0)))
        if layer is None:
            in_specs.append(out_specs[-1])
        else:
            in_specs.append(pl.BlockSpec(
                (None, rows, cols), lambda *idx, _l=layer: (_l, linear_step(*idx), 0)))
    return in_specs, out_specs


def _cast_slabs(src_refs, dst_refs):
    for src, dst in zip(src_refs, dst_refs):
        dst[...] = src[...].astype(dst.dtype)


def _norm_proj_kernel(x_ref, g_ref, *refs, n_out):
    w_refs, s_refs, o_refs = refs[:n_out], refs[n_out:2 * n_out], refs[2 * n_out:]
    sub = DENSE_SUB_ROWS
    n_sub = x_ref.shape[0] // sub

    def normed(r):
        x = x_ref[r * sub:(r + 1) * sub, :]
        return x * _rms_scale(x)

    xhat, xhat_next = normed(0), None
    for r in range(n_sub):
        rows = slice(r * sub, (r + 1) * sub)
        for i in range(n_out):
            hn = (xhat * g_ref[i:i + 1, :]).astype(BF16)
            out = _dot(hn, w_refs[i][...])
            if i == 0 and r + 1 < n_sub:
                xhat_next = normed(r + 1)
            o_refs[i][rows, :] = (out * s_refs[i][...]).astype(o_refs[i].dtype)
        xhat = xhat_next


def _norm_proj(x2d, gains, weights, col_scales):
    m, d = x2d.shape
    n_out = len(weights)
    return pl.pallas_call(
        functools.partial(_norm_proj_kernel, n_out=n_out),
        grid=(m // ROW_TILE,),
        in_specs=[pl.BlockSpec((ROW_TILE, d), lambda i: (i, 0)),
                  _resident(gains.shape)]
                 + [_resident(w.shape) for w in weights]
                 + [_resident(s.shape) for s in col_scales],
        out_specs=[pl.BlockSpec((ROW_TILE, w.shape[1]), lambda i: (i, 0)) for w in weights],
        out_shape=[jax.ShapeDtypeStruct((m, w.shape[1]), BF16) for w in weights],
        compiler_params=pltpu.CompilerParams(
            dimension_semantics=("parallel",), vmem_limit_bytes=V7X_VMEM_LIMIT_BYTES),
        name="norm_proj",
    )(x2d, gains, *weights, *col_scales)


def _mlp_kernel(h_ref, hn_ref, g_post_ref, wu_ref, wd_ref, o_ref):
    sub = DENSE_SUB_ROWS
    n_sub = h_ref.shape[0] // sub
    d_ff = wu_ref.shape[1]

    def epilogue(r, y):
        rows = slice(r * sub, (r + 1) * sub)
        o_ref[rows, :] = h_ref[rows, :] + y * _rms_scale(y) * g_post_ref[...]

    finished = None
    for r in range(n_sub):
        hn = hn_ref[r * sub:(r + 1) * sub, :]
        y = None
        for c in range(d_ff // FF_CHUNK):
            cols = slice(c * FF_CHUNK, (c + 1) * FF_CHUNK)
            u = _dot(hn, wu_ref[:, cols])
            if c == 0 and finished is not None:
                epilogue(*finished)
            a = jnp.square(jnp.maximum(u, 0.0)).astype(BF16)
            d = _dot(a, wd_ref[cols, :])
            y = d if y is None else y + d
        finished = (r, y)
    epilogue(*finished)


def _mlp(h2d, hn2d, g_post, w_up, w_down):
    m, d = h2d.shape
    row_spec = pl.BlockSpec((ROW_TILE, d), lambda i: (i, 0))
    return pl.pallas_call(
        _mlp_kernel,
        grid=(m // ROW_TILE,),
        in_specs=[row_spec, row_spec, _resident(g_post.shape),
                  _resident(w_up.shape), _resident(w_down.shape)],
        out_specs=row_spec,
        out_shape=jax.ShapeDtypeStruct((m, d), F32),
        compiler_params=pltpu.CompilerParams(
            dimension_semantics=("parallel",), vmem_limit_bytes=V7X_VMEM_LIMIT_BYTES),
        name="mlp",
    )(h2d, hn2d, g_post, w_up, w_down)


def _proj_retention_kernel(x_ref, g_pre_ref, w_ref, scale_ref, gain_ref, *refs, n_cast):
    y_ref, qm_ref = refs[n_cast:n_cast + 2]
    state_ref = refs[-1]
    _cast_slabs(refs[:n_cast], refs[n_cast + 2:-1])
    c = RET_CHUNK
    sub = RET_SUB_ROWS
    qk_w = RET_HEADS * RET_QK_DIM
    v_w = RET_HEADS * RET_V_DIM

    @pl.when(pl.program_id(1) == 0)
    def _():
        state_ref[...] = jnp.zeros(state_ref.shape, F32)

    row = lax.broadcasted_iota(jnp.int32, (c, c), 0)
    col = lax.broadcasted_iota(jnp.int32, (c, c), 1)
    rel = (row - col).astype(F32)
    pos = lax.broadcasted_iota(jnp.int32, (c, 1), 0).astype(F32)
    qk_scale = RET_QK_DIM ** -0.5

    tables = []
    for h in range(RET_HEADS):
        log_g = float(np.log1p(-np.exp2(np.float32(-5.0 - h))))
        d_intra = jnp.where(rel >= 0, jnp.exp(log_g * jnp.maximum(rel, 0.0)), 0.0) * qk_scale
        zeta = jnp.exp(log_g * (c - 1.0 - pos)) * qk_scale
        xi = jnp.exp(log_g * (pos + 1.0))
        chunk_decay = float(np.exp(np.float32(log_g * c)))
        tables.append((d_intra, zeta, xi, chunk_decay))

    n_pieces = w_ref.shape[1] // PROJ_PIECE_COLS

    def project_pieces(r):
        x = x_ref[r * sub:(r + 1) * sub, :]
        hn = (x * _rms_scale(x) * g_pre_ref[...]).astype(BF16)

        def piece(n):
            cols = slice(n * PROJ_PIECE_COLS, (n + 1) * PROJ_PIECE_COLS)
            return (_dot(hn, w_ref[:, cols]) * scale_ref[:, cols]).astype(BF16)
        return [functools.partial(piece, n) for n in range(n_pieces)]

    def retain(r, proj, next_pieces):
        qm_ref[r * sub:(r + 1) * sub, :] = proj[:, 2 * qk_w + 2 * v_w:]
        bodies = [(ci, h) for ci in range(sub // c) for h in range(RET_HEADS)]
        done = []

        def qk_scores(ci, h):
            rows = slice(ci * c, (ci + 1) * c)
            q = proj[rows, h * RET_QK_DIM:(h + 1) * RET_QK_DIM]
            k = proj[rows, qk_w + h * RET_QK_DIM:qk_w + (h + 1) * RET_QK_DIM]
            return q, k, _dot_nt(q, k)

        pending = qk_scores(*bodies[0])
        for i, (ci, h) in enumerate(bodies):
            while len(done) < (i + 1) * len(next_pieces) // len(bodies):
                done.append(next_pieces[len(done)]())
            q, k, qk = pending
            if i + 1 < len(bodies):
                pending = qk_scores(*bodies[i + 1])
            rows = slice(ci * c, (ci + 1) * c)
            out_rows = slice(r * sub + ci * c, r * sub + (ci + 1) * c)
            d_intra, zeta, xi, chunk_decay = tables[h]
            v_cols = slice(h * RET_V_DIM, (h + 1) * RET_V_DIM)
            v = proj[rows, 2 * qk_w + h * RET_V_DIM:2 * qk_w + (h + 1) * RET_V_DIM]
            g = proj[rows, 2 * qk_w + v_w + h * RET_V_DIM:
                     2 * qk_w + v_w + (h + 1) * RET_V_DIM].astype(F32)
            state = state_ref[h]
            s = qk * d_intra
            y = _dot(s.astype(BF16), v) + _dot(q, state.astype(BF16)) * xi
            kz = (k.astype(F32) * zeta).astype(BF16)
            state_ref[h] = state * chunk_decay + _dot_tn(kz, v)
            ms = jnp.mean(y * y, axis=-1, keepdims=True)
            yn = y * lax.rsqrt(ms + EPS) * gain_ref[:, v_cols]
            y_ref[out_rows, v_cols] = (g / (1.0 + jnp.exp2(g * -LOG2_E)) * yn).astype(
                y_ref.dtype)
        return done

    n_groups = x_ref.shape[0] // sub
    proj = jnp.concatenate([piece() for piece in project_pieces(0)], axis=1)
    for r in range(n_groups):
        next_pieces = project_pieces(r + 1) if r + 1 < n_groups else []
        done = retain(r, proj, next_pieces)
        if done:
            proj = jnp.concatenate(done, axis=1)


def _proj_retention(x2d, g_pre, w_in, col_scale, gn_gain, batch, seq, casts):
    m, d = x2d.shape
    steps = seq // RET_ROW_TILE
    v_w = RET_HEADS * RET_V_DIM
    mem_w = MEM_HEADS * MEM_HEAD_DIM
    linear_step = lambda b, s: b * steps + s
    row_block = lambda width: pl.BlockSpec((RET_ROW_TILE, width),
                                           lambda b, s: (linear_step(b, s), 0))
    cast_in, cast_out = _cast_slab_specs(casts, batch * steps, linear_step)
    outs = pl.pallas_call(
        functools.partial(_proj_retention_kernel, n_cast=len(casts)),
        grid=(batch, steps),
        in_specs=[row_block(d), _resident(g_pre.shape), _resident(w_in.shape),
                  _resident(col_scale.shape), _resident(gn_gain.shape)] + cast_in,
        out_specs=[row_block(v_w), row_block(mem_w)] + cast_out,
        out_shape=[jax.ShapeDtypeStruct((m, v_w), BF16), jax.ShapeDtypeStruct((m, mem_w), BF16)]
                  + [jax.ShapeDtypeStruct(w.shape[-2:], BF16) for w, _ in casts],
        scratch_shapes=[pltpu.VMEM((RET_HEADS, RET_QK_DIM, RET_V_DIM), F32)],
        compiler_params=pltpu.CompilerParams(
            dimension_semantics=("parallel", "arbitrary"),
            vmem_limit_bytes=V7X_VMEM_LIMIT_BYTES),
        name="proj_retention",
    )(x2d, g_pre, w_in, col_scale, gn_gain, *[w for w, _ in casts])
    return outs[0], outs[1], outs[2:]


def _mix_out_kernel(a_ref, qm_ref, mem_ref, wmkv_ref, wout_ref, h_ref, gain_ref, g_mlp_ref,
                    o_ref, hn_ref, mk_ref, mv_ref):
    mem_w = MEM_HEADS * MEM_HEAD_DIM
    groups = mem_w // LANES
    heads_per_group = LANES // MEM_HEAD_DIM
    lane = lax.broadcasted_iota(jnp.int32, (1, LANES), 1)
    head_lanes = [(lane >= hh * MEM_HEAD_DIM) & (lane < (hh + 1) * MEM_HEAD_DIM)
                  for hh in range(heads_per_group)]

    @pl.when(pl.program_id(1) == 0)
    def _():
        mkv = _dot(mem_ref[...].astype(BF16), wmkv_ref[...])
        mk_ref[...] = mkv[:, :mem_w].astype(BF16)
        for p in range(groups):
            mvp = mkv[:, mem_w + p * LANES:mem_w + (p + 1) * LANES].astype(BF16)
            mv_ref[p] = jnp.concatenate([mvp, jnp.ones_like(mvp)], axis=-1)

    a_w = a_ref.shape[1]
    sub = MIX_SUB_ROWS

    n_sub = a_ref.shape[0] // sub

    def scores(p, hh):
        cols = slice(p * LANES, (p + 1) * LANES)
        qp = qm_ref[:, cols]
        qh = jnp.where(head_lanes[hh], qp, jnp.zeros_like(qp))
        s = _dot_nt(qh, mk_ref[:, cols])
        return s, jnp.max(s, axis=-1, keepdims=True)

    def attend(p, hh, s, m_row):
        acc = _dot(jnp.exp2(s - m_row).astype(BF16), mv_ref[p])
        return acc[:, :LANES] / acc[:, LANES:]

    def project_a(r):
        return _dot(a_ref[r * sub:(r + 1) * sub, :], wout_ref[:a_w, :])

    def epilogue(r, mix):
        rows = slice(r * sub, (r + 1) * sub)
        h_new = h_ref[rows, :] + mix * _rms_scale(mix) * gain_ref[...]
        o_ref[rows, :] = h_new
        hn_ref[rows, :] = (h_new * _rms_scale(h_new) * g_mlp_ref[...]).astype(hn_ref.dtype)

    work = [(p, hh) for p in range(groups) for hh in range(heads_per_group)]
    outs, mix_a = {}, {}
    pending = scores(*work[0])
    for i, (p, hh) in enumerate(work):
        current = pending
        if i + 1 < len(work):
            pending = scores(*work[i + 1])
        if i < n_sub // 2:
            mix_a[i] = project_a(i)
        outs[p, hh] = attend(p, hh, *current)
    mem_out = jnp.concatenate(
        [jnp.where(head_lanes[0], outs[p, 0], outs[p, 1]).astype(BF16) for p in range(groups)],
        axis=-1)
    projected = []
    for r in range(n_sub):
        if r not in mix_a:
            mix_a[r] = project_a(r)
            while projected:
                epilogue(*projected.pop(0))
        mix = mix_a.pop(r) + _dot(mem_out[r * sub:(r + 1) * sub], wout_ref[a_w:, :])
        projected.append((r, mix))
    while projected:
        epilogue(*projected.pop(0))


def _mix_out(a2d, qm_src, qm_col_block, mem, w_mkv, w_out, h2d, gain, g_mlp, batch, seq):
    m, d = h2d.shape
    a_w = a2d.shape[1]
    mem_w = MEM_HEADS * MEM_HEAD_DIM
    mem_len = mem.shape[1]
    steps = seq // MIX_ROW_TILE
    row = lambda b, s: b * steps + s
    row_block = pl.BlockSpec((MIX_ROW_TILE, d), lambda b, s: (row(b, s), 0))
    return pl.pallas_call(
        _mix_out_kernel,
        grid=(batch, steps),
        in_specs=[pl.BlockSpec((MIX_ROW_TILE, a_w), lambda b, s: (row(b, s), 0)),
                  pl.BlockSpec((MIX_ROW_TILE, mem_w), lambda b, s: (row(b, s), qm_col_block)),
                  pl.BlockSpec((None, mem_len, d), lambda b, s: (b, 0, 0)),
                  _resident(w_mkv.shape), _resident(w_out.shape),
                  row_block, _resident(gain.shape), _resident(g_mlp.shape)],
        out_specs=[row_block, row_block],
        out_shape=[jax.ShapeDtypeStruct((m, d), F32), jax.ShapeDtypeStruct((m, d), BF16)],
        scratch_shapes=[pltpu.VMEM((mem_len, mem_w), BF16),
                        pltpu.VMEM((mem_w // LANES, mem_len, 2 * LANES), BF16)],
        compiler_params=pltpu.CompilerParams(
            dimension_semantics=("parallel", "arbitrary"),
            vmem_limit_bytes=V7X_VMEM_LIMIT_BYTES),
        name="mix_out",
    )(a2d, qm_src, mem, w_mkv, w_out, h2d, gain, g_mlp)


def _moba_kernel(slopes_ref, q_ref, k_ref, v_ref, o_ref, tmpl_ref, km_ref):
    seq = k_ref.shape[0]
    bs = MOBA_BLOCK
    nb = seq // bs
    n_groups = k_ref.shape[1] // LANES
    heads_per_group = LANES // MOBA_HEAD_DIM
    first_gated = MOBA_TOPK + 1
    p = pl.program_id(0)
    lane = lax.broadcasted_iota(jnp.int32, (1, LANES), 1)
    bias_lane0 = [nb + ALIBI_SPLIT * hh for hh in range(heads_per_group)]
    in_head = [(lane >= hh * MOBA_HEAD_DIM) & (lane < (hh + 1) * MOBA_HEAD_DIM)
               for hh in range(heads_per_group)]

    @pl.when(pl.program_id(1) == 0)
    def _():
        posi = lax.broadcasted_iota(jnp.int32, (seq, 1), 0)
        pos = posi.astype(F32)
        for gi in range(n_groups):
            t = jnp.where(lane == posi // bs, 1.0, 0.0)
            for hh in range(heads_per_group):
                head = heads_per_group * (n_groups * p + gi) + hh
                x = (slopes_ref[head] * LOG2_E) * pos
                hi = x.astype(BF16).astype(F32)
                mid = (x - hi).astype(BF16).astype(F32)
                lo = (x - hi - mid).astype(BF16).astype(F32)
                t = jnp.where(lane == bias_lane0[hh], hi, t)
                t = jnp.where(lane == bias_lane0[hh] + 1, mid, t)
                t = jnp.where(lane == bias_lane0[hh] + 2, lo, t)
            tmpl_ref[gi] = t.astype(BF16)

    rowi = lax.broadcasted_iota(jnp.int32, (bs, bs), 0)
    coli = lax.broadcasted_iota(jnp.int32, (bs, bs), 1)
    causal = coli <= rowi
    blk_id = lax.broadcasted_iota(jnp.int32, (nb, bs), 0)
    gated_rows = slice(first_gated * bs, seq)

    def prepare(gi):
        cols = slice(gi * LANES, (gi + 1) * LANES)
        k_all = k_ref[:, cols]
        for n in range(nb):
            kb = k_all[n * bs:(n + 1) * bs, :].astype(F32)
            km_ref[gi, n:n + 1, :] = jnp.sum(kb, axis=0, keepdims=True) * (1.0 / bs)
        km = km_ref[gi]
        km_hi = km.astype(BF16)
        km_lo = (km - km_hi.astype(F32)).astype(BF16)
        q_all = q_ref[:, cols]
        v_all = v_ref[:, cols]
        k_ext = jnp.concatenate([k_all, tmpl_ref[gi]], axis=-1)
        v_ext = jnp.concatenate([v_all, jnp.ones_like(v_all)], axis=-1)
        q_ext = []
        for hh in range(heads_per_group):
            qh = jnp.where(in_head[hh], q_all, jnp.zeros_like(q_all))
            gate_t = _dot_nt(km_hi, qh[gated_rows]) + _dot_nt(km_lo, qh[gated_rows])
            bias_t = []
            for j in range(first_gated, nb):
                g = gate_t[:, (j - first_gated) * bs:(j - first_gated + 1) * bs]
                beaten_by = jnp.zeros((nb, bs), F32)
                for mth in range(j):
                    gm = g[mth:mth + 1, :]
                    wins = (gm > g) | ((gm == g) & (blk_id > mth))
                    beaten_by = beaten_by + jnp.where(wins, 1.0, 0.0)
                keep = (beaten_by < float(MOBA_TOPK)) | (blk_id >= j)
                bias_t.append(jnp.where(keep, 0.0, NEG_INF))
            bias_t = jnp.concatenate(bias_t, axis=1)
            block_bias = jnp.concatenate(
                [bias_t, jnp.zeros((LANES - nb, bias_t.shape[1]), F32)], axis=0).T
            ones_lanes = jnp.where(
                (lane >= bias_lane0[hh]) & (lane < bias_lane0[hh] + ALIBI_SPLIT), 1.0, 0.0)
            gated_aug = (block_bias + ones_lanes).astype(BF16)
            blocks = []
            for j in range(nb):
                aug = (gated_aug[(j - first_gated) * bs:(j - first_gated + 1) * bs]
                       if j >= first_gated
                       else jnp.broadcast_to(ones_lanes.astype(BF16), (bs, LANES)))
                blocks.append(jnp.concatenate([qh[j * bs:(j + 1) * bs], aug], axis=-1))
            q_ext.append(blocks)
        return k_ext, v_ext, q_ext

    def scores(group, j, hh):
        k_ext, _, q_ext = group
        rows = slice(j * bs, (j + 1) * bs)
        qa = q_ext[hh][j]
        s_own = jnp.where(causal, _dot_nt(qa, k_ext[rows]), NEG_INF)
        m_row = jnp.max(s_own, axis=-1, keepdims=True)
        s_past = None
        if j:
            s_past = _dot_nt(qa, k_ext[:j * bs])
            m_row = jnp.maximum(m_row, jnp.max(s_past, axis=-1, keepdims=True))
        return s_own, s_past, m_row

    def attend(group, j, s_own, s_past, m_row):
        v_ext = group[1]
        rows = slice(j * bs, (j + 1) * bs)
        acc = _dot(jnp.exp2(s_own - m_row).astype(BF16), v_ext[rows])
        if j:
            acc = acc + _dot(jnp.exp2(s_past - m_row).astype(BF16), v_ext[:j * bs])
        return acc[:, :LANES] / acc[:, LANES:]

    order = [j for pair in zip(range(nb - 1, -1, -1), range(nb)) for j in pair][:nb]
    work = [(gi, j, hh) for gi in range(n_groups) for j in order
            for hh in range(heads_per_group)]
    groups = {0: prepare(0)}
    head_out = {}
    pending = scores(groups[0], *work[0][1:])
    for i, (gi, j, hh) in enumerate(work):
        current = pending
        if i + 1 < len(work):
            nxt = work[i + 1]
            if nxt[0] not in groups:
                groups[nxt[0]] = prepare(nxt[0])
            pending = scores(groups[nxt[0]], *nxt[1:])
        head_out[hh] = attend(groups[gi], j, *current)
        if i == 0 and n_groups > 1:
            groups[1] = prepare(1)
        if hh == heads_per_group - 1:
            o_ref[j * bs:(j + 1) * bs, gi * LANES:(gi + 1) * LANES] = jnp.where(
                in_head[0], head_out[0], head_out[1]).astype(o_ref.dtype)


def _alibi_slopes(n):
    def pow2(m):
        return [2.0 ** (-8.0 * (i + 1) / m) for i in range(m)]
    p = 2 ** int(math.floor(math.log2(n)))
    s = pow2(p)
    if p < n:
        s = s + pow2(2 * p)[0::2][: n - p]
    return np.asarray(s, dtype=np.float32)


def _moba(qproj3, kv3):
    batch, seq, _ = qproj3.shape
    n_pairs = MOBA_HEADS * MOBA_HEAD_DIM // LANES
    nb = seq // MOBA_BLOCK
    slopes = jnp.asarray(_alibi_slopes(MOBA_HEADS))
    gps = MOBA_GROUPS_PER_STEP
    width = gps * LANES
    steps = n_pairs // gps
    grid_spec = pltpu.PrefetchScalarGridSpec(
        num_scalar_prefetch=1,
        grid=(steps, batch),
        in_specs=[pl.BlockSpec((None, seq, width), lambda p, b, *_: (b, 0, p)),
                  pl.BlockSpec((None, seq, width), lambda p, b, *_: (b, 0, p)),
                  pl.BlockSpec((None, seq, width), lambda p, b, *_: (b, 0, steps + p))],
        out_specs=pl.BlockSpec((None, seq, width), lambda p, b, *_: (b, 0, p)),
        scratch_shapes=[pltpu.VMEM((gps, seq, LANES), BF16), pltpu.VMEM((gps, nb, LANES), F32)],
    )
    return pl.pallas_call(
        _moba_kernel,
        grid_spec=grid_spec,
        out_shape=jax.ShapeDtypeStruct((batch, seq, n_pairs * LANES), BF16),
        compiler_params=pltpu.CompilerParams(
            dimension_semantics=("parallel", "arbitrary"),
            vmem_limit_bytes=V7X_VMEM_LIMIT_BYTES),
        name="moba",
    )(slopes, qproj3, kv3, kv3)


def kernel(x, mem, w_in_a, ret_norm_gain, w_out_a, kv_norm_gain, w_kv_shared, w_in_b, w_out_b,
           w_mem_kv, norm_pre_mix, norm_post_mix, norm_pre_mlp, norm_post_mlp, w_up, w_down):
    batch, seq, d = x.shape
    m = batch * seq
    depth = norm_pre_mix.shape[0]
    n_a = depth // 2
    qk_w = RET_HEADS * RET_QK_DIM
    v_w = RET_HEADS * RET_V_DIM
    mem_w = MEM_HEADS * MEM_HEAD_DIM
    moba_w = MOBA_HEADS * MOBA_HEAD_DIM
    row_vec = lambda g: g.reshape(1, -1)
    mem_q_scale = jnp.full((1, mem_w), MEM_HEAD_DIM ** -0.5 * LOG2_E, F32)
    moba_q_scale = jnp.full((1, moba_w), MOBA_HEAD_DIM ** -0.5 * LOG2_E, F32)
    scale_a = jnp.concatenate(
        [jnp.ones((1, 2 * qk_w + 2 * v_w), F32), mem_q_scale], axis=1)
    scale_b = jnp.concatenate([moba_q_scale, mem_q_scale], axis=1)
    scale_kv = jnp.ones((1, 2 * moba_w), F32)

    def layer_weights(l):
        w = {"mem_kv": (w_mem_kv, l), "up": (w_up, l), "down": (w_down, l)}
        if l < n_a:
            w.update({"in": (w_in_a, l), "out": (w_out_a, l)})
        else:
            w.update({"in": (w_in_b, l - n_a), "out": (w_out_b, l - n_a)})
            if l == n_a:
                w["kv"] = (w_kv_shared, None)
        return w

    def cast_outside(arr, index):
        return (arr if index is None else arr[index]).astype(BF16)

    def hosted_casts(names, weights):
        slab_ok = lambda arr: arr.shape[-2] % (m // RET_ROW_TILE * BF16_SUBLANES) == 0
        return [n for n in names if slab_ok(weights[n][0])]

    h = x.reshape(m, d)
    kv3 = None
    ready = {}
    for l in range(depth):
        f32w = layer_weights(l)
        bf16w = dict(ready)
        ready = {}
        if l < n_a:
            bf16w["in"] = cast_outside(*f32w["in"])
            todo = {("own", n): f32w[n] for n in f32w if n not in bf16w}
            if l + 1 == n_a and l + 1 < depth:
                todo.update({("next", n): w for n, w in layer_weights(l + 1).items()})
            keys = hosted_casts(list(todo), todo)
            y, q_m, cast = _proj_retention(h, norm_pre_mix[l:l + 1], bf16w["in"], scale_a,
                                           row_vec(ret_norm_gain[l]), batch, seq,
                                           [todo[k] for k in keys])
            for (where, n), w in zip(keys, cast):
                (bf16w if where == "own" else ready)[n] = w
        for n in f32w:
            if n not in bf16w:
                bf16w[n] = cast_outside(*f32w[n])
        if l < n_a:
            h, hn = _mix_out(y, q_m, 0, mem, bf16w["mem_kv"], bf16w["out"], h,
                             row_vec(norm_post_mix[l]), row_vec(norm_pre_mlp[l]), batch, seq)
        else:
            if l == n_a:
                qproj, kv = _norm_proj(
                    h, jnp.stack([norm_pre_mix[l], kv_norm_gain]),
                    [bf16w["in"], bf16w["kv"]], [scale_b, scale_kv])
                kv3 = kv.reshape(batch, seq, 2 * moba_w)
            else:
                (qproj,) = _norm_proj(h, norm_pre_mix[l:l + 1], [bf16w["in"]], [scale_b])
            a = _moba(qproj.reshape(batch, seq, moba_w + mem_w), kv3)
            h, hn = _mix_out(a.reshape(m, moba_w), qproj, moba_w // mem_w, mem,
                             bf16w["mem_kv"], bf16w["out"], h,
                             row_vec(norm_post_mix[l]), row_vec(norm_pre_mlp[l]), batch, seq)
        h = _mlp(h, hn, row_vec(norm_post_mlp[l]), bf16w["up"], bf16w["down"])
    return h.reshape(batch, seq, d)
```
